```python
import jax, jax.numpy as jnp
from jax import lax
import numpy as np

D_MODEL = 4096
BATCH = 2
SEQ = 8192
DEPTH = 2

N_MIXERS = 2
N_A = (DEPTH + 1) // 2
N_B = DEPTH // 2

ML_HEADS = 8
ML_V_DIM = D_MODEL // ML_HEADS
ML_QK_DIM = ML_V_DIM // 2
ML_CHUNK = 64
GATE_SOFTCAP = 15.0
ML_IN_COLS = ML_HEADS * (2 * ML_QK_DIM + ML_V_DIM) + D_MODEL + 2 * ML_HEADS

MB_HEADS = 32
MB_HEAD_DIM = D_MODEL // MB_HEADS
MB_BLOCK = 256
MB_TOPK = 3
MB_QUERY_CHUNK = 16

D_FF = -(-(8 * D_MODEL // 3) // 256) * 256
CONV_WIDTH = 3
NORM_EPS = 1e-6

kernel_name = "hybrid_mlstm_moba_convffn"


def rmsnorm(x, g):
    xf = x.astype(jnp.float32)
    y = xf * lax.rsqrt(jnp.mean(xf * xf, axis=-1, keepdims=True) + NORM_EPS)
    return (y * g.astype(jnp.float32)).astype(x.dtype)


def mlstm_mixer(xn, w_in, gate_bias, head_gain, w_out):
    bsz, seq, _ = xn.shape
    H, dk, dv, L = ML_HEADS, ML_QK_DIM, ML_V_DIM, ML_CHUNK
    f32 = jnp.float32
    proj = xn @ w_in
    s1 = H * dk
    s2 = 2 * H * dk
    s3 = s2 + H * dv
    s4 = s3 + D_MODEL
    o = proj[..., s3:s4]

    def heads(t, d):
        return t.reshape(bsz, seq, H, d).transpose(0, 2, 1, 3).astype(f32)

    q = heads(proj[..., :s1], dk)
    k = heads(proj[..., s1:s2], dk) * (dk ** -0.5)
    v = heads(proj[..., s2:s3], dv)
    g = (proj[..., s4:] + gate_bias).astype(f32)
    g = GATE_SOFTCAP * jnp.tanh(g / GATE_SOFTCAP)
    i_pre = g[..., :H].transpose(0, 2, 1)
    log_f = jax.nn.log_sigmoid(g[..., H:]).transpose(0, 2, 1)

    nc = seq // L

    def to_chunks(t):
        t = t.reshape(t.shape[:2] + (nc, L) + t.shape[3:])
        return jnp.moveaxis(t, 2, 0)

    tril = jnp.tril(jnp.ones((L, L), dtype=bool))

    def step(carry, inp):
        C, n, m = carry
        qc, kc, vc, ic, lfc = inp
        b = jnp.cumsum(lfc, axis=-1)
        dmat = b[..., :, None] - b[..., None, :] + ic[..., None, :]
        dmat = jnp.where(tril, dmat, -jnp.inf)
        inter = b + m[..., None]
        m_row = jnp.maximum(jnp.max(dmat, axis=-1), inter)
        a_inter = jnp.exp(inter - m_row)
        s = jnp.einsum('bhjd,bhid->bhji', qc, kc) * jnp.exp(dmat - m_row[..., None])
        num = (a_inter[..., None] * jnp.einsum('bhjd,bhde->bhje', qc, C)
               + jnp.einsum('bhji,bhie->bhje', s, vc))
        den = a_inter * jnp.einsum('bhjd,bhd->bhj', qc, n) + jnp.sum(s, axis=-1)
        h = num / jnp.maximum(jnp.abs(den), jnp.exp(-m_row))[..., None]
        b_last = b[..., -1]
        dec_i = b_last[..., None] - b + ic
        m_new = jnp.maximum(b_last + m, jnp.max(dec_i, axis=-1))
        w_i = jnp.exp(dec_i - m_new[..., None])
        a_old = jnp.exp(b_last + m - m_new)
        kw = kc * w_i[..., None]
        C_new = a_old[..., None, None] * C + jnp.einsum('bhid,bhie->bhde', kw, vc)
        n_new = a_old[..., None] * n + jnp.sum(kw, axis=2)
        return (C_new, n_new, m_new), h

    init = (jnp.zeros((bsz, H, dk, dv), f32), jnp.zeros((bsz, H, dk), f32),
            jnp.zeros((bsz, H), f32))
    _, hs = lax.scan(step, init, (to_chunks(q), to_chunks(k), to_chunks(v),
                                  to_chunks(i_pre), to_chunks(log_f)))
    h = jnp.moveaxis(hs, 0, 2).reshape(bsz, H, seq, dv)
    h = h * lax.rsqrt(jnp.mean(h * h, axis=-1, keepdims=True) + NORM_EPS)
    h = h.transpose(0, 2, 1, 3).reshape(bsz, seq, D_MODEL) * head_gain.astype(f32)
    h = h * jax.nn.sigmoid(o.astype(f32))
    return h.astype(xn.dtype) @ w_out


def moba_mixer(xn, w_qkv, w_out):
    bsz, seq, _ = xn.shape
    H, dh, bs = MB_HEADS, MB_HEAD_DIM, MB_BLOCK
    f32 = jnp.float32
    q, k, v = jnp.split(xn @ w_qkv, 3, axis=-1)

    def heads(t):
        return t.reshape(bsz, seq, H, dh).transpose(0, 2, 1, 3)

    q, k, v = heads(q), heads(k), heads(v)
    n_blk = -(-seq // bs)
    pad = n_blk * bs - seq
    kp = jnp.pad(k, ((0, 0), (0, 0), (0, pad), (0, 0)))
    vp = jnp.pad(v, ((0, 0), (0, 0), (0, pad), (0, 0)))
    kb = kp.reshape(bsz, H, n_blk, bs, dh)
    vb = vp.reshape(bsz, H, n_blk, bs, dh)
    k_mean = jnp.mean(kb.astype(f32), axis=3)
    gate = jnp.einsum('bhsd,bhnd->bhsn', q.astype(f32), k_mean)
    q_blk = jnp.arange(seq) // bs
    fully_past = jnp.arange(n_blk)[None, :] < q_blk[:, None]
    gate = jnp.where(fully_past, gate, -jnp.inf)
    k_sel = min(MB_TOPK, n_blk)
    _, sel_idx = lax.top_k(gate, k_sel)
    sel_valid = jnp.arange(k_sel)[None, :] < q_blk[:, None]

    qc_len = MB_QUERY_CHUNK
    nq = seq // qc_len
    q_ch = jnp.moveaxis(q.reshape(bsz, H, nq, qc_len, dh), 2, 0)
    idx_ch = jnp.moveaxis(sel_idx.reshape(bsz, H, nq, qc_len, k_sel), 2, 0)
    valid_ch = sel_valid.reshape(nq, qc_len, k_sel)
    gather = jax.vmap(jax.vmap(lambda blocks, ix: blocks[ix]))
    scale = dh ** -0.5

    def attend(args):
        ci, qc, ic, vm = args
        t0 = ci * qc_len
        own = t0 // bs
        k_g = gather(kb, ic)
        v_g = gather(vb, ic)
        s_sel = jnp.einsum('bhqd,bhqjpd->bhqjp', qc, k_g).astype(f32) * scale
        s_sel = jnp.where(vm[None, None, :, :, None], s_sel, -jnp.inf)
        s_sel = s_sel.reshape(bsz, H, qc_len, k_sel * bs)
        k_own = lax.dynamic_slice_in_dim(kp, own * bs, bs, axis=2)
        v_own = lax.dynamic_slice_in_dim(vp, own * bs, bs, axis=2)
        s_own = jnp.einsum('bhqd,bhpd->bhqp', qc, k_own).astype(f32) * scale
        causal = (own * bs + jnp.arange(bs))[None, :] <= (t0 + jnp.arange(qc_len))[:, None]
        s_own = jnp.where(causal, s_own, -jnp.inf)
        p = jax.nn.softmax(jnp.concatenate([s_sel, s_own], axis=-1), axis=-1)
        p_sel = p[..., :k_sel * bs].reshape(bsz, H, qc_len, k_sel, bs)
        p_own = p[..., k_sel * bs:]
        out = (jnp.einsum('bhqjp,bhqjpe->bhqe', p_sel, v_g.astype(f32))
               + jnp.einsum('bhqp,bhpe->bhqe', p_own, v_own.astype(f32)))
        return out.astype(qc.dtype)

    out = lax.map(attend, (jnp.arange(nq), q_ch, idx_ch, valid_ch))
    out = jnp.moveaxis(out, 0, 2).reshape(bsz, H, seq, dh)
    out = out.transpose(0, 2, 1, 3).reshape(bsz, seq, D_MODEL)
    return out @ w_out


def conv_ffn(xn, w_up, conv_w, conv_b, w_down):
    seq = xn.shape[1]
    h = xn @ w_up
    hp = jnp.pad(h, ((0, 0), (CONV_WIDTH - 1, 0), (0, 0)))
    h = conv_b + sum(conv_w[j] * hp[:, j:j + seq] for j in range(CONV_WIDTH))
    gate, up = jnp.split(h, 2, axis=-1)
    return (jax.nn.silu(gate) * up) @ w_down


def setup_inputs(seed: int = 0) -> dict:
    key = jax.random.key(seed)
    ks = jax.random.split(key, 16)
    f32 = jnp.float32
    nrm = jax.random.normal
    D, F2 = D_MODEL, 2 * D_FF
    x = nrm(ks[0], (BATCH, SEQ, D), f32)
    norm_mix = 1.0 + 0.02 * nrm(ks[1], (DEPTH, D), f32)
    norm_ffn = 1.0 + 0.02 * nrm(ks[2], (DEPTH, D), f32)
    a_w_in = nrm(ks[3], (N_A, D, ML_IN_COLS), f32) * D ** -0.5
    kb1, kb2 = jax.random.split(ks[4])
    a_gate_bias = jnp.concatenate([
        0.1 * nrm(kb1, (N_A, ML_HEADS), f32),
        3.0 + 0.1 * nrm(kb2, (N_A, ML_HEADS), f32)], axis=-1)
    a_head_norm = 1.0 + 0.02 * nrm(ks[5], (N_A, D), f32)
    a_w_out = nrm(ks[6], (N_A, D, D), f32) * D ** -0.5
    b_w_qkv = nrm(ks[7], (N_B, D, 3 * D), f32) * D ** -0.5
    b_w_out = nrm(ks[8], (N_B, D, D), f32) * D ** -0.5
    ffn_w_up = nrm(ks[9], (DEPTH, D, F2), f32) * D ** -0.5
    ffn_conv_w = nrm(ks[10], (DEPTH, CONV_WIDTH, F2), f32) * CONV_WIDTH ** -0.5
    ffn_conv_b = 0.01 * nrm(ks[11], (DEPTH, F2), f32)
    ffn_w_down = nrm(ks[12], (DEPTH, D_FF, D), f32) * D_FF ** -0.5
    final_norm = 1.0 + 0.02 * nrm(ks[13], (D,), f32)
    return {"x": x, "norm_mix": norm_mix, "norm_ffn": norm_ffn,
            "a_w_in": a_w_in, "a_gate_bias": a_gate_bias, "a_head_norm": a_head_norm,
            "a_w_out": a_w_out, "b_w_qkv": b_w_qkv, "b_w_out": b_w_out,
            "ffn_w_up": ffn_w_up, "ffn_conv_w": ffn_conv_w, "ffn_conv_b": ffn_conv_b,
            "ffn_w_down": ffn_w_down, "final_norm": final_norm}


def reference(x, norm_mix, norm_ffn, a_w_in, a_gate_bias, a_head_norm, a_w_out,
              b_w_qkv, b_w_out, ffn_w_up, ffn_conv_w, ffn_conv_b, ffn_w_down, final_norm):
    for i in range(DEPTH):
        h = rmsnorm(x, norm_mix[i])
        j = i // N_MIXERS
        if i % N_MIXERS == 0:
            x = x + mlstm_mixer(h, a_w_in[j], a_gate_bias[j], a_head_norm[j], a_w_out[j])
        else:
            x = x + moba_mixer(h, b_w_qkv[j], b_w_out[j])
        h = rmsnorm(x, norm_ffn[i])
        x = x + conv_ffn(h, ffn_w_up[i], ffn_conv_w[i], ffn_conv_b[i], ffn_w_down[i])
    return rmsnorm(x, final_norm)
```

```python
import functools

import jax
import jax.numpy as jnp
from jax import lax
from jax.experimental import pallas as pl
from jax.experimental.pallas import tpu as pltpu

F32 = jnp.float32
BF16 = jnp.bfloat16

NORM_EPS = 1e-6
ML_HEADS = 8
GATE_SOFTCAP = 15.0
MB_HEADS = 32
MB_BLOCK = 256
MB_TOPK = 3
CONV_WIDTH = 3

V7X_LANES = 128
V7X_BF16_SUBLANES = 16
V7X_VMEM_BYTES = 64 * 1024 * 1024
VMEM_RESERVE_BYTES = 6 * 1024 * 1024

ML_CHUNK = 256
NORM_ROWS = 256
MM_ROWS = 1024
MM_COLS = 512
DOWN_ROWS = 512
DOWN_COLS = 256
FFN_COLS = 256
CONV_HALO = V7X_BF16_SUBLANES


def _vmem_limit(block_bytes):
    want = int(block_bytes) + VMEM_RESERVE_BYTES
    return max(min(want, V7X_VMEM_BYTES - VMEM_RESERVE_BYTES), 16 * 1024 * 1024)


def _params(sem, block_bytes):
    return pltpu.CompilerParams(dimension_semantics=sem, vmem_limit_bytes=_vmem_limit(block_bytes))


def _sigmoid(x):
    return 1.0 / (1.0 + jnp.exp(-x))


def _rmsnorm_kernel(x_ref, g_ref, o_ref):
    x = x_ref[...]
    ms = jnp.mean(x * x, axis=-1, keepdims=True)
    o_ref[...] = (x * lax.rsqrt(ms + NORM_EPS) * g_ref[...]).astype(o_ref.dtype)


def _rmsnorm(x, g, out_dtype):
    m, d = x.shape
    tm = min(NORM_ROWS, m)
    assert m % tm == 0
    blk = 2 * tm * d * (4 + jnp.dtype(out_dtype).itemsize) + 3 * tm * d * 4
    return pl.pallas_call(
        _rmsnorm_kernel,
        out_shape=jax.ShapeDtypeStruct((m, d), out_dtype),
        grid=(m // tm,),
        in_specs=[pl.BlockSpec((tm, d), lambda i: (i, 0)), pl.BlockSpec((1, d), lambda i: (0, 0))],
        out_specs=pl.BlockSpec((tm, d), lambda i: (i, 0)),
        compiler_params=_params(("arbitrary",), blk),
        name="rmsnorm",
    )(x, g.reshape(1, d))


def _rmsnorm_gates_kernel(x_ref, g_ref, wg_ref, b_ref, o_ref, gate_ref):
    x = x_ref[...]
    ms = jnp.mean(x * x, axis=-1, keepdims=True)
    y = x * lax.rsqrt(ms + NORM_EPS) * g_ref[...]
    o_ref[...] = y.astype(o_ref.dtype)
    gate_ref[...] = jnp.dot(y, wg_ref[...], precision=lax.Precision.HIGHEST,
                            preferred_element_type=F32) + b_ref[...]


def _rmsnorm_gates(x, g, w_gates, bias):
    m, d = x.shape
    tm = min(NORM_ROWS, m)
    assert m % tm == 0 and w_gates.shape == (d, V7X_LANES)
    blk = 2 * tm * d * 6 + 2 * d * V7X_LANES * 4 + 4 * tm * d * 4
    return pl.pallas_call(
        _rmsnorm_gates_kernel,
        out_shape=(jax.ShapeDtypeStruct((m, d), BF16), jax.ShapeDtypeStruct((m, V7X_LANES), F32)),
        grid=(m // tm,),
        in_specs=[pl.BlockSpec((tm, d), lambda i: (i, 0)), pl.BlockSpec((1, d), lambda i: (0, 0)),
                  pl.BlockSpec((d, V7X_LANES), lambda i: (0, 0)), pl.BlockSpec((1, V7X_LANES), lambda i: (0, 0))],
        out_specs=(pl.BlockSpec((tm, d), lambda i: (i, 0)), pl.BlockSpec((tm, V7X_LANES), lambda i: (i, 0))),
        compiler_params=_params(("arbitrary",), blk),
        name="rmsnorm_gates",
    )(x, g.reshape(1, d), w_gates, bias)


def _mm_kernel(a_ref, b_ref, o_ref):
    o_ref[...] = jnp.dot(a_ref[...], b_ref[...], preferred_element_type=F32).astype(o_ref.dtype)


def _mm_res_kernel(a_ref, b_ref, r_ref, o_ref):
    acc = jnp.dot(a_ref[...], b_ref[...], preferred_element_type=F32)
    o_ref[...] = (acc + r_ref[...]).astype(o_ref.dtype)


def _matmul(a, b, out_dtype, residual=None, rows=MM_ROWS, cols=MM_COLS):
    m, k = a.shape
    n = b.shape[1]
    tm, tn = min(rows, m), min(cols, n)
    assert m % tm == 0 and n % tn == 0
    osz = jnp.dtype(out_dtype).itemsize
    blk = 2 * (tm * k * 2 + k * tn * 2 + tm * tn * osz) + tm * tn * 4
    in_specs = [pl.BlockSpec((tm, k), lambda i, j: (i, 0)), pl.BlockSpec((k, tn), lambda i, j: (0, j))]
    args = [a, b]
    kern = _mm_kernel
    if residual is not None:
        in_specs.append(pl.BlockSpec((tm, tn), lambda i, j: (i, j)))
        args.append(residual)
        kern = _mm_res_kernel
        blk += 2 * tm * tn * 4
    return pl.pallas_call(
        kern,
        out_shape=jax.ShapeDtypeStruct((m, n), out_dtype),
        grid=(m // tm, n // tn),
        in_specs=in_specs,
        out_specs=pl.BlockSpec((tm, tn), lambda i, j: (i, j)),
        compiler_params=_params(("arbitrary", "arbitrary"), blk),
        name="matmul_res" if residual is not None else "matmul",
    )(*args)


def _mlstm_kernel(q_ref, k_ref, v_ref, o_ref, g_ref, gain_ref, out_ref, c_ref, n_ref, m_ref, *, dk, dv):
    L = q_ref.shape[0]
    heads = ML_HEADS

    @pl.when(pl.program_id(1) == 0)
    def _():
        c_ref[...] = jnp.zeros_like(c_ref)
        n_ref[...] = jnp.zeros_like(n_ref)
        m_ref[...] = jnp.zeros_like(m_ref)

    rows = lax.broadcasted_iota(jnp.int32, (L, L), 0)
    cols = lax.broadcasted_iota(jnp.int32, (L, L), 1)
    causal = cols <= rows
    tril = causal.astype(F32)

    g = g_ref[...]
    gcap = GATE_SOFTCAP * jnp.tanh(g / GATE_SOFTCAP)
    log_f = jnp.minimum(gcap, 0.0) - jnp.log(1.0 + jnp.exp(-jnp.abs(gcap)))
    bcum = jnp.dot(tril, log_f, precision=lax.Precision.HIGHEST, preferred_element_type=F32)
    lane = lax.broadcasted_iota(jnp.int32, g.shape, 1)
    gates = jnp.where(lane < heads, gcap, bcum)
    gates_t = gates.T

    scale = dk ** -0.5
    nt = (((1,), (1,)), ((), ()))
    tn = (((0,), (0,)), ((), ()))
    for hd in range(heads):
        q = q_ref[:, hd * dk:(hd + 1) * dk]
        k = k_ref[:, hd * dk:(hd + 1) * dk]
        v = v_ref[:, hd * dv:(hd + 1) * dv]
        i_col = gates[:, hd:hd + 1]
        b_col = gates[:, heads + hd:heads + hd + 1]
        i_row = gates_t[hd:hd + 1, :]
        b_row = gates_t[heads + hd:heads + hd + 1, :]
        m_prev = m_ref[hd, :, 0:1]
        c_prev = c_ref[hd]
        n_prev = n_ref[hd]

        dmat = jnp.where(causal, b_col - b_row + i_row, -jnp.inf)
        inter = b_col + m_prev
        m_row = jnp.maximum(jnp.max(dmat, axis=-1, keepdims=True), inter)
        a_inter = jnp.exp(inter - m_row)
        s = lax.dot_general(q, k, nt, preferred_element_type=F32) * scale * jnp.exp(dmat - m_row)
        num = (a_inter * jnp.dot(q, c_prev.astype(BF16), preferred_element_type=F32)
               + jnp.dot(s.astype(BF16), v, preferred_element_type=F32))
        den = (a_inter * jnp.sum(q.astype(F32) * n_prev, axis=-1, keepdims=True)
               + jnp.sum(s, axis=-1, keepdims=True))
        h = num / jnp.maximum(jnp.abs(den), jnp.exp(-m_row))

        b_last = b_col[L - 1:L, :]
        dec = b_last - b_col + i_col
        m_new = jnp.maximum(b_last + m_prev, jnp.max(dec, axis=0, keepdims=True))
        a_old = jnp.exp(b_last + m_prev - m_new)
        kw = k.astype(F32) * (jnp.exp(dec - m_new) * scale)
        c_ref[hd] = a_old * c_prev + lax.dot_general(kw.astype(BF16), v, tn, preferred_element_type=F32)
        n_ref[hd] = a_old * n_prev + jnp.sum(kw, axis=0, keepdims=True)
        m_ref[hd] = jnp.broadcast_to(m_new, m_ref.shape[1:])

        h = h * lax.rsqrt(jnp.mean(h * h, axis=-1, keepdims=True) + NORM_EPS)
        h = h * gain_ref[:, hd * dv:(hd + 1) * dv]
        h = h * _sigmoid(o_ref[:, hd * dv:(hd + 1) * dv].astype(F32))
        out_ref[:, hd * dv:(hd + 1) * dv] = h.astype(out_ref.dtype)


def _mlstm_scan(proj, gates, head_gain, bsz, seq):
    m, width = proj.shape
    d = head_gain.shape[0]
    dv = d // ML_HEADS
    dk = dv // 2
    hk = ML_HEADS * dk
    assert width == 2 * hk + 2 * d and hk * 2 == d
    L = min(ML_CHUNK, seq)
    assert seq % L == 0
    nc = seq // L
    row = lambda b, c: b * nc + c
    blk = 2 * (L * (2 * hk + 2 * d) * 2 + L * V7X_LANES * 4 + d * 4 + L * d * 2) \
        + ML_HEADS * dk * dv * 4 + 16 * L * L * 4 + 8 * L * dv * 4
    return pl.pallas_call(
        functools.partial(_mlstm_kernel, dk=dk, dv=dv),
        out_shape=jax.ShapeDtypeStruct((m, d), BF16),
        grid=(bsz, nc),
        in_specs=[pl.BlockSpec((L, hk), lambda b, c: (row(b, c), 0)),
                  pl.BlockSpec((L, hk), lambda b, c: (row(b, c), 1)),
                  pl.BlockSpec((L, d), lambda b, c: (row(b, c), 1)),
                  pl.BlockSpec((L, d), lambda b, c: (row(b, c), 2)),
                  pl.BlockSpec((L, V7X_LANES), lambda b, c: (row(b, c), 0)),
                  pl.BlockSpec((1, d), lambda b, c: (0, 0))],
        out_specs=pl.BlockSpec((L, d), lambda b, c: (row(b, c), 0)),
        scratch_shapes=[pltpu.VMEM((ML_HEADS, dk, dv), F32),
                        pltpu.VMEM((ML_HEADS, 1, dk), F32),
                        pltpu.VMEM((ML_HEADS, 1, V7X_LANES), F32)],
        compiler_params=_params(("arbitrary", "arbitrary"), blk),
        name="mlstm_scan",
    )(proj, proj, proj, proj, gates, head_gain.reshape(1, d))


def _moba_kernel(q_ref, k_ref, v_ref, o_ref, vt_ref, km_ref, sel_ref, *, nb, bs, dh):
    qb = pl.program_id(2)
    dh_ext = vt_ref.shape[1]
    nt = (((1,), (1,)), ((), ()))

    @pl.when(qb == 0)
    def _():
        def prep(n, carry):
            r0 = pl.multiple_of(n * bs, bs)
            vb = v_ref[pl.ds(r0, bs), :].astype(F32)
            vt_ref[n, 0:dh, :] = vb.T.astype(BF16)
            vt_ref[n, dh:dh_ext, :] = jnp.ones((dh_ext - dh, bs), BF16)
            kb = k_ref[pl.ds(r0, bs), :].astype(F32)
            km_ref[pl.ds(n, 1), :] = jnp.mean(kb, axis=0, keepdims=True)
            return carry
        lax.fori_loop(0, nb, prep, 0)

    q = q_ref[...]

    km = km_ref[...]
    km_hi = km.astype(BF16)
    r1 = km - km_hi.astype(F32)
    km_mid = r1.astype(BF16)
    km_lo = (r1 - km_mid.astype(F32)).astype(BF16)
    gate = (lax.dot_general(km_hi, q, nt, preferred_element_type=F32)
            + lax.dot_general(km_mid, q, nt, preferred_element_type=F32)
            + lax.dot_general(km_lo, q, nt, preferred_element_type=F32))

    blk_idx = lax.broadcasted_iota(jnp.int32, (nb, bs), 0)
    removed = blk_idx >= qb
    sel = jnp.zeros((nb, bs), F32)
    for r in range(MB_TOPK):
        gm = jnp.where(removed, -jnp.inf, gate)
        mx = jnp.max(gm, axis=0, keepdims=True)
        cand = jnp.logical_and(jnp.logical_not(removed), gm == mx)
        idx = jnp.min(jnp.where(cand, blk_idx, nb), axis=0, keepdims=True)
        hit = blk_idx == idx
        sel = jnp.where(jnp.logical_and(hit, qb > r), 1.0, sel)
        removed = jnp.logical_or(removed, hit)
    sel_ref[...] = sel

    scale = dh ** -0.5

    def scores(n):
        kb = k_ref[pl.ds(pl.multiple_of(n * bs, bs), bs), :]
        return lax.dot_general(kb, q, nt, preferred_element_type=F32) * scale

    kidx = lax.broadcasted_iota(jnp.int32, (bs, bs), 0)
    qidx = lax.broadcasted_iota(jnp.int32, (bs, bs), 1)
    s = jnp.where(kidx <= qidx, scores(qb), -jnp.inf)
    m0 = jnp.max(s, axis=0, keepdims=True)
    p = jnp.exp(s - m0)
    acc0 = jnp.dot(vt_ref[qb], p.astype(BF16), preferred_element_type=F32)

    def body(n, carry):
        m_run, acc = carry
        s = jnp.where(sel_ref[pl.ds(n, 1), :] > 0.0, scores(n), -jnp.inf)
        m_new = jnp.maximum(m_run, jnp.max(s, axis=0, keepdims=True))
        alpha = jnp.exp(m_run - m_new)
        p = jnp.exp(s - m_new)
        acc = acc * alpha + jnp.dot(vt_ref[n], p.astype(BF16), preferred_element_type=F32)
        return m_new, acc

    _, acc = lax.fori_loop(0, qb, body, (m0, acc0))
    out = acc[0:dh, :] / acc[dh:dh + 1, :]
    o_ref[...] = out.T.astype(o_ref.dtype)


def _moba_attention(qkv, bsz, seq):
    m, width = qkv.shape
    d = width // 3
    dh = d // MB_HEADS
    bs = MB_BLOCK
    assert seq % bs == 0 and dh % V7X_LANES == 0
    nb = seq // bs
    dh_ext = dh + V7X_BF16_SUBLANES
    blk = 2 * (bs * dh * 2 + 2 * seq * dh * 2 + bs * dh * 2) + nb * dh_ext * bs * 2 + 24 * bs * bs * 4
    return pl.pallas_call(
        functools.partial(_moba_kernel, nb=nb, bs=bs, dh=dh),
        out_shape=jax.ShapeDtypeStruct((m, d), BF16),
        grid=(bsz, MB_HEADS, nb),
        in_specs=[pl.BlockSpec((bs, dh), lambda b, h, i: (b * nb + i, h)),
                  pl.BlockSpec((seq, dh), lambda b, h, i: (b, MB_HEADS + h)),
                  pl.BlockSpec((seq, dh), lambda b, h, i: (b, 2 * MB_HEADS + h))],
        out_specs=pl.BlockSpec((bs, dh), lambda b, h, i: (b * nb + i, h)),
        scratch_shapes=[pltpu.VMEM((nb, dh_ext, bs), BF16),
                        pltpu.VMEM((nb, dh), F32),
                        pltpu.VMEM((nb, bs), F32)],
        compiler_params=_params(("arbitrary", "arbitrary", "arbitrary"), blk),
        name="moba_attention",
    )(qkv, qkv, qkv)


def _ffn_up_kernel(xprev_ref, x_ref, wg_ref, wu_ref, cwg_ref, cwu_ref, cbg_ref, cbu_ref, o_ref,
                   xcat_ref, hg_ref, hu_ref, *, tiles_per_seq):
    tm = x_ref.shape[0]

    @pl.when(pl.program_id(1) == 0)
    def _():
        prev = xprev_ref[...]
        seq_start = (pl.program_id(0) % tiles_per_seq) == 0
        xcat_ref[0:CONV_HALO, :] = jnp.where(seq_start, jnp.zeros_like(prev), prev)
        xcat_ref[CONV_HALO:, :] = x_ref[...]

    xc = xcat_ref[...]
    hg_ref[...] = jnp.dot(xc, wg_ref[...], preferred_element_type=F32)
    hu_ref[...] = jnp.dot(xc, wu_ref[...], preferred_element_type=F32)

    def conv(h_ref, cw_ref, cb_ref):
        acc = cb_ref[...]
        for t in range(CONV_WIDTH):
            acc = acc + cw_ref[t:t + 1, :] * h_ref[pl.ds(CONV_HALO - (CONV_WIDTH - 1) + t, tm), :]
        return acc

    gate = conv(hg_ref, cwg_ref, cbg_ref)
    up = conv(hu_ref, cwu_ref, cbu_ref)
    o_ref[...] = (gate * _sigmoid(gate) * up).astype(o_ref.dtype)


def _ffn_up(h, w_up, conv_w, conv_b, seq):
    m, d = h.shape
    f = w_up.shape[1] // 2
    tm = min(MM_ROWS, seq)
    tn = min(FFN_COLS, f)
    assert seq % tm == 0 and f % tn == 0 and tm % CONV_HALO == 0
    nj = f // tn
    halo_blocks = tm // CONV_HALO
    blk = (2 * (tm * d * 2 + CONV_HALO * d * 2 + 2 * d * tn * 2 + tm * tn * 2)
           + (tm + CONV_HALO) * d * 2 + 2 * (tm + CONV_HALO) * tn * 4 + 6 * tm * tn * 4)
    return pl.pallas_call(
        functools.partial(_ffn_up_kernel, tiles_per_seq=seq // tm),
        out_shape=jax.ShapeDtypeStruct((m, f), BF16),
        grid=(m // tm, nj),
        in_specs=[pl.BlockSpec((CONV_HALO, d), lambda i, j: (jnp.maximum(i * halo_blocks - 1, 0), 0)),
                  pl.BlockSpec((tm, d), lambda i, j: (i, 0)),
                  pl.BlockSpec((d, tn), lambda i, j: (0, j)),
                  pl.BlockSpec((d, tn), lambda i, j: (0, nj + j)),
                  pl.BlockSpec((CONV_WIDTH, tn), lambda i, j: (0, j)),
                  pl.BlockSpec((CONV_WIDTH, tn), lambda i, j: (0, nj + j)),
                  pl.BlockSpec((1, tn), lambda i, j: (0, j)),
                  pl.BlockSpec((1, tn), lambda i, j: (0, nj + j))],
        out_specs=pl.BlockSpec((tm, tn), lambda i, j: (i, j)),
        scratch_shapes=[pltpu.VMEM((tm + CONV_HALO, d), BF16),
                        pltpu.VMEM((tm + CONV_HALO, tn), F32),
                        pltpu.VMEM((tm + CONV_HALO, tn), F32)],
        compiler_params=_params(("arbitrary", "arbitrary"), blk),
        name="ffn_up_conv_gate",
    )(h, h, w_up, w_up, conv_w, conv_w, conv_b.reshape(1, 2 * f), conv_b.reshape(1, 2 * f))


def _conv_ffn(x, norm_g, w_up, conv_w, conv_b, w_down, seq):
    h = _rmsnorm(x, norm_g, BF16)
    act = _ffn_up(h, w_up.astype(BF16), conv_w, conv_b, seq)
    return _matmul(act, w_down.astype(BF16), F32, residual=x, rows=DOWN_ROWS, cols=DOWN_COLS)


def kernel(x, norm_mix, norm_ffn, a_w_in, a_gate_bias, a_head_norm, a_w_out, b_w_qkv, b_w_out,
           ffn_w_up, ffn_conv_w, ffn_conv_b, ffn_w_down, final_norm):
    bsz, seq, d = x.shape
    m = bsz * seq
    x = x.reshape(m, d)

    w_in = a_w_in[0]
    n_main = w_in.shape[1] - 2 * ML_HEADS
    w_gates = jnp.pad(w_in[:, n_main:], ((0, 0), (0, V7X_LANES - 2 * ML_HEADS)))
    gate_bias = jnp.pad(a_gate_bias[0], (0, V7X_LANES - 2 * ML_HEADS)).reshape(1, V7X_LANES)
    h, gates = _rmsnorm_gates(x, norm_mix[0], w_gates, gate_bias)
    proj = _matmul(h, w_in[:, :n_main].astype(BF16), BF16)
    mixed = _mlstm_scan(proj, gates, a_head_norm[0], bsz, seq)
    x = _matmul(mixed, a_w_out[0].astype(BF16), F32, residual=x)
    x = _conv_ffn(x, norm_ffn[0], ffn_w_up[0], ffn_conv_w[0], ffn_conv_b[0], ffn_w_down[0], seq)

    h = _rmsnorm(x, norm_mix[1], BF16)
    qkv = _matmul(h, b_w_qkv[0].astype(BF16), BF16)
    attn = _moba_attention(qkv, bsz, seq)
    x = _matmul(attn, b_w_out[0].astype(BF16), F32, residual=x)
    x = _conv_ffn(x, norm_ffn[1], ffn_w_up[1], ffn_conv_w[1], ffn_conv_b[1], ffn_w_down[1], seq)

    return _rmsnorm(x, final_norm, F32).reshape(bsz, seq, d)
```

```python
import functools
import math

import jax
import jax.numpy as jnp
from jax import lax
from jax.experimental import pallas as pl
from jax.experimental.pallas import tpu as pltpu

F32 = jnp.float32
BF16 = jnp.bfloat16

NORM_EPS = 1e-6
ML_HEADS = 8
GATE_SOFTCAP = 15.0
MB_HEADS = 32
MB_BLOCK = 256
MB_TOPK = 3
CONV_WIDTH = 3

V7X_LANES = 128
V7X_BF16_SUBLANES = 16
V7X_VMEM_BYTES = 64 * 1024 * 1024
VMEM_RESERVE_BYTES = 6 * 1024 * 1024

ML_CHUNK = 256
NORM_ROWS = 256
MM_ROWS = 1024
MM_COLS = 512
DOWN_ROWS = 512
DOWN_COLS = 256
FFN_COLS = 256
CONV_HALO = V7X_BF16_SUBLANES
MB_GROUP = 4
MB_HEADS_PER_STEP = 2

LOG2_E = math.log2(math.e)


def _vmem_limit(block_bytes):
    want = int(block_bytes) + VMEM_RESERVE_BYTES
    return max(min(want, V7X_VMEM_BYTES - VMEM_RESERVE_BYTES), 16 * 1024 * 1024)


def _params(sem, block_bytes):
    return pltpu.CompilerParams(dimension_semantics=sem, vmem_limit_bytes=_vmem_limit(block_bytes))


def _sigmoid(x):
    return 1.0 / (1.0 + jnp.exp(-x))


def _rmsnorm_kernel(x_ref, g_ref, o_ref):
    x = x_ref[...]
    ms = jnp.mean(x * x, axis=-1, keepdims=True)
    o_ref[...] = (x * lax.rsqrt(ms + NORM_EPS) * g_ref[...]).astype(o_ref.dtype)


def _rmsnorm(x, g, out_dtype):
    m, d = x.shape
    tm = min(NORM_ROWS, m)
    assert m % tm == 0
    blk = 2 * tm * d * (4 + jnp.dtype(out_dtype).itemsize) + 3 * tm * d * 4
    return pl.pallas_call(
        _rmsnorm_kernel,
        out_shape=jax.ShapeDtypeStruct((m, d), out_dtype),
        grid=(m // tm,),
        in_specs=[pl.BlockSpec((tm, d), lambda i: (i, 0)), pl.BlockSpec((1, d), lambda i: (0, 0))],
        out_specs=pl.BlockSpec((tm, d), lambda i: (i, 0)),
        compiler_params=_params(("arbitrary",), blk),
        name="rmsnorm",
    )(x, g.reshape(1, d))


def _rmsnorm_gates_kernel(x_ref, g_ref, wg_ref, b_ref, o_ref, gate_ref):
    x = x_ref[...]
    ms = jnp.mean(x * x, axis=-1, keepdims=True)
    y = x * lax.rsqrt(ms + NORM_EPS) * g_ref[...]
    o_ref[...] = y.astype(o_ref.dtype)
    gate_ref[...] = jnp.dot(y, wg_ref[...], precision=lax.Precision.HIGHEST,
                            preferred_element_type=F32) + b_ref[...]


def _rmsnorm_gates(x, g, w_gates, bias):
    m, d = x.shape
    tm = min(NORM_ROWS, m)
    assert m % tm == 0 and w_gates.shape == (d, V7X_LANES)
    blk = 2 * tm * d * 6 + 2 * d * V7X_LANES * 4 + 4 * tm * d * 4
    return pl.pallas_call(
        _rmsnorm_gates_kernel,
        out_shape=(jax.ShapeDtypeStruct((m, d), BF16), jax.ShapeDtypeStruct((m, V7X_LANES), F32)),
        grid=(m // tm,),
        in_specs=[pl.BlockSpec((tm, d), lambda i: (i, 0)), pl.BlockSpec((1, d), lambda i: (0, 0)),
                  pl.BlockSpec((d, V7X_LANES), lambda i: (0, 0)), pl.BlockSpec((1, V7X_LANES), lambda i: (0, 0))],
        out_specs=(pl.BlockSpec((tm, d), lambda i: (i, 0)), pl.BlockSpec((tm, V7X_LANES), lambda i: (i, 0))),
        compiler_params=_params(("arbitrary",), blk),
        name="rmsnorm_gates",
    )(x, g.reshape(1, d), w_gates, bias)


def _mm_kernel(a_ref, b_ref, o_ref):
    o_ref[...] = jnp.dot(a_ref[...], b_ref[...], preferred_element_type=F32).astype(o_ref.dtype)


def _mm_res_kernel(a_ref, b_ref, r_ref, o_ref):
    acc = jnp.dot(a_ref[...], b_ref[...], preferred_element_type=F32)
    o_ref[...] = (acc + r_ref[...]).astype(o_ref.dtype)


def _matmul(a, b, out_dtype, residual=None, n=None, rows=MM_ROWS, cols=MM_COLS):
    m, k = a.shape
    n = b.shape[1] if n is None else n
    tm, tn = min(rows, m), min(cols, n)
    assert m % tm == 0 and n % tn == 0 and n <= b.shape[1]
    osz = jnp.dtype(out_dtype).itemsize
    blk = 2 * (tm * k * 2 + k * tn * 2 + tm * tn * osz) + tm * tn * 4
    in_specs = [pl.BlockSpec((tm, k), lambda i, j: (i, 0)), pl.BlockSpec((k, tn), lambda i, j: (0, j))]
    args = [a, b]
    kern = _mm_kernel
    if residual is not None:
        in_specs.append(pl.BlockSpec((tm, tn), lambda i, j: (i, j)))
        args.append(residual)
        kern = _mm_res_kernel
        blk += 2 * tm * tn * 4
    return pl.pallas_call(
        kern,
        out_shape=jax.ShapeDtypeStruct((m, n), out_dtype),
        grid=(m // tm, n // tn),
        in_specs=in_specs,
        out_specs=pl.BlockSpec((tm, tn), lambda i, j: (i, j)),
        compiler_params=_params(("arbitrary", "arbitrary"), blk),
        name="matmul_res" if residual is not None else "matmul",
    )(*args)


def _mlstm_kernel(q_ref, k_ref, v_ref, o_ref, g_ref, gain_ref, out_ref, c_ref, n_ref, m_ref, *, dk, dv):
    L = q_ref.shape[0]
    heads = ML_HEADS

    @pl.when(pl.program_id(1) == 0)
    def _():
        c_ref[...] = jnp.zeros_like(c_ref)
        n_ref[...] = jnp.zeros_like(n_ref)
        m_ref[...] = jnp.zeros_like(m_ref)

    rows = lax.broadcasted_iota(jnp.int32, (L, L), 0)
    cols = lax.broadcasted_iota(jnp.int32, (L, L), 1)
    causal = cols <= rows
    tril = causal.astype(F32)

    g = g_ref[...]
    gcap = GATE_SOFTCAP * jnp.tanh(g / GATE_SOFTCAP)
    log_f = jnp.minimum(gcap, 0.0) - jnp.log(1.0 + jnp.exp(-jnp.abs(gcap)))
    bcum = jnp.dot(tril, log_f, precision=lax.Precision.HIGHEST, preferred_element_type=F32)
    lane = lax.broadcasted_iota(jnp.int32, g.shape, 1)
    gates = jnp.where(lane < heads, gcap, bcum)
    gates_t = gates.T

    scale = dk ** -0.5
    nt = (((1,), (1,)), ((), ()))
    tn = (((0,), (0,)), ((), ()))
    for hd in range(heads):
        q = q_ref[:, hd * dk:(hd + 1) * dk]
        k = k_ref[:, hd * dk:(hd + 1) * dk]
        v = v_ref[:, hd * dv:(hd + 1) * dv]
        i_col = gates[:, hd:hd + 1]
        b_col = gates[:, heads + hd:heads + hd + 1]
        i_row = gates_t[hd:hd + 1, :]
        b_row = gates_t[heads + hd:heads + hd + 1, :]
        m_prev = m_ref[hd, :, 0:1]
        c_prev = c_ref[hd]
        n_prev = n_ref[hd]

        dmat = jnp.where(causal, b_col - b_row + i_row, -jnp.inf)
        inter = b_col + m_prev
        m_row = jnp.maximum(jnp.max(dmat, axis=-1, keepdims=True), inter)
        a_inter = jnp.exp(inter - m_row)
        s = lax.dot_general(q, k, nt, preferred_element_type=F32) * scale * jnp.exp(dmat - m_row)
        num = (a_inter * jnp.dot(q, c_prev.astype(BF16), preferred_element_type=F32)
               + jnp.dot(s.astype(BF16), v, preferred_element_type=F32))
        den = (a_inter * jnp.sum(q.astype(F32) * n_prev, axis=-1, keepdims=True)
               + jnp.sum(s, axis=-1, keepdims=True))
        h = num / jnp.maximum(jnp.abs(den), jnp.exp(-m_row))

        b_last = b_col[L - 1:L, :]
        dec = b_last - b_col + i_col
        m_new = jnp.maximum(b_last + m_prev, jnp.max(dec, axis=0, keepdims=True))
        a_old = jnp.exp(b_last + m_prev - m_new)
        kw = k.astype(F32) * (jnp.exp(dec - m_new) * scale)
        c_ref[hd] = a_old * c_prev + lax.dot_general(kw.astype(BF16), v, tn, preferred_element_type=F32)
        n_ref[hd] = a_old * n_prev + jnp.sum(kw, axis=0, keepdims=True)
        m_ref[hd] = jnp.broadcast_to(m_new, m_ref.shape[1:])

        h = h * lax.rsqrt(jnp.mean(h * h, axis=-1, keepdims=True) + NORM_EPS)
        h = h * gain_ref[:, hd * dv:(hd + 1) * dv]
        h = h * _sigmoid(o_ref[:, hd * dv:(hd + 1) * dv].astype(F32))
        out_ref[:, hd * dv:(hd + 1) * dv] = h.astype(out_ref.dtype)


def _mlstm_scan(proj, gates, head_gain, bsz, seq):
    m, width = proj.shape
    d = head_gain.shape[0]
    dv = d // ML_HEADS
    dk = dv // 2
    hk = ML_HEADS * dk
    assert width == 2 * hk + 2 * d and hk * 2 == d
    L = min(ML_CHUNK, seq)
    assert seq % L == 0
    nc = seq // L
    row = lambda b, c: b * nc + c
    blk = 2 * (L * (2 * hk + 2 * d) * 2 + L * V7X_LANES * 4 + d * 4 + L * d * 2) \
        + ML_HEADS * dk * dv * 4 + 16 * L * L * 4 + 8 * L * dv * 4
    return pl.pallas_call(
        functools.partial(_mlstm_kernel, dk=dk, dv=dv),
        out_shape=jax.ShapeDtypeStruct((m, d), BF16),
        grid=(bsz, nc),
        in_specs=[pl.BlockSpec((L, hk), lambda b, c: (row(b, c), 0)),
                  pl.BlockSpec((L, hk), lambda b, c: (row(b, c), 1)),
                  pl.BlockSpec((L, d), lambda b, c: (row(b, c), 1)),
                  pl.BlockSpec((L, d), lambda b, c: (row(b, c), 2)),
                  pl.BlockSpec((L, V7X_LANES), lambda b, c: (row(b, c), 0)),
                  pl.BlockSpec((1, d), lambda b, c: (0, 0))],
        out_specs=pl.BlockSpec((L, d), lambda b, c: (row(b, c), 0)),
        scratch_shapes=[pltpu.VMEM((ML_HEADS, dk, dv), F32),
                        pltpu.VMEM((ML_HEADS, 1, dk), F32),
                        pltpu.VMEM((ML_HEADS, 1, V7X_LANES), F32)],
        compiler_params=_params(("arbitrary", "arbitrary"), blk),
        name="mlstm_scan",
    )(proj, proj, proj, proj, gates, head_gain.reshape(1, d))


def _moba_kernel(q_ref, k_ref, v_ref, o_ref, vt_ref, km_ref, sel_ref, acc_ref, *, nb, bs, dh, hps, grp):
    qb = pl.program_id(2)
    dh_ext = vt_ref.shape[2]
    nt = (((1,), (1,)), ((), ()))
    c2 = (dh ** -0.5) * LOG2_E

    @pl.when(qb == 0)
    def _():
        def prep(n, carry):
            r0 = pl.multiple_of(n * bs, bs)
            for h in range(hps):
                vb = v_ref[pl.ds(r0, bs), h * dh:(h + 1) * dh].astype(F32)
                vt_ref[h, n, 0:dh, :] = vb.T.astype(BF16)
                vt_ref[h, n, dh:dh_ext, :] = jnp.ones((dh_ext - dh, bs), BF16)
                kb = k_ref[pl.ds(r0, bs), h * dh:(h + 1) * dh].astype(F32)
                km_ref[h, pl.ds(n, 1), :] = jnp.mean(kb, axis=0, keepdims=True)
            return carry
        lax.fori_loop(0, nb, prep, 0)

    blk_idx = lax.broadcasted_iota(jnp.int32, (nb, bs), 0)
    kidx = lax.broadcasted_iota(jnp.int32, (bs, bs), 0)
    qidx = lax.broadcasted_iota(jnp.int32, (bs, bs), 1)
    own_r0 = pl.multiple_of(qb * bs, bs)

    qs, m0s = [], []
    for h in range(hps):
        q = q_ref[:, h * dh:(h + 1) * dh]
        qs.append(q)

        km = km_ref[h]
        km_hi = km.astype(BF16)
        r1 = km - km_hi.astype(F32)
        km_mid = r1.astype(BF16)
        km_lo = (r1 - km_mid.astype(F32)).astype(BF16)
        gate = (lax.dot_general(km_hi, q, nt, preferred_element_type=F32)
                + lax.dot_general(km_mid, q, nt, preferred_element_type=F32)
                + lax.dot_general(km_lo, q, nt, preferred_element_type=F32))

        removed = blk_idx >= qb
        sel = jnp.zeros((nb, bs), F32)
        for r in range(MB_TOPK):
            gm = jnp.where(removed, -jnp.inf, gate)
            mx = jnp.max(gm, axis=0, keepdims=True)
            cand = jnp.logical_and(jnp.logical_not(removed), gm == mx)
            idx = jnp.min(jnp.where(cand, blk_idx, nb), axis=0, keepdims=True)
            hit = blk_idx == idx
            sel = jnp.where(jnp.logical_and(hit, qb > r), 1.0, sel)
            removed = jnp.logical_or(removed, hit)
        sel_ref[h] = sel

        k_own = k_ref[pl.ds(own_r0, bs), h * dh:(h + 1) * dh]
        s = lax.dot_general(k_own, q, nt, preferred_element_type=F32) * c2
        s = jnp.where(kidx <= qidx, s, -jnp.inf)
        m0 = jnp.max(s, axis=0, keepdims=True)
        p = jnp.exp2(s - m0)
        acc_ref[h] = jnp.dot(vt_ref[h, qb], p.astype(BF16), preferred_element_type=F32)
        m0s.append(m0)

    def body(c, ms):
        r0 = pl.multiple_of(c * (grp * bs), grp * bs)
        new_ms = []
        for h in range(hps):
            kc = k_ref[pl.ds(r0, grp * bs), h * dh:(h + 1) * dh]
            s = lax.dot_general(kc, qs[h], nt, preferred_element_type=F32) * c2
            maxes, pvs = [], []
            for g in range(grp):
                sg = jnp.where(sel_ref[h, pl.ds(c * grp + g, 1), :] > 0.0, s[g * bs:(g + 1) * bs, :], -jnp.inf)
                mg = jnp.max(sg, axis=0, keepdims=True)
                p = jnp.exp2(sg - jnp.where(mg == -jnp.inf, 0.0, mg)).astype(BF16)
                pvs.append(jnp.dot(vt_ref[h, c * grp + g], p, preferred_element_type=F32))
                maxes.append(mg)
            mx = ms[h]
            for mg in maxes:
                mx = jnp.maximum(mx, mg)
            acc = acc_ref[h] * jnp.exp2(ms[h] - mx)
            for mg, pv in zip(maxes, pvs):
                acc = acc + pv * jnp.exp2(mg - mx)
            acc_ref[h] = acc
            new_ms.append(mx)
        return tuple(new_ms)

    n_chunks = lax.div(qb + (grp - 1), grp)
    lax.fori_loop(0, n_chunks, body, tuple(m0s))

    for h in range(hps):
        acc = acc_ref[h]
        out = acc[0:dh, :] / acc[dh:dh + 1, :]
        o_ref[:, h * dh:(h + 1) * dh] = out.T.astype(o_ref.dtype)


def _moba_attention(qkv, bsz, seq):
    m, width = qkv.shape
    d = width // 3
    dh = d // MB_HEADS
    bs = MB_BLOCK
    hps = MB_HEADS_PER_STEP
    nb = seq // bs
    grp = min(MB_GROUP, nb)
    assert seq % bs == 0 and dh % V7X_LANES == 0 and MB_HEADS % hps == 0 and nb % grp == 0
    dh_ext = dh + V7X_BF16_SUBLANES
    wblk = hps * dh
    nhb = MB_HEADS // hps
    blk = (2 * (2 * bs * wblk * 2 + 2 * seq * wblk * 2) + hps * nb * dh_ext * bs * 2
           + hps * (8 * grp + 8) * bs * bs * 4)
    return pl.pallas_call(
        functools.partial(_moba_kernel, nb=nb, bs=bs, dh=dh, hps=hps, grp=grp),
        out_shape=jax.ShapeDtypeStruct((m, d), BF16),
        grid=(bsz, nhb, nb),
        in_specs=[pl.BlockSpec((bs, wblk), lambda b, h, i: (b * nb + i, h)),
                  pl.BlockSpec((seq, wblk), lambda b, h, i: (b, nhb + h)),
                  pl.BlockSpec((seq, wblk), lambda b, h, i: (b, 2 * nhb + h))],
        out_specs=pl.BlockSpec((bs, wblk), lambda b, h, i: (b * nb + i, h)),
        scratch_shapes=[pltpu.VMEM((hps, nb, dh_ext, bs), BF16),
                        pltpu.VMEM((hps, nb, dh), F32),
                        pltpu.VMEM((hps, nb, bs), F32),
                        pltpu.VMEM((hps, dh_ext, bs), F32)],
        compiler_params=_params(("arbitrary", "arbitrary", "arbitrary"), blk),
        name="moba_attention",
    )(qkv, qkv, qkv)


def _ffn_up_kernel(xprev_ref, x_ref, wg_ref, wu_ref, cwg_ref, cwu_ref, cbg_ref, cbu_ref, o_ref,
                   xcat_ref, hg0_ref, hu0_ref, hg1_ref, hu1_ref, *, tiles_per_seq):
    tm = x_ref.shape[0]
    i, j = pl.program_id(0), pl.program_id(1)

    @pl.when(j == 0)
    def _():
        prev = xprev_ref[...]
        seq_start = (i % tiles_per_seq) == 0
        xcat_ref[0:CONV_HALO, :] = jnp.where(seq_start, jnp.zeros_like(prev), prev)
        xcat_ref[CONV_HALO:, :] = x_ref[...]

    @pl.when(jnp.logical_and(i == 0, j == 0))
    def _():
        hg1_ref[...] = jnp.zeros_like(hg1_ref)
        hu1_ref[...] = jnp.zeros_like(hu1_ref)

    def conv(h_ref, cw_ref, cb_ref):
        acc = cb_ref[...]
        for t in range(CONV_WIDTH):
            acc = acc + cw_ref[t:t + 1, :] * h_ref[pl.ds(CONV_HALO - (CONV_WIDTH - 1) + t, tm), :]
        return acc

    def step(hg_cur, hu_cur, hg_done, hu_done):
        gate = conv(hg_done, cwg_ref, cbg_ref)
        up = conv(hu_done, cwu_ref, cbu_ref)
        o_ref[...] = (gate * _sigmoid(gate) * up).astype(o_ref.dtype)
        xc = xcat_ref[...]
        hg_cur[...] = jnp.dot(xc, wg_ref[...], preferred_element_type=F32)
        hu_cur[...] = jnp.dot(xc, wu_ref[...], preferred_element_type=F32)

    @pl.when(j % 2 == 0)
    def _():
        step(hg0_ref, hu0_ref, hg1_ref, hu1_ref)

    @pl.when(j % 2 == 1)
    def _():
        step(hg1_ref, hu1_ref, hg0_ref, hu0_ref)


def _ffn_up(h, w_up, conv_w, conv_b, seq):
    m, d = h.shape
    f = w_up.shape[1] // 2
    tm = min(MM_ROWS, seq)
    tn = min(FFN_COLS, f)
    assert seq % tm == 0 and f % tn == 0 and tm % CONV_HALO == 0
    nj = f // tn
    halo_blocks = tm // CONV_HALO
    cur = lambda j: jnp.minimum(j, nj - 1)
    done = lambda j: jnp.maximum(j - 1, 0)
    blk = (2 * (tm * d * 2 + CONV_HALO * d * 2 + 2 * d * tn * 2 + tm * tn * 2)
           + (tm + CONV_HALO) * d * 2 + 4 * (tm + CONV_HALO) * tn * 4 + 6 * tm * tn * 4)
    return pl.pallas_call(
        functools.partial(_ffn_up_kernel, tiles_per_seq=seq // tm),
        out_shape=jax.ShapeDtypeStruct((m, f), BF16),
        grid=(m // tm, nj + 1),
        in_specs=[pl.BlockSpec((CONV_HALO, d), lambda i, j: (jnp.maximum(i * halo_blocks - 1, 0), 0)),
                  pl.BlockSpec((tm, d), lambda i, j: (i, 0)),
                  pl.BlockSpec((d, tn), lambda i, j: (0, cur(j))),
                  pl.BlockSpec((d, tn), lambda i, j: (0, nj + cur(j))),
                  pl.BlockSpec((CONV_WIDTH, tn), lambda i, j: (0, done(j))),
                  pl.BlockSpec((CONV_WIDTH, tn), lambda i, j: (0, nj + done(j))),
                  pl.BlockSpec((1, tn), lambda i, j: (0, done(j))),
                  pl.BlockSpec((1, tn), lambda i, j: (0, nj + done(j)))],
        out_specs=pl.BlockSpec((tm, tn), lambda i, j: (i, done(j))),
        scratch_shapes=[pltpu.VMEM((tm + CONV_HALO, d), BF16),
                        pltpu.VMEM((tm + CONV_HALO, tn), F32),
                        pltpu.VMEM((tm + CONV_HALO, tn), F32),
                        pltpu.VMEM((tm + CONV_HALO, tn), F32),
                        pltpu.VMEM((tm + CONV_HALO, tn), F32)],
        compiler_params=_params(("arbitrary", "arbitrary"), blk),
        name="ffn_up_conv_gate",
    )(h, h, w_up, w_up, conv_w, conv_w, conv_b.reshape(1, 2 * f), conv_b.reshape(1, 2 * f))


def _conv_ffn(x, norm_g, w_up, conv_w, conv_b, w_down, seq):
    h = _rmsnorm(x, norm_g, BF16)
    act = _ffn_up(h, w_up.astype(BF16), conv_w, conv_b, seq)
    return _matmul(act, w_down.astype(BF16), F32, residual=x, rows=DOWN_ROWS, cols=DOWN_COLS)


def kernel(x, norm_mix, norm_ffn, a_w_in, a_gate_bias, a_head_norm, a_w_out, b_w_qkv, b_w_out,
           ffn_w_up, ffn_conv_w, ffn_conv_b, ffn_w_down, final_norm):
    bsz, seq, d = x.shape
    m = bsz * seq
    x = x.reshape(m, d)

    w_in = a_w_in[0]
    n_main = w_in.shape[1] - 2 * ML_HEADS
    w_gates = jnp.pad(w_in[:, n_main:], ((0, 0), (0, V7X_LANES - 2 * ML_HEADS)))
    gate_bias = jnp.pad(a_gate_bias[0], (0, V7X_LANES - 2 * ML_HEADS)).reshape(1, V7X_LANES)
    h, gates = _rmsnorm_gates(x, norm_mix[0], w_gates, gate_bias)
    proj = _matmul(h, w_in.astype(BF16), BF16, n=n_main)
    mixed = _mlstm_scan(proj, gates, a_head_norm[0], bsz, seq)
    x = _matmul(mixed, a_w_out[0].astype(BF16), F32, residual=x)
    x = _conv_ffn(x, norm_ffn[0], ffn_w_up[0], ffn_conv_w[0], ffn_conv_b[0], ffn_w_down[0], seq)

    h = _rmsnorm(x, norm_mix[1], BF16)
    qkv = _matmul(h, b_w_qkv[0].astype(BF16), BF16)
    attn = _moba_attention(qkv, bsz, seq)
    x = _matmul(attn, b_w_out[0].astype(BF16), F32, residual=x)
    x = _conv_ffn(x, norm_ffn[1], ffn_w_up[1], ffn_conv_w[1], ffn_conv_b[1], ffn_w_down[1], seq)

    return _rmsnorm(x, final_norm, F32).reshape(bsz, seq, d)
```

```python
import functools
import math

import jax
import jax.numpy as jnp
from jax import lax
from jax.experimental import pallas as pl
from jax.experimental.pallas import tpu as pltpu

F32 = jnp.float32
BF16 = jnp.bfloat16

NORM_EPS = 1e-6
ML_HEADS = 8
GATE_SOFTCAP = 15.0
MB_HEADS = 32
MB_BLOCK = 256
MB_TOPK = 3
CONV_WIDTH = 3

V7X_LANES = 128
V7X_BF16_SUBLANES = 16
V7X_VMEM_BYTES = 64 * 1024 * 1024
VMEM_RESERVE_BYTES = 6 * 1024 * 1024

ML_CHUNK = 256
NORM_ROWS = 256
MM_ROWS = 1024
MM_COLS = 512
DOWN_ROWS = 512
DOWN_COLS = 256
FFN_COLS = 256
CONV_HALO = V7X_BF16_SUBLANES
MB_GROUP = 2
MB_HEADS_PER_STEP = 2

LOG2_E = math.log2(math.e)
MB_MASK = 2.0 ** 100


def _vmem_limit(block_bytes):
    want = int(block_bytes) + VMEM_RESERVE_BYTES
    return max(min(want, V7X_VMEM_BYTES - VMEM_RESERVE_BYTES), 16 * 1024 * 1024)


def _params(sem, block_bytes):
    return pltpu.CompilerParams(dimension_semantics=sem, vmem_limit_bytes=_vmem_limit(block_bytes))


def _sigmoid(x):
    return 1.0 / (1.0 + jnp.exp(-x))


def _rmsnorm_kernel(x_ref, g_ref, o_ref):
    x = x_ref[...]
    ms = jnp.mean(x * x, axis=-1, keepdims=True)
    o_ref[...] = (x * lax.rsqrt(ms + NORM_EPS) * g_ref[...]).astype(o_ref.dtype)


def _rmsnorm(x, g, out_dtype):
    m, d = x.shape
    tm = min(NORM_ROWS, m)
    assert m % tm == 0
    blk = 2 * tm * d * (4 + jnp.dtype(out_dtype).itemsize) + 3 * tm * d * 4
    return pl.pallas_call(
        _rmsnorm_kernel,
        out_shape=jax.ShapeDtypeStruct((m, d), out_dtype),
        grid=(m // tm,),
        in_specs=[pl.BlockSpec((tm, d), lambda i: (i, 0)), pl.BlockSpec((1, d), lambda i: (0, 0))],
        out_specs=pl.BlockSpec((tm, d), lambda i: (i, 0)),
        compiler_params=_params(("arbitrary",), blk),
        name="rmsnorm",
    )(x, g.reshape(1, d))


def _rmsnorm_gates_kernel(x_ref, g_ref, wg_ref, b_ref, o_ref, gate_ref):
    x = x_ref[...]
    ms = jnp.mean(x * x, axis=-1, keepdims=True)
    y = x * lax.rsqrt(ms + NORM_EPS) * g_ref[...]
    o_ref[...] = y.astype(o_ref.dtype)
    gate_ref[...] = jnp.dot(y, wg_ref[...], precision=lax.Precision.HIGHEST,
                            preferred_element_type=F32) + b_ref[...]


def _rmsnorm_gates(x, g, w_gates, bias):
    m, d = x.shape
    tm = min(NORM_ROWS, m)
    assert m % tm == 0 and w_gates.shape == (d, V7X_LANES)
    blk = 2 * tm * d * 6 + 2 * d * V7X_LANES * 4 + 4 * tm * d * 4
    return pl.pallas_call(
        _rmsnorm_gates_kernel,
        out_shape=(jax.ShapeDtypeStruct((m, d), BF16), jax.ShapeDtypeStruct((m, V7X_LANES), F32)),
        grid=(m // tm,),
        in_specs=[pl.BlockSpec((tm, d), lambda i: (i, 0)), pl.BlockSpec((1, d), lambda i: (0, 0)),
                  pl.BlockSpec((d, V7X_LANES), lambda i: (0, 0)), pl.BlockSpec((1, V7X_LANES), lambda i: (0, 0))],
        out_specs=(pl.BlockSpec((tm, d), lambda i: (i, 0)), pl.BlockSpec((tm, V7X_LANES), lambda i: (i, 0))),
        compiler_params=_params(("arbitrary",), blk),
        name="rmsnorm_gates",
    )(x, g.reshape(1, d), w_gates, bias)


def _mm_kernel(a_ref, b_ref, o_ref):
    o_ref[...] = jnp.dot(a_ref[...], b_ref[...], preferred_element_type=F32).astype(o_ref.dtype)


def _mm_res_kernel(a_ref, b_ref, r_ref, o_ref):
    acc = jnp.dot(a_ref[...], b_ref[...], preferred_element_type=F32)
    o_ref[...] = (acc + r_ref[...]).astype(o_ref.dtype)


def _matmul(a, b, out_dtype, residual=None, n=None, rows=MM_ROWS, cols=MM_COLS):
    m, k = a.shape
    n = b.shape[1] if n is None else n
    tm, tn = min(rows, m), min(cols, n)
    assert m % tm == 0 and n % tn == 0 and n <= b.shape[1]
    osz = jnp.dtype(out_dtype).itemsize
    blk = 2 * (tm * k * 2 + k * tn * 2 + tm * tn * osz) + tm * tn * 4
    in_specs = [pl.BlockSpec((tm, k), lambda i, j: (i, 0)), pl.BlockSpec((k, tn), lambda i, j: (0, j))]
    args = [a, b]
    kern = _mm_kernel
    if residual is not None:
        in_specs.append(pl.BlockSpec((tm, tn), lambda i, j: (i, j)))
        args.append(residual)
        kern = _mm_res_kernel
        blk += 2 * tm * tn * 4
    return pl.pallas_call(
        kern,
        out_shape=jax.ShapeDtypeStruct((m, n), out_dtype),
        grid=(m // tm, n // tn),
        in_specs=in_specs,
        out_specs=pl.BlockSpec((tm, tn), lambda i, j: (i, j)),
        compiler_params=_params(("arbitrary", "arbitrary"), blk),
        name="matmul_res" if residual is not None else "matmul",
    )(*args)


def _mlstm_kernel(q_ref, k_ref, v_ref, o_ref, g_ref, gain_ref, out_ref, c_ref, n_ref, m_ref, *, dk, dv):
    L = q_ref.shape[0]
    heads = ML_HEADS

    @pl.when(pl.program_id(1) == 0)
    def _():
        c_ref[...] = jnp.zeros_like(c_ref)
        n_ref[...] = jnp.zeros_like(n_ref)
        m_ref[...] = jnp.zeros_like(m_ref)

    rows = lax.broadcasted_iota(jnp.int32, (L, L), 0)
    cols = lax.broadcasted_iota(jnp.int32, (L, L), 1)
    causal = cols <= rows
    tril = causal.astype(F32)

    g = g_ref[...]
    gcap = GATE_SOFTCAP * jnp.tanh(g / GATE_SOFTCAP)
    log_f = jnp.minimum(gcap, 0.0) - jnp.log(1.0 + jnp.exp(-jnp.abs(gcap)))
    bcum = jnp.dot(tril, log_f, precision=lax.Precision.HIGHEST, preferred_element_type=F32)
    lane = lax.broadcasted_iota(jnp.int32, g.shape, 1)
    gates = jnp.where(lane < heads, gcap, bcum)
    gates_t = gates.T

    scale = dk ** -0.5
    nt = (((1,), (1,)), ((), ()))
    tn = (((0,), (0,)), ((), ()))
    for hd in range(heads):
        q = q_ref[:, hd * dk:(hd + 1) * dk]
        k = k_ref[:, hd * dk:(hd + 1) * dk]
        v = v_ref[:, hd * dv:(hd + 1) * dv]
        i_col = gates[:, hd:hd + 1]
        b_col = gates[:, heads + hd:heads + hd + 1]
        i_row = gates_t[hd:hd + 1, :]
        b_row = gates_t[heads + hd:heads + hd + 1, :]
        m_prev = m_ref[hd, :, 0:1]
        c_prev = c_ref[hd]
        n_prev = n_ref[hd]

        dmat = jnp.where(causal, b_col - b_row + i_row, -jnp.inf)
        inter = b_col + m_prev
        m_row = jnp.maximum(jnp.max(dmat, axis=-1, keepdims=True), inter)
        a_inter = jnp.exp(inter - m_row)
        s = lax.dot_general(q, k, nt, preferred_element_type=F32) * scale * jnp.exp(dmat - m_row)
        num = (a_inter * jnp.dot(q, c_prev.astype(BF16), preferred_element_type=F32)
               + jnp.dot(s.astype(BF16), v, preferred_element_type=F32))
        den = (a_inter * jnp.sum(q.astype(F32) * n_prev, axis=-1, keepdims=True)
               + jnp.sum(s, axis=-1, keepdims=True))
        h = num / jnp.maximum(jnp.abs(den), jnp.exp(-m_row))

        b_last = b_col[L - 1:L, :]
        dec = b_last - b_col + i_col
        m_new = jnp.maximum(b_last + m_prev, jnp.max(dec, axis=0, keepdims=True))
        a_old = jnp.exp(b_last + m_prev - m_new)
        kw = k.astype(F32) * (jnp.exp(dec - m_new) * scale)
        c_ref[hd] = a_old * c_prev + lax.dot_general(kw.astype(BF16), v, tn, preferred_element_type=F32)
        n_ref[hd] = a_old * n_prev + jnp.sum(kw, axis=0, keepdims=True)
        m_ref[hd] = jnp.broadcast_to(m_new, m_ref.shape[1:])

        h = h * lax.rsqrt(jnp.mean(h * h, axis=-1, keepdims=True) + NORM_EPS)
        h = h * gain_ref[:, hd * dv:(hd + 1) * dv]
        h = h * _sigmoid(o_ref[:, hd * dv:(hd + 1) * dv].astype(F32))
        out_ref[:, hd * dv:(hd + 1) * dv] = h.astype(out_ref.dtype)


def _mlstm_scan(proj, gates, head_gain, bsz, seq):
    m, width = proj.shape
    d = head_gain.shape[0]
    dv = d // ML_HEADS
    dk = dv // 2
    hk = ML_HEADS * dk
    assert width == 2 * hk + 2 * d and hk * 2 == d
    L = min(ML_CHUNK, seq)
    assert seq % L == 0
    nc = seq // L
    row = lambda b, c: b * nc + c
    blk = 2 * (L * (2 * hk + 2 * d) * 2 + L * V7X_LANES * 4 + d * 4 + L * d * 2) \
        + ML_HEADS * dk * dv * 4 + 16 * L * L * 4 + 8 * L * dv * 4
    return pl.pallas_call(
        functools.partial(_mlstm_kernel, dk=dk, dv=dv),
        out_shape=jax.ShapeDtypeStruct((m, d), BF16),
        grid=(bsz, nc),
        in_specs=[pl.BlockSpec((L, hk), lambda b, c: (row(b, c), 0)),
                  pl.BlockSpec((L, hk), lambda b, c: (row(b, c), 1)),
                  pl.BlockSpec((L, d), lambda b, c: (row(b, c), 1)),
                  pl.BlockSpec((L, d), lambda b, c: (row(b, c), 2)),
                  pl.BlockSpec((L, V7X_LANES), lambda b, c: (row(b, c), 0)),
                  pl.BlockSpec((1, d), lambda b, c: (0, 0))],
        out_specs=pl.BlockSpec((L, d), lambda b, c: (row(b, c), 0)),
        scratch_shapes=[pltpu.VMEM((ML_HEADS, dk, dv), F32),
                        pltpu.VMEM((ML_HEADS, 1, dk), F32),
                        pltpu.VMEM((ML_HEADS, 1, V7X_LANES), F32)],
        compiler_params=_params(("arbitrary", "arbitrary"), blk),
        name="mlstm_scan",
    )(proj, proj, proj, proj, gates, head_gain.reshape(1, d))


def _moba_kernel(q_ref, k_ref, v_ref, o_ref, kx_ref, vt_ref, km_ref, s_ref, acc_ref, *, nb, bs, dh, hps, grp):
    qb = pl.program_id(2)
    dh_ext = vt_ref.shape[2]
    nbp = vt_ref.shape[1]
    nt = (((1,), (1,)), ((), ()))
    cw = grp * bs

    @pl.when(qb == 0)
    def _():
        lane = lax.broadcasted_iota(jnp.int32, (bs, dh), 1)

        def prep(n, carry):
            r0 = pl.multiple_of(n * bs, bs)
            onehot = (lane == n).astype(BF16)
            for h in range(hps):
                vb = v_ref[pl.ds(r0, bs), h * dh:(h + 1) * dh].astype(F32)
                vt_ref[h, n, 0:dh, :] = vb.T.astype(BF16)
                vt_ref[h, n, dh:dh_ext, :] = jnp.ones((dh_ext - dh, bs), BF16)
                kb = k_ref[pl.ds(r0, bs), h * dh:(h + 1) * dh]
                kx_ref[h, pl.ds(r0, bs), 0:dh] = kb
                kx_ref[h, pl.ds(r0, bs), dh:2 * dh] = onehot
                km_ref[h, pl.ds(n, 1), :] = jnp.mean(kb.astype(F32), axis=0, keepdims=True)
            return carry
        lax.fori_loop(0, nb, prep, 0)
        masked = (lane == nb).astype(BF16)
        for h in range(hps):
            for n in range(nb, nbp):
                vt_ref[h, n] = jnp.zeros((dh_ext, bs), BF16)
                kx_ref[h, n * bs:(n + 1) * bs, 0:dh] = jnp.zeros((bs, dh), BF16)
                kx_ref[h, n * bs:(n + 1) * bs, dh:2 * dh] = masked

    blk_idx = lax.broadcasted_iota(jnp.int32, (nb, bs), 0)
    row_idx = lax.broadcasted_iota(jnp.int32, (dh, bs), 0)
    kidx = lax.broadcasted_iota(jnp.int32, (bs, bs), 0)
    qidx = lax.broadcasted_iota(jnp.int32, (bs, bs), 1)
    own_r0 = pl.multiple_of(qb * bs, bs)

    qxs, m0s = [], []
    for h in range(hps):
        q = q_ref[:, h * dh:(h + 1) * dh]

        km = km_ref[h]
        km_hi = km.astype(BF16)
        r1 = km - km_hi.astype(F32)
        km_mid = r1.astype(BF16)
        km_lo = (r1 - km_mid.astype(F32)).astype(BF16)
        gate = (lax.dot_general(km_hi, q, nt, preferred_element_type=F32)
                + lax.dot_general(km_mid, q, nt, preferred_element_type=F32)
                + lax.dot_general(km_lo, q, nt, preferred_element_type=F32))

        removed = blk_idx >= qb
        sel = jnp.zeros((nb, bs), F32)
        for r in range(MB_TOPK):
            gm = jnp.where(removed, -jnp.inf, gate)
            mx = jnp.max(gm, axis=0, keepdims=True)
            cand = jnp.logical_and(jnp.logical_not(removed), gm == mx)
            idx = jnp.min(jnp.where(cand, blk_idx, nb), axis=0, keepdims=True)
            hit = blk_idx == idx
            sel = jnp.where(jnp.logical_and(hit, qb > r), 1.0, sel)
            removed = jnp.logical_or(removed, hit)

        sel_pad = jnp.concatenate([sel, jnp.zeros((dh - nb, bs), F32)], axis=0)
        neg = jnp.where(jnp.logical_or(sel_pad > 0.0, row_idx > nb), 0.0, -MB_MASK)
        qxs.append(jnp.concatenate([q, neg.T.astype(BF16)], axis=1))

        k_own = k_ref[pl.ds(own_r0, bs), h * dh:(h + 1) * dh]
        s = lax.dot_general(k_own, q, nt, preferred_element_type=F32)
        s = jnp.where(kidx <= qidx, s, -jnp.inf)
        m0 = jnp.max(s, axis=0, keepdims=True)
        p = jnp.exp2(s - m0)
        acc_ref[h] = jnp.dot(vt_ref[h, qb], p.astype(BF16), preferred_element_type=F32)
        m0s.append(m0)

    def score(c, slot):
        r0 = pl.multiple_of(c * cw, cw)
        for h in range(hps):
            s_ref[slot, h] = lax.dot_general(kx_ref[h, pl.ds(r0, cw), :], qxs[h], nt, preferred_element_type=F32)

    def consume(c, slot, ms):
        new_ms = []
        for h in range(hps):
            maxes, pvs = [], []
            for g in range(grp):
                sg = s_ref[slot, h, g * bs:(g + 1) * bs, :]
                mg = jnp.max(sg, axis=0, keepdims=True)
                p = jnp.exp2(sg - mg).astype(BF16)
                pvs.append(jnp.dot(vt_ref[h, c * grp + g], p, preferred_element_type=F32))
                maxes.append(mg)
            mx = ms[h]
            for mg in maxes:
                mx = jnp.maximum(mx, mg)
            acc = acc_ref[h] * jnp.exp2(ms[h] - mx)
            for mg, pv in zip(maxes, pvs):
                acc = acc + pv * jnp.exp2(mg - mx)
            acc_ref[h] = acc
            new_ms.append(mx)
        return tuple(new_ms)

    def body(u, ms):
        c = 2 * u
        score(c + 1, 1)
        ms = consume(c, 0, ms)
        score(c + 2, 0)
        return consume(c + 1, 1, ms)

    n_chunks = lax.div(qb + (grp - 1), grp)
    score(0, 0)
    lax.fori_loop(0, lax.div(n_chunks + 1, 2), body, tuple(m0s))

    for h in range(hps):
        acc = acc_ref[h]
        out = acc[0:dh, :] / acc[dh:dh + 1, :]
        o_ref[:, h * dh:(h + 1) * dh] = out.T.astype(o_ref.dtype)


def _moba_attention(qkv, bsz, seq):
    m, width = qkv.shape
    d = width // 3
    dh = d // MB_HEADS
    bs = MB_BLOCK
    hps = MB_HEADS_PER_STEP
    nb = seq // bs
    grp = MB_GROUP
    nbp = (nb + grp - 1) // grp * grp + 2 * grp
    assert seq % bs == 0 and dh == V7X_LANES and MB_HEADS % hps == 0 and nb < dh
    dh_ext = dh + V7X_BF16_SUBLANES
    wblk = hps * dh
    nhb = MB_HEADS // hps
    blk = (2 * (2 * bs * wblk * 2 + 2 * seq * wblk * 2) + hps * nbp * bs * (2 * dh + dh_ext) * 2
           + 2 * hps * grp * bs * bs * 4 + hps * (4 * grp + 8) * bs * bs * 4)
    return pl.pallas_call(
        functools.partial(_moba_kernel, nb=nb, bs=bs, dh=dh, hps=hps, grp=grp),
        out_shape=jax.ShapeDtypeStruct((m, d), BF16),
        grid=(bsz, nhb, nb),
        in_specs=[pl.BlockSpec((bs, wblk), lambda b, h, i: (b * nb + i, h)),
                  pl.BlockSpec((seq, wblk), lambda b, h, i: (b, nhb + h)),
                  pl.BlockSpec((seq, wblk), lambda b, h, i: (b, 2 * nhb + h))],
        out_specs=pl.BlockSpec((bs, wblk), lambda b, h, i: (b * nb + i, h)),
        scratch_shapes=[pltpu.VMEM((hps, nbp * bs, 2 * dh), BF16),
                        pltpu.VMEM((hps, nbp, dh_ext, bs), BF16),
                        pltpu.VMEM((hps, nb, dh), F32),
                        pltpu.VMEM((2, hps, grp * bs, bs), F32),
                        pltpu.VMEM((hps, dh_ext, bs), F32)],
        compiler_params=_params(("arbitrary", "arbitrary", "arbitrary"), blk),
        name="moba_attention",
    )(qkv, qkv, qkv)


def _ffn_up_kernel(xprev_ref, x_ref, wg_ref, wu_ref, cwg_ref, cwu_ref, cbg_ref, cbu_ref, o_ref,
                   xcat_ref, hg_ref, hu_ref, *, tiles_per_seq):
    tm = x_ref.shape[0]

    @pl.when(pl.program_id(1) == 0)
    def _():
        prev = xprev_ref[...]
        seq_start = (pl.program_id(0) % tiles_per_seq) == 0
        xcat_ref[0:CONV_HALO, :] = jnp.where(seq_start, jnp.zeros_like(prev), prev)
        xcat_ref[CONV_HALO:, :] = x_ref[...]

    xc = xcat_ref[...]
    hg_ref[...] = jnp.dot(xc, wg_ref[...], preferred_element_type=F32)
    hu_ref[...] = jnp.dot(xc, wu_ref[...], preferred_element_type=F32)

    def conv(h_ref, cw_ref, cb_ref):
        acc = cb_ref[...]
        for t in range(CONV_WIDTH):
            acc = acc + cw_ref[t:t + 1, :] * h_ref[pl.ds(CONV_HALO - (CONV_WIDTH - 1) + t, tm), :]
        return acc

    gate = conv(hg_ref, cwg_ref, cbg_ref)
    up = conv(hu_ref, cwu_ref, cbu_ref)
    o_ref[...] = (gate * _sigmoid(gate) * up).astype(o_ref.dtype)


def _ffn_up(h, w_up, conv_w, conv_b, seq):
    m, d = h.shape
    f = w_up.shape[1] // 2
    tm = min(MM_ROWS, seq)
    tn = min(FFN_COLS, f)
    assert seq % tm == 0 and f % tn == 0 and tm % CONV_HALO == 0
    nj = f // tn
    halo_blocks = tm // CONV_HALO
    blk = (2 * (tm * d * 2 + CONV_HALO * d * 2 + 2 * d * tn * 2 + tm * tn * 2)
           + (tm + CONV_HALO) * d * 2 + 2 * (tm + CONV_HALO) * tn * 4 + 6 * tm * tn * 4)
    return pl.pallas_call(
        functools.partial(_ffn_up_kernel, tiles_per_seq=seq // tm),
        out_shape=jax.ShapeDtypeStruct((m, f), BF16),
        grid=(m // tm, nj),
        in_specs=[pl.BlockSpec((CONV_HALO, d), lambda i, j: (jnp.maximum(i * halo_blocks - 1, 0), 0)),
                  pl.BlockSpec((tm, d), lambda i, j: (i, 0)),
                  pl.BlockSpec((d, tn), lambda i, j: (0, j)),
                  pl.BlockSpec((d, tn), lambda i, j: (0, nj + j)),
                  pl.BlockSpec((CONV_WIDTH, tn), lambda i, j: (0, j)),
                  pl.BlockSpec((CONV_WIDTH, tn), lambda i, j: (0, nj + j)),
                  pl.BlockSpec((1, tn), lambda i, j: (0, j)),
                  pl.BlockSpec((1, tn), lambda i, j: (0, nj + j))],
        out_specs=pl.BlockSpec((tm, tn), lambda i, j: (i, j)),
        scratch_shapes=[pltpu.VMEM((tm + CONV_HALO, d), BF16),
                        pltpu.VMEM((tm + CONV_HALO, tn), F32),
                        pltpu.VMEM((tm + CONV_HALO, tn), F32)],
        compiler_params=_params(("arbitrary", "arbitrary"), blk),
        name="ffn_up_conv_gate",
    )(h, h, w_up, w_up, conv_w, conv_w, conv_b.reshape(1, 2 * f), conv_b.reshape(1, 2 * f))


def _conv_ffn(x, norm_g, w_up, conv_w, conv_b, w_down, seq):
    h = _rmsnorm(x, norm_g, BF16)
    act = _ffn_up(h, w_up.astype(BF16), conv_w, conv_b, seq)
    return _matmul(act, w_down.astype(BF16), F32, residual=x, rows=DOWN_ROWS, cols=DOWN_COLS)


def kernel(x, norm_mix, norm_ffn, a_w_in, a_gate_bias, a_head_norm, a_w_out, b_w_qkv, b_w_out,
           ffn_w_up, ffn_conv_w, ffn_conv_b, ffn_w_down, final_norm):
    bsz, seq, d = x.shape
    m = bsz * seq
    x = x.reshape(m, d)

    w_in = a_w_in[0]
    n_main = w_in.shape[1] - 2 * ML_HEADS
    w_gates = jnp.pad(w_in[:, n_main:], ((0, 0), (0, V7X_LANES - 2 * ML_HEADS)))
    gate_bias = jnp.pad(a_gate_bias[0], (0, V7X_LANES - 2 * ML_HEADS)).reshape(1, V7X_LANES)
    h, gates = _rmsnorm_gates(x, norm_mix[0], w_gates, gate_bias)
    proj = _matmul(h, w_in.astype(BF16), BF16, n=n_main)
    mixed = _mlstm_scan(proj, gates, a_head_norm[0], bsz, seq)
    x = _matmul(mixed, a_w_out[0].astype(BF16), F32, residual=x)
    x = _conv_ffn(x, norm_ffn[0], ffn_w_up[0], ffn_conv_w[0], ffn_conv_b[0], ffn_w_down[0], seq)

    h = _rmsnorm(x, norm_mix[1], BF16)
    q_scale = jnp.where(jnp.arange(3 * d) < d, (d // MB_HEADS) ** -0.5 * LOG2_E, 1.0).astype(F32)
    qkv = _matmul(h, (b_w_qkv[0] * q_scale).astype(BF16), BF16)
    attn = _moba_attention(qkv, bsz, seq)
    x = _matmul(attn, b_w_out[0].astype(BF16), F32, residual=x)
    x = _conv_ffn(x, norm_ffn[1], ffn_w_up[1], ffn_conv_w[1], ffn_conv_b[1], ffn_w_down[1], seq)

    return _rmsnorm(x, final_norm, F32).reshape(bsz, seq, d)
```

```python
import functools
import math

import jax
import jax.numpy as jnp
from jax import lax
from jax.experimental import pallas as pl
from jax.experimental.pallas import tpu as pltpu

F32 = jnp.float32
BF16 = jnp.bfloat16

NORM_EPS = 1e-6
ML_HEADS = 8
GATE_SOFTCAP = 15.0
MB_HEADS = 32
MB_BLOCK = 256
MB_TOPK = 3
CONV_WIDTH = 3

V7X_LANES = 128
V7X_BF16_SUBLANES = 16
V7X_VMEM_BYTES = 64 * 1024 * 1024
VMEM_RESERVE_BYTES = 6 * 1024 * 1024

ML_CHUNK = 256
NORM_ROWS = 256
MM_ROWS = 1024
MM_COLS = 512
DOWN_ROWS = 512
DOWN_COLS = 256
FFN_COLS = 256
CONV_HALO = V7X_BF16_SUBLANES
MB_GROUP = 2
MB_HEADS_PER_STEP = 2
MB_SELECT_UNROLL = 4

LOG2_E = math.log2(math.e)
MB_MASK = 2.0 ** 100


def _vmem_limit(block_bytes):
    want = int(block_bytes) + VMEM_RESERVE_BYTES
    return max(min(want, V7X_VMEM_BYTES - VMEM_RESERVE_BYTES), 16 * 1024 * 1024)


def _params(sem, block_bytes):
    return pltpu.CompilerParams(dimension_semantics=sem, vmem_limit_bytes=_vmem_limit(block_bytes))


def _sigmoid(x):
    return 1.0 / (1.0 + jnp.exp(-x))


def _rmsnorm_kernel(x_ref, g_ref, o_ref):
    x = x_ref[...]
    ms = jnp.mean(x * x, axis=-1, keepdims=True)
    o_ref[...] = (x * lax.rsqrt(ms + NORM_EPS) * g_ref[...]).astype(o_ref.dtype)


def _rmsnorm(x, g, out_dtype):
    m, d = x.shape
    tm = min(NORM_ROWS, m)
    assert m % tm == 0
    blk = 2 * tm * d * (4 + jnp.dtype(out_dtype).itemsize) + 3 * tm * d * 4
    return pl.pallas_call(
        _rmsnorm_kernel,
        out_shape=jax.ShapeDtypeStruct((m, d), out_dtype),
        grid=(m // tm,),
        in_specs=[pl.BlockSpec((tm, d), lambda i: (i, 0)), pl.BlockSpec((1, d), lambda i: (0, 0))],
        out_specs=pl.BlockSpec((tm, d), lambda i: (i, 0)),
        compiler_params=_params(("arbitrary",), blk),
        name="rmsnorm",
    )(x, g.reshape(1, d))


def _rmsnorm_gates_kernel(x_ref, g_ref, wg_ref, b_ref, o_ref, gate_ref):
    x = x_ref[...]
    ms = jnp.mean(x * x, axis=-1, keepdims=True)
    y = x * lax.rsqrt(ms + NORM_EPS) * g_ref[...]
    o_ref[...] = y.astype(o_ref.dtype)
    gate_ref[...] = jnp.dot(y, wg_ref[...], precision=lax.Precision.HIGHEST,
                            preferred_element_type=F32) + b_ref[...]


def _rmsnorm_gates(x, g, w_gates, bias):
    m, d = x.shape
    tm = min(NORM_ROWS, m)
    assert m % tm == 0 and w_gates.shape == (d, V7X_LANES)
    blk = 2 * tm * d * 6 + 2 * d * V7X_LANES * 4 + 4 * tm * d * 4
    return pl.pallas_call(
        _rmsnorm_gates_kernel,
        out_shape=(jax.ShapeDtypeStruct((m, d), BF16), jax.ShapeDtypeStruct((m, V7X_LANES), F32)),
        grid=(m // tm,),
        in_specs=[pl.BlockSpec((tm, d), lambda i: (i, 0)), pl.BlockSpec((1, d), lambda i: (0, 0)),
                  pl.BlockSpec((d, V7X_LANES), lambda i: (0, 0)), pl.BlockSpec((1, V7X_LANES), lambda i: (0, 0))],
        out_specs=(pl.BlockSpec((tm, d), lambda i: (i, 0)), pl.BlockSpec((tm, V7X_LANES), lambda i: (i, 0))),
        compiler_params=_params(("arbitrary",), blk),
        name="rmsnorm_gates",
    )(x, g.reshape(1, d), w_gates, bias)


def _mm_kernel(*refs, has_scale, has_res):
    a_ref, b_ref, o_ref = refs[0], refs[1], refs[-1]
    b = b_ref[...]
    if has_scale:
        b = b * refs[2][...]
    acc = jnp.dot(a_ref[...], b.astype(BF16), preferred_element_type=F32)
    if has_res:
        acc = acc + refs[-2][...]
    o_ref[...] = acc.astype(o_ref.dtype)


def _matmul(a, b, out_dtype, residual=None, n=None, layer=None, col_scale=None, rows=MM_ROWS, cols=MM_COLS):
    m, k = a.shape
    n = b.shape[-1] if n is None else n
    tm, tn = min(rows, m), min(cols, n)
    assert m % tm == 0 and n % tn == 0 and n <= b.shape[-1] and b.shape[-2] == k
    osz = jnp.dtype(out_dtype).itemsize
    blk = 2 * (tm * k * 2 + k * tn * b.dtype.itemsize + tm * tn * osz) + tm * tn * 4 + k * tn * 6
    if layer is None:
        b_spec = pl.BlockSpec((k, tn), lambda i, j: (0, j))
    else:
        b_spec = pl.BlockSpec((None, k, tn), lambda i, j: (layer, 0, j))
    in_specs = [pl.BlockSpec((tm, k), lambda i, j: (i, 0)), b_spec]
    args = [a, b]
    if col_scale is not None:
        in_specs.append(pl.BlockSpec((1, tn), lambda i, j: (0, j)))
        args.append(col_scale)
    if residual is not None:
        in_specs.append(pl.BlockSpec((tm, tn), lambda i, j: (i, j)))
        args.append(residual)
        blk += 2 * tm * tn * 4
    return pl.pallas_call(
        functools.partial(_mm_kernel, has_scale=col_scale is not None, has_res=residual is not None),
        out_shape=jax.ShapeDtypeStruct((m, n), out_dtype),
        grid=(m // tm, n // tn),
        in_specs=in_specs,
        out_specs=pl.BlockSpec((tm, tn), lambda i, j: (i, j)),
        compiler_params=_params(("arbitrary", "arbitrary"), blk),
        name="matmul_res" if residual is not None else "matmul",
    )(*args)


def _mlstm_kernel(q_ref, k_ref, v_ref, o_ref, g_ref, gain_ref, out_ref, c_ref, n_ref, m_ref, *, dk, dv):
    L = q_ref.shape[0]
    heads = ML_HEADS

    @pl.when(pl.program_id(1) == 0)
    def _():
        c_ref[...] = jnp.zeros_like(c_ref)
        n_ref[...] = jnp.zeros_like(n_ref)
        m_ref[...] = jnp.zeros_like(m_ref)

    rows = lax.broadcasted_iota(jnp.int32, (L, L), 0)
    cols = lax.broadcasted_iota(jnp.int32, (L, L), 1)
    causal = cols <= rows
    tril = causal.astype(F32)

    g = g_ref[...]
    gcap = GATE_SOFTCAP * jnp.tanh(g / GATE_SOFTCAP)
    log_f = jnp.minimum(gcap, 0.0) - jnp.log(1.0 + jnp.exp(-jnp.abs(gcap)))
    bcum = jnp.dot(tril, log_f, precision=lax.Precision.HIGHEST, preferred_element_type=F32)
    lane = lax.broadcasted_iota(jnp.int32, g.shape, 1)
    gates = jnp.where(lane < heads, gcap, bcum)
    gates_t = gates.T

    scale = dk ** -0.5
    nt = (((1,), (1,)), ((), ()))
    tn = (((0,), (0,)), ((), ()))
    for hd in range(heads):
        q = q_ref[:, hd * dk:(hd + 1) * dk]
        k = k_ref[:, hd * dk:(hd + 1) * dk]
        v = v_ref[:, hd * dv:(hd + 1) * dv]
        i_col = gates[:, hd:hd + 1]
        b_col = gates[:, heads + hd:heads + hd + 1]
        i_row = gates_t[hd:hd + 1, :]
        b_row = gates_t[heads + hd:heads + hd + 1, :]
        m_prev = m_ref[hd, :, 0:1]
        c_prev = c_ref[hd]
        n_prev = n_ref[hd]

        dmat = jnp.where(causal, b_col - b_row + i_row, -jnp.inf)
        inter = b_col + m_prev
        m_row = jnp.maximum(jnp.max(dmat, axis=-1, keepdims=True), inter)
        a_inter = jnp.exp(inter - m_row)
        s = lax.dot_general(q, k, nt, preferred_element_type=F32) * scale * jnp.exp(dmat - m_row)
        num = (a_inter * jnp.dot(q, c_prev.astype(BF16), preferred_element_type=F32)
               + jnp.dot(s.astype(BF16), v, preferred_element_type=F32))
        den = (a_inter * jnp.sum(q.astype(F32) * n_prev, axis=-1, keepdims=True)
               + jnp.sum(s, axis=-1, keepdims=True))
        h = num / jnp.maximum(jnp.abs(den), jnp.exp(-m_row))

        b_last = b_col[L - 1:L, :]
        dec = b_last - b_col + i_col
        m_new = jnp.maximum(b_last + m_prev, jnp.max(dec, axis=0, keepdims=True))
        a_old = jnp.exp(b_last + m_prev - m_new)
        kw = k.astype(F32) * (jnp.exp(dec - m_new) * scale)
        c_ref[hd] = a_old * c_prev + lax.dot_general(kw.astype(BF16), v, tn, preferred_element_type=F32)
        n_ref[hd] = a_old * n_prev + jnp.sum(kw, axis=0, keepdims=True)
        m_ref[hd] = jnp.broadcast_to(m_new, m_ref.shape[1:])

        h = h * lax.rsqrt(jnp.mean(h * h, axis=-1, keepdims=True) + NORM_EPS)
        h = h * gain_ref[:, hd * dv:(hd + 1) * dv]
        h = h * _sigmoid(o_ref[:, hd * dv:(hd + 1) * dv].astype(F32))
        out_ref[:, hd * dv:(hd + 1) * dv] = h.astype(out_ref.dtype)


def _mlstm_scan(proj, gates, head_gain, bsz, seq):
    m, width = proj.shape
    d = head_gain.shape[0]
    dv = d // ML_HEADS
    dk = dv // 2
    hk = ML_HEADS * dk
    assert width == 2 * hk + 2 * d and hk * 2 == d
    L = min(ML_CHUNK, seq)
    assert seq % L == 0
    nc = seq // L
    row = lambda b, c: b * nc + c
    blk = 2 * (L * (2 * hk + 2 * d) * 2 + L * V7X_LANES * 4 + d * 4 + L * d * 2) \
        + ML_HEADS * dk * dv * 4 + 16 * L * L * 4 + 8 * L * dv * 4
    return pl.pallas_call(
        functools.partial(_mlstm_kernel, dk=dk, dv=dv),
        out_shape=jax.ShapeDtypeStruct((m, d), BF16),
        grid=(bsz, nc),
        in_specs=[pl.BlockSpec((L, hk), lambda b, c: (row(b, c), 0)),
                  pl.BlockSpec((L, hk), lambda b, c: (row(b, c), 1)),
                  pl.BlockSpec((L, d), lambda b, c: (row(b, c), 1)),
                  pl.BlockSpec((L, d), lambda b, c: (row(b, c), 2)),
                  pl.BlockSpec((L, V7X_LANES), lambda b, c: (row(b, c), 0)),
                  pl.BlockSpec((1, d), lambda b, c: (0, 0))],
        out_specs=pl.BlockSpec((L, d), lambda b, c: (row(b, c), 0)),
        scratch_shapes=[pltpu.VMEM((ML_HEADS, dk, dv), F32),
                        pltpu.VMEM((ML_HEADS, 1, dk), F32),
                        pltpu.VMEM((ML_HEADS, 1, V7X_LANES), F32)],
        compiler_params=_params(("arbitrary", "arbitrary"), blk),
        name="mlstm_scan",
    )(proj, proj, proj, proj, gates, head_gain.reshape(1, d))


def _moba_kernel(q_ref, k_ref, v_ref, o_ref, kx_ref, vt_ref, km_ref, neg_ref, s_ref, acc_ref, *,
                 nb, bs, dh, hps, grp):
    qb = pl.program_id(2)
    dh_ext = vt_ref.shape[2]
    nbp = vt_ref.shape[1]
    nt = (((1,), (1,)), ((), ()))
    cw = grp * bs

    @pl.when(qb == 0)
    def _():
        lane = lax.broadcasted_iota(jnp.int32, (bs, dh), 1)

        def prep(n, carry):
            r0 = pl.multiple_of(n * bs, bs)
            onehot = (lane == n).astype(BF16)
            for h in range(hps):
                vb = v_ref[pl.ds(r0, bs), h * dh:(h + 1) * dh].astype(F32)
                vt_ref[h, n, 0:dh, :] = vb.T.astype(BF16)
                vt_ref[h, n, dh:dh_ext, :] = jnp.ones((dh_ext - dh, bs), BF16)
                kb = k_ref[pl.ds(r0, bs), h * dh:(h + 1) * dh]
                kx_ref[h, pl.ds(r0, bs), 0:dh] = kb
                kx_ref[h, pl.ds(r0, bs), dh:2 * dh] = onehot
                km_ref[h, pl.ds(n, 1), :] = jnp.mean(kb.astype(F32), axis=0, keepdims=True)
            return carry
        lax.fori_loop(0, nb, prep, 0)
        masked = (lane == nb).astype(BF16)
        for h in range(hps):
            for n in range(nb, nbp):
                vt_ref[h, n] = jnp.zeros((dh_ext, bs), BF16)
                kx_ref[h, n * bs:(n + 1) * bs, 0:dh] = jnp.zeros((bs, dh), BF16)
                kx_ref[h, n * bs:(n + 1) * bs, dh:2 * dh] = masked

        blk_idx = lax.broadcasted_iota(jnp.int32, (nb, bs), 0)
        row_idx = lax.broadcasted_iota(jnp.int32, (dh, bs), 0)
        km_parts = []
        for h in range(hps):
            km = km_ref[h]
            km_hi = km.astype(BF16)
            r1 = km - km_hi.astype(F32)
            km_mid = r1.astype(BF16)
            km_parts.append((km_hi, km_mid, (r1 - km_mid.astype(F32)).astype(BF16)))

        def select(c, carry):
            for j in range(MB_SELECT_UNROLL):
                qj = c * MB_SELECT_UNROLL + j
                r0 = pl.multiple_of(qj * bs, bs)
                for h in range(hps):
                    q = q_ref[pl.ds(r0, bs), h * dh:(h + 1) * dh]
                    gate = sum(lax.dot_general(part, q, nt, preferred_element_type=F32)
                               for part in km_parts[h])
                    removed = blk_idx >= qj
                    sel = jnp.zeros((nb, bs), F32)
                    for r in range(MB_TOPK):
                        gm = jnp.where(removed, -jnp.inf, gate)
                        mx = jnp.max(gm, axis=0, keepdims=True)
                        cand = jnp.logical_and(jnp.logical_not(removed), gm == mx)
                        idx = jnp.min(jnp.where(cand, blk_idx, nb), axis=0, keepdims=True)
                        hit = blk_idx == idx
                        sel = jnp.where(jnp.logical_and(hit, qj > r), 1.0, sel)
                        removed = jnp.logical_or(removed, hit)
                    sel_pad = jnp.concatenate([sel, jnp.zeros((dh - nb, bs), F32)], axis=0)
                    neg = jnp.where(jnp.logical_or(sel_pad > 0.0, row_idx > nb), 0.0, -MB_MASK)
                    neg_ref[h, pl.ds(r0, bs), :] = neg.T.astype(BF16)
            return carry
        lax.fori_loop(0, nb // MB_SELECT_UNROLL, select, 0)

    kidx = lax.broadcasted_iota(jnp.int32, (bs, bs), 0)
    qidx = lax.broadcasted_iota(jnp.int32, (bs, bs), 1)
    own_r0 = pl.multiple_of(qb * bs, bs)

    qs = [q_ref[pl.ds(own_r0, bs), h * dh:(h + 1) * dh] for h in range(hps)]
    qxs = [jnp.concatenate([qs[h], neg_ref[h, pl.ds(own_r0, bs), :]], axis=1) for h in range(hps)]

    def score(c, slot):
        r0 = pl.multiple_of(c * cw, cw)
        for h in range(hps):
            s_ref[slot, h] = lax.dot_general(kx_ref[h, pl.ds(r0, cw), :], qxs[h], nt, preferred_element_type=F32)

    score(0, 0)

    m0s = []
    for h in range(hps):
        k_own = k_ref[pl.ds(own_r0, bs), h * dh:(h + 1) * dh]
        s = lax.dot_general(k_own, qs[h], nt, preferred_element_type=F32)
        s = jnp.where(kidx <= qidx, s, -jnp.inf)
        m0 = jnp.max(s, axis=0, keepdims=True)
        p = jnp.exp2(s - m0)
        acc_ref[h] = jnp.dot(vt_ref[h, qb], p.astype(BF16), preferred_element_type=F32)
        m0s.append(m0)

    def consume(c, slot, ms):
        new_ms = []
        for h in range(hps):
            maxes, pvs = [], []
            for g in range(grp):
                sg = s_ref[slot, h, g * bs:(g + 1) * bs, :]
                mg = jnp.max(sg, axis=0, keepdims=True)
                p = jnp.exp2(sg - mg).astype(BF16)
                pvs.append(jnp.dot(vt_ref[h, c * grp + g], p, preferred_element_type=F32))
                maxes.append(mg)
            mx = ms[h]
            for mg in maxes:
                mx = jnp.maximum(mx, mg)
            acc = acc_ref[h] * jnp.exp2(ms[h] - mx)
            for mg, pv in zip(maxes, pvs):
                acc = acc + pv * jnp.exp2(mg - mx)
            acc_ref[h] = acc
            new_ms.append(mx)
        return tuple(new_ms)

    def body(u, ms):
        c = 2 * u
        score(c + 1, 1)
        ms = consume(c, 0, ms)
        score(c + 2, 0)
        return consume(c + 1, 1, ms)

    n_chunks = lax.div(qb + (grp - 1), grp)
    lax.fori_loop(0, lax.div(n_chunks + 1, 2), body, tuple(m0s))

    for h in range(hps):
        acc = acc_ref[h]
        out = acc[0:dh, :] / acc[dh:dh + 1, :]
        o_ref[:, h * dh:(h + 1) * dh] = out.T.astype(o_ref.dtype)


def _moba_attention(qkv, bsz, seq):
    m, width = qkv.shape
    d = width // 3
    dh = d // MB_HEADS
    bs = MB_BLOCK
    hps = MB_HEADS_PER_STEP
    nb = seq // bs
    grp = MB_GROUP
    nbp = (nb + grp - 1) // grp * grp + 2 * grp
    assert seq % bs == 0 and dh == V7X_LANES and MB_HEADS % hps == 0 and nb < dh and nb % MB_SELECT_UNROLL == 0
    dh_ext = dh + V7X_BF16_SUBLANES
    wblk = hps * dh
    nhb = MB_HEADS // hps
    blk = (2 * (bs * wblk * 2 + 3 * seq * wblk * 2) + hps * nbp * bs * (2 * dh + dh_ext) * 2 + hps * seq * dh * 2
           + 2 * hps * grp * bs * bs * 4 + hps * (4 * grp + 8) * bs * bs * 4)
    return pl.pallas_call(
        functools.partial(_moba_kernel, nb=nb, bs=bs, dh=dh, hps=hps, grp=grp),
        out_shape=jax.ShapeDtypeStruct((m, d), BF16),
        grid=(bsz, nhb, nb),
        in_specs=[pl.BlockSpec((seq, wblk), lambda b, h, i: (b, h)),
                  pl.BlockSpec((seq, wblk), lambda b, h, i: (b, nhb + h)),
                  pl.BlockSpec((seq, wblk), lambda b, h, i: (b, 2 * nhb + h))],
        out_specs=pl.BlockSpec((bs, wblk), lambda b, h, i: (b * nb + i, h)),
        scratch_shapes=[pltpu.VMEM((hps, nbp * bs, 2 * dh), BF16),
                        pltpu.VMEM((hps, nbp, dh_ext, bs), BF16),
                        pltpu.VMEM((hps, nb, dh), F32),
                        pltpu.VMEM((hps, seq, dh), BF16),
                        pltpu.VMEM((2, hps, grp * bs, bs), F32),
                        pltpu.VMEM((hps, dh_ext, bs), F32)],
        compiler_params=_params(("arbitrary", "arbitrary", "arbitrary"), blk),
        name="moba_attention",
    )(qkv, qkv, qkv)


def _ffn_up_kernel(xprev_ref, x_ref, wg_ref, wu_ref, cwg_ref, cwu_ref, cbg_ref, cbu_ref, o_ref,
                   xcat_ref, hg_ref, hu_ref, *, tiles_per_seq):
    tm = x_ref.shape[0]

    @pl.when(pl.program_id(1) == 0)
    def _():
        prev = xprev_ref[...]
        seq_start = (pl.program_id(0) % tiles_per_seq) == 0
        xcat_ref[0:CONV_HALO, :] = jnp.where(seq_start, jnp.zeros_like(prev), prev)
        xcat_ref[CONV_HALO:, :] = x_ref[...]

    xc = xcat_ref[...]
    hg_ref[...] = jnp.dot(xc, wg_ref[...].astype(BF16), preferred_element_type=F32)
    hu_ref[...] = jnp.dot(xc, wu_ref[...].astype(BF16), preferred_element_type=F32)

    def conv(h_ref, cw_ref, cb_ref):
        acc = cb_ref[...]
        for t in range(CONV_WIDTH):
            acc = acc + cw_ref[t:t + 1, :] * h_ref[pl.ds(CONV_HALO - (CONV_WIDTH - 1) + t, tm), :]
        return acc

    gate = conv(hg_ref, cwg_ref, cbg_ref)
    up = conv(hu_ref, cwu_ref, cbu_ref)
    o_ref[...] = (gate * _sigmoid(gate) * up).astype(o_ref.dtype)


def _ffn_up(h, w_up, layer, conv_w, conv_b, seq):
    m, d = h.shape
    f = w_up.shape[-1] // 2
    tm = min(MM_ROWS, seq)
    tn = min(FFN_COLS, f)
    assert seq % tm == 0 and f % tn == 0 and tm % CONV_HALO == 0
    nj = f // tn
    halo_blocks = tm // CONV_HALO
    blk = (2 * (tm * d * 2 + CONV_HALO * d * 2 + 2 * d * tn * 4 + tm * tn * 2) + 2 * d * tn * 6
           + (tm + CONV_HALO) * d * 2 + 2 * (tm + CONV_HALO) * tn * 4 + 6 * tm * tn * 4)
    return pl.pallas_call(
        functools.partial(_ffn_up_kernel, tiles_per_seq=seq // tm),
        out_shape=jax.ShapeDtypeStruct((m, f), BF16),
        grid=(m // tm, nj),
        in_specs=[pl.BlockSpec((CONV_HALO, d), lambda i, j: (jnp.maximum(i * halo_blocks - 1, 0), 0)),
                  pl.BlockSpec((tm, d), lambda i, j: (i, 0)),
                  pl.BlockSpec((None, d, tn), lambda i, j: (layer, 0, j)),
                  pl.BlockSpec((None, d, tn), lambda i, j: (layer, 0, nj + j)),
                  pl.BlockSpec((CONV_WIDTH, tn), lambda i, j: (0, j)),
                  pl.BlockSpec((CONV_WIDTH, tn), lambda i, j: (0, nj + j)),
                  pl.BlockSpec((1, tn), lambda i, j: (0, j)),
                  pl.BlockSpec((1, tn), lambda i, j: (0, nj + j))],
        out_specs=pl.BlockSpec((tm, tn), lambda i, j: (i, j)),
        scratch_shapes=[pltpu.VMEM((tm + CONV_HALO, d), BF16),
                        pltpu.VMEM((tm + CONV_HALO, tn), F32),
                        pltpu.VMEM((tm + CONV_HALO, tn), F32)],
        compiler_params=_params(("arbitrary", "arbitrary"), blk),
        name="ffn_up_conv_gate",
    )(h, h, w_up, w_up, conv_w, conv_w, conv_b.reshape(1, 2 * f), conv_b.reshape(1, 2 * f))


def _conv_ffn(x, norm_g, w_up, layer, conv_w, conv_b, w_down, seq):
    h = _rmsnorm(x, norm_g, BF16)
    act = _ffn_up(h, w_up, layer, conv_w, conv_b, seq)
    return _matmul(act, w_down.astype(BF16), F32, residual=x, rows=DOWN_ROWS, cols=DOWN_COLS)


def kernel(x, norm_mix, norm_ffn, a_w_in, a_gate_bias, a_head_norm, a_w_out, b_w_qkv, b_w_out,
           ffn_w_up, ffn_conv_w, ffn_conv_b, ffn_w_down, final_norm):
    bsz, seq, d = x.shape
    m = bsz * seq
    x = x.reshape(m, d)

    w_in = a_w_in[0]
    n_main = w_in.shape[1] - 2 * ML_HEADS
    w_gates = jnp.pad(w_in[:, n_main:], ((0, 0), (0, V7X_LANES - 2 * ML_HEADS)))
    gate_bias = jnp.pad(a_gate_bias[0], (0, V7X_LANES - 2 * ML_HEADS)).reshape(1, V7X_LANES)
    h, gates = _rmsnorm_gates(x, norm_mix[0], w_gates, gate_bias)
    proj = _matmul(h, a_w_in, BF16, n=n_main, layer=0)
    mixed = _mlstm_scan(proj, gates, a_head_norm[0], bsz, seq)
    x = _matmul(mixed, a_w_out, F32, residual=x, layer=0)
    x = _conv_ffn(x, norm_ffn[0], ffn_w_up, 0, ffn_conv_w[0], ffn_conv_b[0], ffn_w_down[0], seq)

    h = _rmsnorm(x, norm_mix[1], BF16)
    q_scale = jnp.where(jnp.arange(3 * d) < d, (d // MB_HEADS) ** -0.5 * LOG2_E, 1.0).astype(F32)
    qkv = _matmul(h, b_w_qkv, BF16, layer=0, col_scale=q_scale.reshape(1, 3 * d))
    attn = _moba_attention(qkv, bsz, seq)
    x = _matmul(attn, b_w_out, F32, residual=x, layer=0)
    x = _conv_ffn(x, norm_ffn[1], ffn_w_up, 1, ffn_conv_w[1], ffn_conv_b[1], ffn_w_down[1], seq)

    return _rmsnorm(x, final_norm, F32).reshape(bsz, seq, d)
```

```python
import functools
import math

import jax
import jax.numpy as jnp
from jax import lax
from jax.experimental import pallas as pl
from jax.experimental.pallas import tpu as pltpu

F32 = jnp.float32
BF16 = jnp.bfloat16

NORM_EPS = 1e-6
ML_HEADS = 8
GATE_SOFTCAP = 15.0
MB_HEADS = 32
MB_BLOCK = 256
MB_TOPK = 3
CONV_WIDTH = 3

V7X_LANES = 128
V7X_BF16_SUBLANES = 16
V7X_VMEM_BYTES = 64 * 1024 * 1024
VMEM_RESERVE_BYTES = 6 * 1024 * 1024

ML_CHUNK = 256
NORM_ROWS = 256
MM_ROWS = 1024
MM_COLS = 512
PROJ_ROWS = 2048
PROJ_COLS = 256
DOWN_ROWS = 512
DOWN_COLS = 512
FFN_ROWS = 1024
FFN_COLS = 256
CONV_HALO = V7X_BF16_SUBLANES
MB_GROUP = 2
MB_HEADS_PER_STEP = 2
MB_QBLOCKS_PER_STEP = 2

LOG2_E = math.log2(math.e)
MB_MASK = 2.0 ** 100


def _vmem_limit(block_bytes):
    want = int(block_bytes) + VMEM_RESERVE_BYTES
    return max(min(want, V7X_VMEM_BYTES - VMEM_RESERVE_BYTES), 16 * 1024 * 1024)


def _params(sem, block_bytes):
    return pltpu.CompilerParams(dimension_semantics=sem, vmem_limit_bytes=_vmem_limit(block_bytes))


def _sigmoid(x):
    return 1.0 / (1.0 + jnp.exp(-x))


def _rmsnorm_kernel(x_ref, g_ref, o_ref):
    x = x_ref[...]
    ms = jnp.mean(x * x, axis=-1, keepdims=True)
    o_ref[...] = (x * lax.rsqrt(ms + NORM_EPS) * g_ref[...]).astype(o_ref.dtype)


def _rmsnorm(x, g, out_dtype):
    m, d = x.shape
    tm = min(NORM_ROWS, m)
    assert m % tm == 0
    blk = 2 * tm * d * (4 + jnp.dtype(out_dtype).itemsize) + 3 * tm * d * 4
    return pl.pallas_call(
        _rmsnorm_kernel,
        out_shape=jax.ShapeDtypeStruct((m, d), out_dtype),
        grid=(m // tm,),
        in_specs=[pl.BlockSpec((tm, d), lambda i: (i, 0)), pl.BlockSpec((1, d), lambda i: (0, 0))],
        out_specs=pl.BlockSpec((tm, d), lambda i: (i, 0)),
        compiler_params=_params(("arbitrary",), blk),
        name="rmsnorm",
    )(x, g.reshape(1, d))


def _rmsnorm_gates_kernel(x_ref, g_ref, wg_ref, b_ref, o_ref, gate_ref):
    x = x_ref[...]
    ms = jnp.mean(x * x, axis=-1, keepdims=True)
    y = x * lax.rsqrt(ms + NORM_EPS) * g_ref[...]
    o_ref[...] = y.astype(o_ref.dtype)
    gate_ref[...] = jnp.dot(y, wg_ref[...], precision=lax.Precision.HIGHEST,
                            preferred_element_type=F32) + b_ref[...]


def _rmsnorm_gates(x, g, w_gates, bias):
    m, d = x.shape
    tm = min(NORM_ROWS, m)
    assert m % tm == 0 and w_gates.shape == (d, V7X_LANES)
    blk = 2 * tm * d * 6 + 2 * d * V7X_LANES * 4 + 4 * tm * d * 4
    return pl.pallas_call(
        _rmsnorm_gates_kernel,
        out_shape=(jax.ShapeDtypeStruct((m, d), BF16), jax.ShapeDtypeStruct((m, V7X_LANES), F32)),
        grid=(m // tm,),
        in_specs=[pl.BlockSpec((tm, d), lambda i: (i, 0)), pl.BlockSpec((1, d), lambda i: (0, 0)),
                  pl.BlockSpec((d, V7X_LANES), lambda i: (0, 0)), pl.BlockSpec((1, V7X_LANES), lambda i: (0, 0))],
        out_specs=(pl.BlockSpec((tm, d), lambda i: (i, 0)), pl.BlockSpec((tm, V7X_LANES), lambda i: (i, 0))),
        compiler_params=_params(("arbitrary",), blk),
        name="rmsnorm_gates",
    )(x, g.reshape(1, d), w_gates, bias)


def _mm_kernel(*refs, has_scale, has_res):
    a_ref, b_ref, o_ref = refs[0], refs[1], refs[-1]
    b = b_ref[...]
    if has_scale:
        b = b * refs[2][...]
    acc = jnp.dot(a_ref[...], b.astype(BF16), preferred_element_type=F32)
    if has_res:
        acc = acc + refs[-2][...]
    o_ref[...] = acc.astype(o_ref.dtype)


def _matmul(a, b, out_dtype, residual=None, n=None, layer=None, col_scale=None, rows=MM_ROWS, cols=MM_COLS):
    m, k = a.shape
    n = b.shape[-1] if n is None else n
    tm, tn = min(rows, m), min(cols, n)
    assert m % tm == 0 and n % tn == 0 and n <= b.shape[-1] and b.shape[-2] == k
    osz = jnp.dtype(out_dtype).itemsize
    blk = 2 * (tm * k * 2 + k * tn * b.dtype.itemsize + tm * tn * osz) + tm * tn * 4 + k * tn * 6
    if layer is None:
        b_spec = pl.BlockSpec((k, tn), lambda i, j: (0, j))
    else:
        b_spec = pl.BlockSpec((None, k, tn), lambda i, j: (layer, 0, j))
    in_specs = [pl.BlockSpec((tm, k), lambda i, j: (i, 0)), b_spec]
    args = [a, b]
    if col_scale is not None:
        in_specs.append(pl.BlockSpec((1, tn), lambda i, j: (0, j)))
        args.append(col_scale)
    if residual is not None:
        in_specs.append(pl.BlockSpec((tm, tn), lambda i, j: (i, j)))
        args.append(residual)
        blk += 2 * tm * tn * 4
    return pl.pallas_call(
        functools.partial(_mm_kernel, has_scale=col_scale is not None, has_res=residual is not None),
        out_shape=jax.ShapeDtypeStruct((m, n), out_dtype),
        grid=(m // tm, n // tn),
        in_specs=in_specs,
        out_specs=pl.BlockSpec((tm, tn), lambda i, j: (i, j)),
        compiler_params=_params(("arbitrary", "arbitrary"), blk),
        name="matmul_res" if residual is not None else "matmul",
    )(*args)


def _mlstm_kernel(q_ref, k_ref, v_ref, o_ref, g_ref, gain_ref, out_ref, c_ref, n_ref, m_ref, *, dk, dv):
    L = q_ref.shape[0]
    heads = ML_HEADS

    @pl.when(pl.program_id(1) == 0)
    def _():
        c_ref[...] = jnp.zeros_like(c_ref)
        n_ref[...] = jnp.zeros_like(n_ref)
        m_ref[...] = jnp.zeros_like(m_ref)

    rows = lax.broadcasted_iota(jnp.int32, (L, L), 0)
    cols = lax.broadcasted_iota(jnp.int32, (L, L), 1)
    causal = cols <= rows
    tril = causal.astype(F32)

    g = g_ref[...]
    gcap = GATE_SOFTCAP * jnp.tanh(g / GATE_SOFTCAP)
    log_f = jnp.minimum(gcap, 0.0) - jnp.log(1.0 + jnp.exp(-jnp.abs(gcap)))
    bcum = jnp.dot(tril, log_f, precision=lax.Precision.HIGHEST, preferred_element_type=F32)
    lane = lax.broadcasted_iota(jnp.int32, g.shape, 1)
    gates = jnp.where(lane < heads, gcap, bcum)
    gates_t = gates.T

    scale = dk ** -0.5
    nt = (((1,), (1,)), ((), ()))
    tn = (((0,), (0,)), ((), ()))
    for hd in range(heads):
        q = q_ref[:, hd * dk:(hd + 1) * dk]
        k = k_ref[:, hd * dk:(hd + 1) * dk]
        v = v_ref[:, hd * dv:(hd + 1) * dv]
        i_col = gates[:, hd:hd + 1]
        b_col = gates[:, heads + hd:heads + hd + 1]
        i_row = gates_t[hd:hd + 1, :]
        b_row = gates_t[heads + hd:heads + hd + 1, :]
        m_prev = m_ref[hd, :, 0:1]
        c_prev = c_ref[hd]
        n_prev = n_ref[hd]

        dmat = jnp.where(causal, b_col - b_row + i_row, -jnp.inf)
        inter = b_col + m_prev
        m_row = jnp.maximum(jnp.max(dmat, axis=-1, keepdims=True), inter)
        a_inter = jnp.exp(inter - m_row)
        s = lax.dot_general(q, k, nt, preferred_element_type=F32) * scale * jnp.exp(dmat - m_row)
        num = (a_inter * jnp.dot(q, c_prev.astype(BF16), preferred_element_type=F32)
               + jnp.dot(s.astype(BF16), v, preferred_element_type=F32))
        den = (a_inter * jnp.sum(q.astype(F32) * n_prev, axis=-1, keepdims=True)
               + jnp.sum(s, axis=-1, keepdims=True))
        h = num / jnp.maximum(jnp.abs(den), jnp.exp(-m_row))

        b_last = b_col[L - 1:L, :]
        dec = b_last - b_col + i_col
        m_new = jnp.maximum(b_last + m_prev, jnp.max(dec, axis=0, keepdims=True))
        a_old = jnp.exp(b_last + m_prev - m_new)
        kw = k.astype(F32) * (jnp.exp(dec - m_new) * scale)
        c_ref[hd] = a_old * c_prev + lax.dot_general(kw.astype(BF16), v, tn, preferred_element_type=F32)
        n_ref[hd] = a_old * n_prev + jnp.sum(kw, axis=0, keepdims=True)
        m_ref[hd] = jnp.broadcast_to(m_new, m_ref.shape[1:])

        h = h * lax.rsqrt(jnp.mean(h * h, axis=-1, keepdims=True) + NORM_EPS)
        h = h * gain_ref[:, hd * dv:(hd + 1) * dv]
        h = h * _sigmoid(o_ref[:, hd * dv:(hd + 1) * dv].astype(F32))
        out_ref[:, hd * dv:(hd + 1) * dv] = h.astype(out_ref.dtype)


def _mlstm_scan(proj, gates, head_gain, bsz, seq):
    m, width = proj.shape
    d = head_gain.shape[0]
    dv = d // ML_HEADS
    dk = dv // 2
    hk = ML_HEADS * dk
    assert width == 2 * hk + 2 * d and hk * 2 == d
    L = min(ML_CHUNK, seq)
    assert seq % L == 0
    nc = seq // L
    row = lambda b, c: b * nc + c
    blk = 2 * (L * (2 * hk + 2 * d) * 2 + L * V7X_LANES * 4 + d * 4 + L * d * 2) \
        + ML_HEADS * dk * dv * 4 + 16 * L * L * 4 + 8 * L * dv * 4
    return pl.pallas_call(
        functools.partial(_mlstm_kernel, dk=dk, dv=dv),
        out_shape=jax.ShapeDtypeStruct((m, d), BF16),
        grid=(bsz, nc),
        in_specs=[pl.BlockSpec((L, hk), lambda b, c: (row(b, c), 0)),
                  pl.BlockSpec((L, hk), lambda b, c: (row(b, c), 1)),
                  pl.BlockSpec((L, d), lambda b, c: (row(b, c), 1)),
                  pl.BlockSpec((L, d), lambda b, c: (row(b, c), 2)),
                  pl.BlockSpec((L, V7X_LANES), lambda b, c: (row(b, c), 0)),
                  pl.BlockSpec((1, d), lambda b, c: (0, 0))],
        out_specs=pl.BlockSpec((L, d), lambda b, c: (row(b, c), 0)),
        scratch_shapes=[pltpu.VMEM((ML_HEADS, dk, dv), F32),
                        pltpu.VMEM((ML_HEADS, 1, dk), F32),
                        pltpu.VMEM((ML_HEADS, 1, V7X_LANES), F32)],
        compiler_params=_params(("arbitrary", "arbitrary"), blk),
        name="mlstm_scan",
    )(proj, proj, proj, proj, gates, head_gain.reshape(1, d))


def _moba_kernel(q_ref, k_ref, v_ref, o_ref, kx_ref, vt_ref, km_ref, s_ref, acc_ref, *,
                 nb, bs, dh, hps, grp, qps):
    step = pl.program_id(2)
    dh_ext = vt_ref.shape[2]
    nbp = vt_ref.shape[1]
    nt = (((1,), (1,)), ((), ()))
    cw = grp * bs

    @pl.when(step == 0)
    def _():
        lane = lax.broadcasted_iota(jnp.int32, (bs, dh), 1)

        def prep(n, carry):
            r0 = pl.multiple_of(n * bs, bs)
            onehot = (lane == n).astype(BF16)
            for h in range(hps):
                vb = v_ref[pl.ds(r0, bs), h * dh:(h + 1) * dh].astype(F32)
                vt_ref[h, n, 0:dh, :] = vb.T.astype(BF16)
                vt_ref[h, n, dh:dh_ext, :] = jnp.ones((dh_ext - dh, bs), BF16)
                kb = k_ref[pl.ds(r0, bs), h * dh:(h + 1) * dh]
                kx_ref[h, pl.ds(r0, bs), 0:dh] = kb
                kx_ref[h, pl.ds(r0, bs), dh:2 * dh] = onehot
                km_ref[h, pl.ds(n, 1), :] = jnp.mean(kb.astype(F32), axis=0, keepdims=True)
            return carry
        lax.fori_loop(0, nb, prep, 0)
        masked = (lane == nb).astype(BF16)
        for h in range(hps):
            for n in range(nb, nbp):
                vt_ref[h, n] = jnp.zeros((dh_ext, bs), BF16)
                kx_ref[h, n * bs:(n + 1) * bs, 0:dh] = jnp.zeros((bs, dh), BF16)
                kx_ref[h, n * bs:(n + 1) * bs, dh:2 * dh] = masked

    blk_idx = lax.broadcasted_iota(jnp.int32, (nb, bs), 0)
    row_idx = lax.broadcasted_iota(jnp.int32, (dh, bs), 0)
    kidx = lax.broadcasted_iota(jnp.int32, (bs, bs), 0)
    qidx = lax.broadcasted_iota(jnp.int32, (bs, bs), 1)

    streams = [(h, r) for h in range(hps) for r in range(qps)]
    qxs, m0s = [], []
    for t, (h, r) in enumerate(streams):
        qb = step * qps + r
        q = q_ref[r * bs:(r + 1) * bs, h * dh:(h + 1) * dh]

        km = km_ref[h]
        km_hi = km.astype(BF16)
        r1 = km - km_hi.astype(F32)
        km_mid = r1.astype(BF16)
        km_lo = (r1 - km_mid.astype(F32)).astype(BF16)
        gate = (lax.dot_general(km_hi, q, nt, preferred_element_type=F32)
                + lax.dot_general(km_mid, q, nt, preferred_element_type=F32)
                + lax.dot_general(km_lo, q, nt, preferred_element_type=F32))

        removed = blk_idx >= qb
        sel = jnp.zeros((nb, bs), F32)
        for rank in range(MB_TOPK):
            gm = jnp.where(removed, -jnp.inf, gate)
            mx = jnp.max(gm, axis=0, keepdims=True)
            cand = jnp.logical_and(jnp.logical_not(removed), gm == mx)
            idx = jnp.min(jnp.where(cand, blk_idx, nb), axis=0, keepdims=True)
            hit = blk_idx == idx
            sel = jnp.where(jnp.logical_and(hit, qb > rank), 1.0, sel)
            removed = jnp.logical_or(removed, hit)

        sel_pad = jnp.concatenate([sel, jnp.zeros((dh - nb, bs), F32)], axis=0)
        neg = jnp.where(jnp.logical_or(sel_pad > 0.0, row_idx > nb), 0.0, -MB_MASK)
        qxs.append(jnp.concatenate([q, neg.T.astype(BF16)], axis=1))

        k_own = k_ref[pl.ds(pl.multiple_of(qb * bs, bs), bs), h * dh:(h + 1) * dh]
        s = lax.dot_general(k_own, q, nt, preferred_element_type=F32)
        s = jnp.where(kidx <= qidx, s, -jnp.inf)
        m0 = jnp.max(s, axis=0, keepdims=True)
        p = jnp.exp2(s - m0)
        acc_ref[t] = jnp.dot(vt_ref[h, qb], p.astype(BF16), preferred_element_type=F32)
        m0s.append(m0)

    def score(c, slot):
        r0 = pl.multiple_of(c * cw, cw)
        for t, (h, r) in enumerate(streams):
            s_ref[slot, t] = lax.dot_general(kx_ref[h, pl.ds(r0, cw), :], qxs[t], nt, preferred_element_type=F32)

    def consume(c, slot, ms):
        new_ms = []
        for t, (h, r) in enumerate(streams):
            maxes, pvs = [], []
            for g in range(grp):
                sg = s_ref[slot, t, g * bs:(g + 1) * bs, :]
                mg = jnp.max(sg, axis=0, keepdims=True)
                p = jnp.exp2(sg - mg).astype(BF16)
                pvs.append(jnp.dot(vt_ref[h, c * grp + g], p, preferred_element_type=F32))
                maxes.append(mg)
            mx = ms[t]
            for mg in maxes:
                mx = jnp.maximum(mx, mg)
            acc = acc_ref[t] * jnp.exp2(ms[t] - mx)
            for mg, pv in zip(maxes, pvs):
                acc = acc + pv * jnp.exp2(mg - mx)
            acc_ref[t] = acc
            new_ms.append(mx)
        return tuple(new_ms)

    def body(u, ms):
        c = 2 * u
        score(c + 1, 1)
        ms = consume(c, 0, ms)
        score(c + 2, 0)
        return consume(c + 1, 1, ms)

    n_chunks = lax.div(step * qps + (qps - 1) + (grp - 1), grp)
    score(0, 0)
    lax.fori_loop(0, lax.div(n_chunks + 1, 2), body, tuple(m0s))

    for t, (h, r) in enumerate(streams):
        acc = acc_ref[t]
        out = acc[0:dh, :] / acc[dh:dh + 1, :]
        o_ref[r * bs:(r + 1) * bs, h * dh:(h + 1) * dh] = out.T.astype(o_ref.dtype)


def _moba_attention(qkv, bsz, seq):
    m, width = qkv.shape
    d = width // 3
    dh = d // MB_HEADS
    bs = MB_BLOCK
    hps = MB_HEADS_PER_STEP
    nb = seq // bs
    grp = MB_GROUP
    nbp = (nb + grp - 1) // grp * grp + 2 * grp
    qps = min(MB_QBLOCKS_PER_STEP, nb)
    assert seq % bs == 0 and dh == V7X_LANES and MB_HEADS % hps == 0 and nb < dh and nb % qps == 0
    dh_ext = dh + V7X_BF16_SUBLANES
    wblk = hps * dh
    nhb = MB_HEADS // hps
    ns = nb // qps
    nstream = hps * qps
    blk = (2 * (2 * qps * bs * wblk * 2 + 2 * seq * wblk * 2) + hps * nbp * bs * (2 * dh + dh_ext) * 2
           + 2 * nstream * grp * bs * bs * 4 + nstream * (4 * grp + 8) * bs * bs * 4)
    return pl.pallas_call(
        functools.partial(_moba_kernel, nb=nb, bs=bs, dh=dh, hps=hps, grp=grp, qps=qps),
        out_shape=jax.ShapeDtypeStruct((m, d), BF16),
        grid=(bsz, nhb, ns),
        in_specs=[pl.BlockSpec((qps * bs, wblk), lambda b, h, i: (b * ns + i, h)),
                  pl.BlockSpec((seq, wblk), lambda b, h, i: (b, nhb + h)),
                  pl.BlockSpec((seq, wblk), lambda b, h, i: (b, 2 * nhb + h))],
        out_specs=pl.BlockSpec((qps * bs, wblk), lambda b, h, i: (b * ns + i, h)),
        scratch_shapes=[pltpu.VMEM((hps, nbp * bs, 2 * dh), BF16),
                        pltpu.VMEM((hps, nbp, dh_ext, bs), BF16),
                        pltpu.VMEM((hps, nb, dh), F32),
                        pltpu.VMEM((2, nstream, grp * bs, bs), F32),
                        pltpu.VMEM((nstream, dh_ext, bs), F32)],
        compiler_params=_params(("arbitrary", "arbitrary", "arbitrary"), blk),
        name="moba_attention",
    )(qkv, qkv, qkv)


def _ffn_up_kernel(xprev_ref, x_ref, wg_ref, wu_ref, cwg_ref, cwu_ref, cbg_ref, cbu_ref, o_ref,
                   xcat_ref, hg_ref, hu_ref, *, tiles_per_seq):
    tm = x_ref.shape[0]

    @pl.when(pl.program_id(1) == 0)
    def _():
        prev = xprev_ref[...]
        seq_start = (pl.program_id(0) % tiles_per_seq) == 0
        xcat_ref[0:CONV_HALO, :] = jnp.where(seq_start, jnp.zeros_like(prev), prev)
        xcat_ref[CONV_HALO:, :] = x_ref[...]

    xc = xcat_ref[...]
    hg_ref[...] = jnp.dot(xc, wg_ref[...].astype(BF16), preferred_element_type=F32)
    hu_ref[...] = jnp.dot(xc, wu_ref[...].astype(BF16), preferred_element_type=F32)

    def conv(h_ref, cw_ref, cb_ref):
        acc = cb_ref[...]
        for t in range(CONV_WIDTH):
            acc = acc + cw_ref[t:t + 1, :] * h_ref[pl.ds(CONV_HALO - (CONV_WIDTH - 1) + t, tm), :]
        return acc

    gate = conv(hg_ref, cwg_ref, cbg_ref)
    up = conv(hu_ref, cwu_ref, cbu_ref)
    o_ref[...] = (gate * _sigmoid(gate) * up).astype(o_ref.dtype)


def _ffn_up(h, w_up, layer, conv_w, conv_b, seq):
    m, d = h.shape
    f = w_up.shape[-1] // 2
    tm = min(FFN_ROWS, seq)
    tn = min(FFN_COLS, f)
    assert seq % tm == 0 and f % tn == 0 and tm % CONV_HALO == 0
    nj = f // tn
    halo_blocks = tm // CONV_HALO
    blk = (2 * (tm * d * 2 + CONV_HALO * d * 2 + 2 * d * tn * 4 + tm * tn * 2) + 2 * d * tn * 6
           + (tm + CONV_HALO) * d * 2 + 2 * (tm + CONV_HALO) * tn * 4 + 6 * tm * tn * 4)
    return pl.pallas_call(
        functools.partial(_ffn_up_kernel, tiles_per_seq=seq // tm),
        out_shape=jax.ShapeDtypeStruct((m, f), BF16),
        grid=(m // tm, nj),
        in_specs=[pl.BlockSpec((CONV_HALO, d), lambda i, j: (jnp.maximum(i * halo_blocks - 1, 0), 0)),
                  pl.BlockSpec((tm, d), lambda i, j: (i, 0)),
                  pl.BlockSpec((None, d, tn), lambda i, j: (layer, 0, j)),
                  pl.BlockSpec((None, d, tn), lambda i, j: (layer, 0, nj + j)),
                  pl.BlockSpec((CONV_WIDTH, tn), lambda i, j: (0, j)),
                  pl.BlockSpec((CONV_WIDTH, tn), lambda i, j: (0, nj + j)),
                  pl.BlockSpec((1, tn), lambda i, j: (0, j)),
                  pl.BlockSpec((1, tn), lambda i, j: (0, nj + j))],
        out_specs=pl.BlockSpec((tm, tn), lambda i, j: (i, j)),
        scratch_shapes=[pltpu.VMEM((tm + CONV_HALO, d), BF16),
                        pltpu.VMEM((tm + CONV_HALO, tn), F32),
                        pltpu.VMEM((tm + CONV_HALO, tn), F32)],
        compiler_params=_params(("arbitrary", "arbitrary"), blk),
        name="ffn_up_conv_gate",
    )(h, h, w_up, w_up, conv_w, conv_w, conv_b.reshape(1, 2 * f), conv_b.reshape(1, 2 * f))


def _conv_ffn(x, norm_g, w_up, layer, conv_w, conv_b, w_down, seq):
    h = _rmsnorm(x, norm_g, BF16)
    act = _ffn_up(h, w_up, layer, conv_w, conv_b, seq)
    return _matmul(act, w_down.astype(BF16), F32, residual=x, rows=DOWN_ROWS, cols=DOWN_COLS)


def kernel(x, norm_mix, norm_ffn, a_w_in, a_gate_bias, a_head_norm, a_w_out, b_w_qkv, b_w_out,
           ffn_w_up, ffn_conv_w, ffn_conv_b, ffn_w_down, final_norm):
    bsz, seq, d = x.shape
    m = bsz * seq
    x = x.reshape(m, d)

    w_in = a_w_in[0]
    n_main = w_in.shape[1] - 2 * ML_HEADS
    w_gates = jnp.pad(w_in[:, n_main:], ((0, 0), (0, V7X_LANES - 2 * ML_HEADS)))
    gate_bias = jnp.pad(a_gate_bias[0], (0, V7X_LANES - 2 * ML_HEADS)).reshape(1, V7X_LANES)
    h, gates = _rmsnorm_gates(x, norm_mix[0], w_gates, gate_bias)
    proj = _matmul(h, a_w_in, BF16, n=n_main, layer=0, rows=PROJ_ROWS, cols=PROJ_COLS)
    mixed = _mlstm_scan(proj, gates, a_head_norm[0], bsz, seq)
    x = _matmul(mixed, a_w_out, F32, residual=x, layer=0)
    x = _conv_ffn(x, norm_ffn[0], ffn_w_up, 0, ffn_conv_w[0], ffn_conv_b[0], ffn_w_down[0], seq)

    h = _rmsnorm(x, norm_mix[1], BF16)
    q_scale = jnp.where(jnp.arange(3 * d) < d, (d // MB_HEADS) ** -0.5 * LOG2_E, 1.0).astype(F32)
    qkv = _matmul(h, b_w_qkv, BF16, layer=0, col_scale=q_scale.reshape(1, 3 * d), rows=PROJ_ROWS, cols=PROJ_COLS)
    attn = _moba_attention(qkv, bsz, seq)
    x = _matmul(attn, b_w_out, F32, residual=x, layer=0)
    x = _conv_ffn(x, norm_ffn[1], ffn_w_up, 1, ffn_conv_w[1], ffn_conv_b[1], ffn_w_down[1], seq)

    return _rmsnorm(x, final_norm, F32).reshape(bsz, seq, d)
```

```python
import functools
import math

import jax
import jax.numpy as jnp
from jax import lax
from jax.experimental import pallas as pl
from jax.experimental.pallas import tpu as pltpu

F32 = jnp.float32
BF16 = jnp.bfloat16

NORM_EPS = 1e-6
ML_HEADS = 8
GATE_SOFTCAP = 15.0
MB_HEADS = 32
MB_BLOCK = 256
MB_TOPK = 3
CONV_WIDTH = 3

V7X_LANES = 128
V7X_SUBLANES = 8
V7X_BF16_SUBLANES = 16
V7X_VMEM_BYTES = 64 * 1024 * 1024
VMEM_RESERVE_BYTES = 6 * 1024 * 1024

ML_CHUNK = 256
NORM_ROWS = 256
MM_ROWS = 1024
MM_COLS = 512
DOWN_ROWS = 512
DOWN_COLS = 512
FFN_ROWS = 1024
FFN_STAGE_SLABS = 4
FFN_COLS = 256
CONV_HALO = V7X_BF16_SUBLANES
MB_GROUP = 2
MB_HEADS_PER_STEP = 2
MB_QBLOCKS_PER_STEP = 2

LOG2_E = math.log2(math.e)
MB_MASK = 2.0 ** 100


def _vmem_limit(block_bytes):
    want = int(block_bytes) + VMEM_RESERVE_BYTES
    return max(min(want, V7X_VMEM_BYTES - VMEM_RESERVE_BYTES), 16 * 1024 * 1024)


def _params(sem, block_bytes):
    return pltpu.CompilerParams(dimension_semantics=sem, vmem_limit_bytes=_vmem_limit(block_bytes))


def _sigmoid(x):
    return 1.0 / (1.0 + jnp.exp(-x))


def _rmsnorm_kernel(x_ref, g_ref, o_ref):
    x = x_ref[...]
    ms = jnp.mean(x * x, axis=-1, keepdims=True)
    o_ref[...] = (x * lax.rsqrt(ms + NORM_EPS) * g_ref[...]).astype(o_ref.dtype)


def _rmsnorm(x, g, out_dtype):
    m, d = x.shape
    tm = min(NORM_ROWS, m)
    assert m % tm == 0
    blk = 2 * tm * d * (4 + jnp.dtype(out_dtype).itemsize) + 3 * tm * d * 4
    return pl.pallas_call(
        _rmsnorm_kernel,
        out_shape=jax.ShapeDtypeStruct((m, d), out_dtype),
        grid=(m // tm,),
        in_specs=[pl.BlockSpec((tm, d), lambda i: (i, 0)), pl.BlockSpec((1, d), lambda i: (0, 0))],
        out_specs=pl.BlockSpec((tm, d), lambda i: (i, 0)),
        compiler_params=_params(("arbitrary",), blk),
        name="rmsnorm",
    )(x, g.reshape(1, d))


def _rmsnorm_gates_kernel(x_ref, g_ref, wg_ref, b_ref, o_ref, gate_ref):
    x = x_ref[...]
    ms = jnp.mean(x * x, axis=-1, keepdims=True)
    y = x * lax.rsqrt(ms + NORM_EPS) * g_ref[...]
    o_ref[...] = y.astype(o_ref.dtype)
    gate_ref[...] = jnp.dot(y, wg_ref[...], precision=lax.Precision.HIGHEST,
                            preferred_element_type=F32) + b_ref[...]


def _rmsnorm_gates(x, g, w_gates, bias):
    m, d = x.shape
    tm = min(NORM_ROWS, m)
    assert m % tm == 0 and w_gates.shape == (d, V7X_LANES)
    blk = 2 * tm * d * 6 + 2 * d * V7X_LANES * 4 + 4 * tm * d * 4
    return pl.pallas_call(
        _rmsnorm_gates_kernel,
        out_shape=(jax.ShapeDtypeStruct((m, d), BF16), jax.ShapeDtypeStruct((m, V7X_LANES), F32)),
        grid=(m // tm,),
        in_specs=[pl.BlockSpec((tm, d), lambda i: (i, 0)), pl.BlockSpec((1, d), lambda i: (0, 0)),
                  pl.BlockSpec((d, V7X_LANES), lambda i: (0, 0)), pl.BlockSpec((1, V7X_LANES), lambda i: (0, 0))],
        out_specs=(pl.BlockSpec((tm, d), lambda i: (i, 0)), pl.BlockSpec((tm, V7X_LANES), lambda i: (i, 0))),
        compiler_params=_params(("arbitrary",), blk),
        name="rmsnorm_gates",
    )(x, g.reshape(1, d), w_gates, bias)


def _mm_kernel(*refs, has_scale, has_res):
    a_ref, b_ref, o_ref = refs[0], refs[1], refs[-1]
    b = b_ref[...]
    if has_scale:
        b = b * refs[2][...]
    acc = jnp.dot(a_ref[...], b.astype(BF16), preferred_element_type=F32)
    if has_res:
        acc = acc + refs[-2][...]
    o_ref[...] = acc.astype(o_ref.dtype)


def _matmul(a, b, out_dtype, residual=None, n=None, layer=None, col_scale=None, rows=MM_ROWS, cols=MM_COLS):
    m, k = a.shape
    n = b.shape[-1] if n is None else n
    tm, tn = min(rows, m), min(cols, n)
    assert m % tm == 0 and n % tn == 0 and n <= b.shape[-1] and b.shape[-2] == k
    osz = jnp.dtype(out_dtype).itemsize
    blk = 2 * (tm * k * 2 + k * tn * b.dtype.itemsize + tm * tn * osz) + tm * tn * 4 + k * tn * 6
    if layer is None:
        b_spec = pl.BlockSpec((k, tn), lambda i, j: (0, j))
    else:
        b_spec = pl.BlockSpec((None, k, tn), lambda i, j: (layer, 0, j))
    in_specs = [pl.BlockSpec((tm, k), lambda i, j: (i, 0)), b_spec]
    args = [a, b]
    if col_scale is not None:
        in_specs.append(pl.BlockSpec((1, tn), lambda i, j: (0, j)))
        args.append(col_scale)
    if residual is not None:
        in_specs.append(pl.BlockSpec((tm, tn), lambda i, j: (i, j)))
        args.append(residual)
        blk += 2 * tm * tn * 4
    return pl.pallas_call(
        functools.partial(_mm_kernel, has_scale=col_scale is not None, has_res=residual is not None),
        out_shape=jax.ShapeDtypeStruct((m, n), out_dtype),
        grid=(m // tm, n // tn),
        in_specs=in_specs,
        out_specs=pl.BlockSpec((tm, tn), lambda i, j: (i, j)),
        compiler_params=_params(("arbitrary", "arbitrary"), blk),
        name="matmul_res" if residual is not None else "matmul",
    )(*args)


def _mlstm_kernel(q_ref, k_ref, v_ref, o_ref, g_ref, gain_ref, out_ref, c_ref, n_ref, m_ref, *, dk, dv):
    L = q_ref.shape[0]
    heads = ML_HEADS

    @pl.when(pl.program_id(1) == 0)
    def _():
        c_ref[...] = jnp.zeros_like(c_ref)
        n_ref[...] = jnp.zeros_like(n_ref)
        m_ref[...] = jnp.zeros_like(m_ref)

    rows = lax.broadcasted_iota(jnp.int32, (L, L), 0)
    cols = lax.broadcasted_iota(jnp.int32, (L, L), 1)
    causal = cols <= rows
    tril = causal.astype(F32)

    g = g_ref[...]
    gcap = GATE_SOFTCAP * jnp.tanh(g / GATE_SOFTCAP)
    log_f = jnp.minimum(gcap, 0.0) - jnp.log(1.0 + jnp.exp(-jnp.abs(gcap)))
    bcum = jnp.dot(tril, log_f, precision=lax.Precision.HIGHEST, preferred_element_type=F32)
    lane = lax.broadcasted_iota(jnp.int32, g.shape, 1)
    gates = jnp.where(lane < heads, gcap, bcum)
    gates_t = gates.T

    scale = dk ** -0.5
    nt = (((1,), (1,)), ((), ()))
    tn = (((0,), (0,)), ((), ()))
    for hd in range(heads):
        q = q_ref[:, hd * dk:(hd + 1) * dk]
        k = k_ref[:, hd * dk:(hd + 1) * dk]
        v = v_ref[:, hd * dv:(hd + 1) * dv]
        i_col = gates[:, hd:hd + 1]
        b_col = gates[:, heads + hd:heads + hd + 1]
        i_row = gates_t[hd:hd + 1, :]
        b_row = gates_t[heads + hd:heads + hd + 1, :]
        m_prev = m_ref[hd, :, 0:1]
        c_prev = c_ref[hd]
        n_prev = n_ref[hd]

        dmat = jnp.where(causal, b_col - b_row + i_row, -jnp.inf)
        inter = b_col + m_prev
        m_row = jnp.maximum(jnp.max(dmat, axis=-1, keepdims=True), inter)
        a_inter = jnp.exp(inter - m_row)
        s = lax.dot_general(q, k, nt, preferred_element_type=F32) * scale * jnp.exp(dmat - m_row)
        num = (a_inter * jnp.dot(q, c_prev.astype(BF16), preferred_element_type=F32)
               + jnp.dot(s.astype(BF16), v, preferred_element_type=F32))
        den = (a_inter * jnp.sum(q.astype(F32) * n_prev, axis=-1, keepdims=True)
               + jnp.sum(s, axis=-1, keepdims=True))
        h = num / jnp.maximum(jnp.abs(den), jnp.exp(-m_row))

        b_last = b_col[L - 1:L, :]
        dec = b_last - b_col + i_col
        m_new = jnp.maximum(b_last + m_prev, jnp.max(dec, axis=0, keepdims=True))
        a_old = jnp.exp(b_last + m_prev - m_new)
        kw = k.astype(F32) * (jnp.exp(dec - m_new) * scale)
        c_ref[hd] = a_old * c_prev + lax.dot_general(kw.astype(BF16), v, tn, preferred_element_type=F32)
        n_ref[hd] = a_old * n_prev + jnp.sum(kw, axis=0, keepdims=True)
        m_ref[hd] = jnp.broadcast_to(m_new, m_ref.shape[1:])

        h = h * lax.rsqrt(jnp.mean(h * h, axis=-1, keepdims=True) + NORM_EPS)
        h = h * gain_ref[:, hd * dv:(hd + 1) * dv]
        h = h * _sigmoid(o_ref[:, hd * dv:(hd + 1) * dv].astype(F32))
        out_ref[:, hd * dv:(hd + 1) * dv] = h.astype(out_ref.dtype)


def _mlstm_scan(proj, gates, head_gain, bsz, seq):
    m, width = proj.shape
    d = head_gain.shape[0]
    dv = d // ML_HEADS
    dk = dv // 2
    hk = ML_HEADS * dk
    assert width == 2 * hk + 2 * d and hk * 2 == d
    L = min(ML_CHUNK, seq)
    assert seq % L == 0
    nc = seq // L
    row = lambda b, c: b * nc + c
    blk = 2 * (L * (2 * hk + 2 * d) * 2 + L * V7X_LANES * 4 + d * 4 + L * d * 2) \
        + ML_HEADS * dk * dv * 4 + 16 * L * L * 4 + 8 * L * dv * 4
    return pl.pallas_call(
        functools.partial(_mlstm_kernel, dk=dk, dv=dv),
        out_shape=jax.ShapeDtypeStruct((m, d), BF16),
        grid=(bsz, nc),
        in_specs=[pl.BlockSpec((L, hk), lambda b, c: (row(b, c), 0)),
                  pl.BlockSpec((L, hk), lambda b, c: (row(b, c), 1)),
                  pl.BlockSpec((L, d), lambda b, c: (row(b, c), 1)),
                  pl.BlockSpec((L, d), lambda b, c: (row(b, c), 2)),
                  pl.BlockSpec((L, V7X_LANES), lambda b, c: (row(b, c), 0)),
                  pl.BlockSpec((1, d), lambda b, c: (0, 0))],
        out_specs=pl.BlockSpec((L, d), lambda b, c: (row(b, c), 0)),
        scratch_shapes=[pltpu.VMEM((ML_HEADS, dk, dv), F32),
                        pltpu.VMEM((ML_HEADS, 1, dk), F32),
                        pltpu.VMEM((ML_HEADS, 1, V7X_LANES), F32)],
        compiler_params=_params(("arbitrary", "arbitrary"), blk),
        name="mlstm_scan",
    )(proj, proj, proj, proj, gates, head_gain.reshape(1, d))


def _moba_kernel(q_ref, k_ref, v_ref, o_ref, kx_ref, vt_ref, km_ref, s_ref, acc_ref, *,
                 nb, bs, dh, hps, grp, qps):
    step = pl.program_id(2)
    dh_ext = vt_ref.shape[2]
    nbp = vt_ref.shape[1]
    nt = (((1,), (1,)), ((), ()))
    cw = grp * bs

    @pl.when(step == 0)
    def _():
        lane = lax.broadcasted_iota(jnp.int32, (bs, dh), 1)

        def prep(n, carry):
            r0 = pl.multiple_of(n * bs, bs)
            onehot = (lane == n).astype(BF16)
            for h in range(hps):
                vb = v_ref[pl.ds(r0, bs), h * dh:(h + 1) * dh].astype(F32)
                vt_ref[h, n, 0:dh, :] = vb.T.astype(BF16)
                vt_ref[h, n, dh:dh_ext, :] = jnp.ones((dh_ext - dh, bs), BF16)
                kb = k_ref[pl.ds(r0, bs), h * dh:(h + 1) * dh]
                kx_ref[h, pl.ds(r0, bs), 0:dh] = kb
                kx_ref[h, pl.ds(r0, bs), dh:2 * dh] = onehot
                km_ref[h, pl.ds(n, 1), :] = jnp.mean(kb.astype(F32), axis=0, keepdims=True)
            return carry
        lax.fori_loop(0, nb, prep, 0)
        masked = (lane == nb).astype(BF16)
        for h in range(hps):
            for n in range(nb, nbp):
                vt_ref[h, n] = jnp.zeros((dh_ext, bs), BF16)
                kx_ref[h, n * bs:(n + 1) * bs, 0:dh] = jnp.zeros((bs, dh), BF16)
                kx_ref[h, n * bs:(n + 1) * bs, dh:2 * dh] = masked

    blk_idx = lax.broadcasted_iota(jnp.int32, (nb, bs), 0)
    row_idx = lax.broadcasted_iota(jnp.int32, (dh, bs), 0)
    kidx = lax.broadcasted_iota(jnp.int32, (bs, bs), 0)
    qidx = lax.broadcasted_iota(jnp.int32, (bs, bs), 1)

    streams = [(h, r) for h in range(hps) for r in range(qps)]
    qxs, m0s = [], []
    for t, (h, r) in enumerate(streams):
        qb = step * qps + r
        q = q_ref[r * bs:(r + 1) * bs, h * dh:(h + 1) * dh]

        km = km_ref[h]
        km_hi = km.astype(BF16)
        r1 = km - km_hi.astype(F32)
        km_mid = r1.astype(BF16)
        km_lo = (r1 - km_mid.astype(F32)).astype(BF16)
        gate = (lax.dot_general(km_hi, q, nt, preferred_element_type=F32)
                + lax.dot_general(km_mid, q, nt, preferred_element_type=F32)
                + lax.dot_general(km_lo, q, nt, preferred_element_type=F32))

        removed = blk_idx >= qb
        sel = jnp.zeros((nb, bs), F32)
        for rank in range(MB_TOPK):
            gm = jnp.where(removed, -jnp.inf, gate)
            mx = jnp.max(gm, axis=0, keepdims=True)
            cand = jnp.logical_and(jnp.logical_not(removed), gm == mx)
            idx = jnp.min(jnp.where(cand, blk_idx, nb), axis=0, keepdims=True)
            hit = blk_idx == idx
            sel = jnp.where(jnp.logical_and(hit, qb > rank), 1.0, sel)
            removed = jnp.logical_or(removed, hit)

        sel_pad = jnp.concatenate([sel, jnp.zeros((dh - nb, bs), F32)], axis=0)
        neg = jnp.where(jnp.logical_or(sel_pad > 0.0, row_idx > nb), 0.0, -MB_MASK)
        qxs.append(jnp.concatenate([q, neg.T.astype(BF16)], axis=1))

        k_own = k_ref[pl.ds(pl.multiple_of(qb * bs, bs), bs), h * dh:(h + 1) * dh]
        s = lax.dot_general(k_own, q, nt, preferred_element_type=F32)
        s = jnp.where(kidx <= qidx, s, -jnp.inf)
        m0 = jnp.max(s, axis=0, keepdims=True)
        p = jnp.exp2(s - m0)
        acc_ref[t] = jnp.dot(vt_ref[h, qb], p.astype(BF16), preferred_element_type=F32)
        m0s.append(m0)

    def score(c, slot):
        r0 = pl.multiple_of(c * cw, cw)
        for t, (h, r) in enumerate(streams):
            s_ref[slot, t] = lax.dot_general(kx_ref[h, pl.ds(r0, cw), :], qxs[t], nt, preferred_element_type=F32)

    def consume(c, slot, ms):
        new_ms = []
        for t, (h, r) in enumerate(streams):
            maxes, pvs = [], []
            for g in range(grp):
                sg = s_ref[slot, t, g * bs:(g + 1) * bs, :]
                mg = jnp.max(sg, axis=0, keepdims=True)
                p = jnp.exp2(sg - mg).astype(BF16)
                pvs.append(jnp.dot(vt_ref[h, c * grp + g], p, preferred_element_type=F32))
                maxes.append(mg)
            mx = ms[t]
            for mg in maxes:
                mx = jnp.maximum(mx, mg)
            acc = acc_ref[t] * jnp.exp2(ms[t] - mx)
            for mg, pv in zip(maxes, pvs):
                acc = acc + pv * jnp.exp2(mg - mx)
            acc_ref[t] = acc
            new_ms.append(mx)
        return tuple(new_ms)

    def body(u, ms):
        c = 2 * u
        score(c + 1, 1)
        ms = consume(c, 0, ms)
        score(c + 2, 0)
        return consume(c + 1, 1, ms)

    n_chunks = lax.div(step * qps + (qps - 1) + (grp - 1), grp)
    score(0, 0)
    lax.fori_loop(0, lax.div(n_chunks + 1, 2), body, tuple(m0s))

    for t, (h, r) in enumerate(streams):
        acc = acc_ref[t]
        out = acc[0:dh, :] / acc[dh:dh + 1, :]
        o_ref[r * bs:(r + 1) * bs, h * dh:(h + 1) * dh] = out.T.astype(o_ref.dtype)


def _moba_attention(qkv, bsz, seq):
    m, width = qkv.shape
    d = width // 3
    dh = d // MB_HEADS
    bs = MB_BLOCK
    hps = MB_HEADS_PER_STEP
    nb = seq // bs
    grp = MB_GROUP
    nbp = (nb + grp - 1) // grp * grp + 2 * grp
    qps = min(MB_QBLOCKS_PER_STEP, nb)
    assert seq % bs == 0 and dh == V7X_LANES and MB_HEADS % hps == 0 and nb < dh and nb % qps == 0
    dh_ext = dh + V7X_BF16_SUBLANES
    wblk = hps * dh
    nhb = MB_HEADS // hps
    ns = nb // qps
    nstream = hps * qps
    blk = (2 * (2 * qps * bs * wblk * 2 + 2 * seq * wblk * 2) + hps * nbp * bs * (2 * dh + dh_ext) * 2
           + 2 * nstream * grp * bs * bs * 4 + nstream * (4 * grp + 8) * bs * bs * 4)
    return pl.pallas_call(
        functools.partial(_moba_kernel, nb=nb, bs=bs, dh=dh, hps=hps, grp=grp, qps=qps),
        out_shape=jax.ShapeDtypeStruct((m, d), BF16),
        grid=(bsz, nhb, ns),
        in_specs=[pl.BlockSpec((qps * bs, wblk), lambda b, h, i: (b * ns + i, h)),
                  pl.BlockSpec((seq, wblk), lambda b, h, i: (b, nhb + h)),
                  pl.BlockSpec((seq, wblk), lambda b, h, i: (b, 2 * nhb + h))],
        out_specs=pl.BlockSpec((qps * bs, wblk), lambda b, h, i: (b * ns + i, h)),
        scratch_shapes=[pltpu.VMEM((hps, nbp * bs, 2 * dh), BF16),
                        pltpu.VMEM((hps, nbp, dh_ext, bs), BF16),
                        pltpu.VMEM((hps, nb, dh), F32),
                        pltpu.VMEM((2, nstream, grp * bs, bs), F32),
                        pltpu.VMEM((nstream, dh_ext, bs), F32)],
        compiler_params=_params(("arbitrary", "arbitrary", "arbitrary"), blk),
        name="moba_attention",
    )(qkv, qkv, qkv)


def _ffn_up_kernel(xprev_ref, x_ref, wg_ref, wu_ref, cwg_ref, cwu_ref, cbg_ref, cbu_ref, o_ref,
                   xperm_ref, stage_ref, hg_ref, hu_ref, unperm_ref, *, tiles_per_seq):
    tm, d = x_ref.shape
    rows = tm + CONV_HALO
    sub = V7X_SUBLANES
    seg = rows // sub
    tn = o_ref.shape[1]
    nslab = stage_ref.shape[0]

    @pl.when(pl.program_id(1) == 0)
    def _():
        seq_start = (pl.program_id(0) % tiles_per_seq) == 0
        prev = xprev_ref[...]
        prev = jnp.where(seq_start, jnp.zeros_like(prev), prev)
        for g in range(d // (nslab * V7X_LANES)):
            for sl in range(nslab):
                c0 = (g * nslab + sl) * V7X_LANES
                stage_ref[sl, 0:CONV_HALO, :] = prev[:, c0:c0 + V7X_LANES].astype(F32)
                stage_ref[sl, CONV_HALO:, :] = x_ref[:, c0:c0 + V7X_LANES].astype(F32)

            def gather(i, carry):
                v = 2 * i
                for sl in range(nslab):
                    c0 = (g * nslab + sl) * V7X_LANES
                    a = stage_ref[sl, pl.ds(v, sub, stride=seg), :]
                    b = stage_ref[sl, pl.ds(v + 1, sub, stride=seg), :]
                    r0 = pl.multiple_of(v * sub, 2 * sub)
                    xperm_ref[pl.ds(r0, 2 * sub), c0:c0 + V7X_LANES] = jnp.concatenate([a, b], axis=0).astype(BF16)
                return carry
            lax.fori_loop(0, seg // 2, gather, 0)

    xp = xperm_ref[...]
    hg_ref[...] = jnp.dot(xp, wg_ref[...].astype(BF16), preferred_element_type=F32)
    hu_ref[...] = jnp.dot(xp, wu_ref[...].astype(BF16), preferred_element_type=F32)

    def conv(h_ref, cw_ref, cb_ref):
        last = pltpu.roll(h_ref[rows - sub:rows, :], 1, 0)
        last2 = pltpu.roll(h_ref[rows - 2 * sub:rows - sub, :], 1, 0)
        back1 = jnp.concatenate([last, h_ref[0:rows - sub, :]], axis=0)
        back2 = jnp.concatenate([last2, last, h_ref[0:rows - 2 * sub, :]], axis=0)
        return (cb_ref[...] + cw_ref[2:3, :] * h_ref[...] + cw_ref[1:2, :] * back1 + cw_ref[0:1, :] * back2)

    gate = conv(hg_ref, cwg_ref, cbg_ref)
    up = conv(hu_ref, cwu_ref, cbu_ref)
    act = gate * _sigmoid(gate) * up
    for v in range(seg):
        for sl in range(tn // V7X_LANES):
            unperm_ref[sl, pl.ds(v, sub, stride=seg), :] = act[v * sub:(v + 1) * sub,
                                                               sl * V7X_LANES:(sl + 1) * V7X_LANES]
    o_ref[...] = jnp.concatenate([unperm_ref[sl, CONV_HALO:, :] for sl in range(tn // V7X_LANES)],
                                 axis=1).astype(o_ref.dtype)


def _ffn_up(h, w_up, layer, conv_w, conv_b, seq):
    m, d = h.shape
    f = w_up.shape[-1] // 2
    tm = min(FFN_ROWS, seq)
    tn = min(FFN_COLS, f)
    rows = tm + CONV_HALO
    assert seq % tm == 0 and f % tn == 0 and tm % CONV_HALO == 0 and CONV_WIDTH == 3
    assert rows % (2 * V7X_SUBLANES) == 0 and d % (FFN_STAGE_SLABS * V7X_LANES) == 0 and tn % V7X_LANES == 0
    nj = f // tn
    halo_blocks = tm // CONV_HALO
    blk = (2 * (tm * d * 2 + CONV_HALO * d * 2 + 2 * d * tn * 4 + tm * tn * 2) + 2 * d * tn * 6
           + rows * d * 2 + FFN_STAGE_SLABS * rows * V7X_LANES * 4 + 3 * rows * tn * 4 + 8 * tm * tn * 4)
    return pl.pallas_call(
        functools.partial(_ffn_up_kernel, tiles_per_seq=seq // tm),
        out_shape=jax.ShapeDtypeStruct((m, f), BF16),
        grid=(m // tm, nj),
        in_specs=[pl.BlockSpec((CONV_HALO, d), lambda i, j: (jnp.maximum(i * halo_blocks - 1, 0), 0)),
                  pl.BlockSpec((tm, d), lambda i, j: (i, 0)),
                  pl.BlockSpec((None, d, tn), lambda i, j: (layer, 0, j)),
                  pl.BlockSpec((None, d, tn), lambda i, j: (layer, 0, nj + j)),
                  pl.BlockSpec((CONV_WIDTH, tn), lambda i, j: (0, j)),
                  pl.BlockSpec((CONV_WIDTH, tn), lambda i, j: (0, nj + j)),
                  pl.BlockSpec((1, tn), lambda i, j: (0, j)),
                  pl.BlockSpec((1, tn), lambda i, j: (0, nj + j))],
        out_specs=pl.BlockSpec((tm, tn), lambda i, j: (i, j)),
        scratch_shapes=[pltpu.VMEM((rows, d), BF16),
                        pltpu.VMEM((FFN_STAGE_SLABS, rows, V7X_LANES), F32),
                        pltpu.VMEM((rows, tn), F32),
                        pltpu.VMEM((rows, tn), F32),
                        pltpu.VMEM((tn // V7X_LANES, rows, V7X_LANES), F32)],
        compiler_params=_params(("arbitrary", "arbitrary"), blk),
        name="ffn_up_conv_gate",
    )(h, h, w_up, w_up, conv_w, conv_w, conv_b.reshape(1, 2 * f), conv_b.reshape(1, 2 * f))


def _conv_ffn(x, norm_g, w_up, layer, conv_w, conv_b, w_down, seq):
    h = _rmsnorm(x, norm_g, BF16)
    act = _ffn_up(h, w_up, layer, conv_w, conv_b, seq)
    return _matmul(act, w_down, F32, residual=x, layer=layer, rows=DOWN_ROWS, cols=DOWN_COLS)


def kernel(x, norm_mix, norm_ffn, a_w_in, a_gate_bias, a_head_norm, a_w_out, b_w_qkv, b_w_out,
           ffn_w_up, ffn_conv_w, ffn_conv_b, ffn_w_down, final_norm):
    bsz, seq, d = x.shape
    m = bsz * seq
    x = x.reshape(m, d)

    w_down = ffn_w_down.astype(BF16)

    w_in = a_w_in[0]
    n_main = w_in.shape[1] - 2 * ML_HEADS
    w_gates = jnp.pad(w_in[:, n_main:], ((0, 0), (0, V7X_LANES - 2 * ML_HEADS)))
    gate_bias = jnp.pad(a_gate_bias[0], (0, V7X_LANES - 2 * ML_HEADS)).reshape(1, V7X_LANES)
    h, gates = _rmsnorm_gates(x, norm_mix[0], w_gates, gate_bias)
    proj = _matmul(h, a_w_in, BF16, n=n_main, layer=0)
    mixed = _mlstm_scan(proj, gates, a_head_norm[0], bsz, seq)
    x = _matmul(mixed, a_w_out, F32, residual=x, layer=0)
    x = _conv_ffn(x, norm_ffn[0], ffn_w_up, 0, ffn_conv_w[0], ffn_conv_b[0], w_down, seq)

    h = _rmsnorm(x, norm_mix[1], BF16)
    q_scale = jnp.where(jnp.arange(3 * d) < d, (d // MB_HEADS) ** -0.5 * LOG2_E, 1.0).astype(F32)
    qkv = _matmul(h, b_w_qkv, BF16, layer=0, col_scale=q_scale.reshape(1, 3 * d))
    attn = _moba_attention(qkv, bsz, seq)
    x = _matmul(attn, b_w_out, F32, residual=x, layer=0)
    x = _conv_ffn(x, norm_ffn[1], ffn_w_up, 1, ffn_conv_w[1], ffn_conv_b[1], w_down, seq)

    return _rmsnorm(x, final_norm, F32).reshape(bsz, seq, d)
```

```python
import functools
import math

import jax
import jax.numpy as jnp
from jax import lax
from jax.experimental import pallas as pl
from jax.experimental.pallas import tpu as pltpu

F32 = jnp.float32
BF16 = jnp.bfloat16

NORM_EPS = 1e-6
ML_HEADS = 8
GATE_SOFTCAP = 15.0
MB_HEADS = 32
MB_BLOCK = 256
MB_TOPK = 3
CONV_WIDTH = 3

V7X_LANES = 128
V7X_SUBLANES = 8
V7X_BF16_SUBLANES = 16
V7X_VMEM_BYTES = 64 * 1024 * 1024
VMEM_RESERVE_BYTES = 6 * 1024 * 1024

ML_CHUNK = 256
NORM_ROWS = 256
MM_ROWS = 1024
MM_COLS = 512
DOWN_ROWS = 512
DOWN_COLS = 512
FFN_ROWS = 1024
FFN_STAGE_SLABS = 4
FFN_COLS = 256
CONV_HALO = V7X_BF16_SUBLANES
MB_GROUP = 2
MB_HEADS_PER_STEP = 2
MB_QBLOCKS_PER_STEP = 2

LOG2_E = math.log2(math.e)
MB_MASK = 2.0 ** 100


def _vmem_limit(block_bytes):
    want = int(block_bytes) + VMEM_RESERVE_BYTES
    return max(min(want, V7X_VMEM_BYTES - VMEM_RESERVE_BYTES), 16 * 1024 * 1024)


def _params(sem, block_bytes):
    return pltpu.CompilerParams(dimension_semantics=sem, vmem_limit_bytes=_vmem_limit(block_bytes))


def _sigmoid(x):
    return 1.0 / (1.0 + jnp.exp(-x))


def _rmsnorm_kernel(x_ref, g_ref, o_ref):
    x = x_ref[...]
    ms = jnp.mean(x * x, axis=-1, keepdims=True)
    o_ref[...] = (x * lax.rsqrt(ms + NORM_EPS) * g_ref[...]).astype(o_ref.dtype)


def _rmsnorm(x, g, out_dtype):
    m, d = x.shape
    tm = min(NORM_ROWS, m)
    assert m % tm == 0
    blk = 2 * tm * d * (4 + jnp.dtype(out_dtype).itemsize) + 3 * tm * d * 4
    return pl.pallas_call(
        _rmsnorm_kernel,
        out_shape=jax.ShapeDtypeStruct((m, d), out_dtype),
        grid=(m // tm,),
        in_specs=[pl.BlockSpec((tm, d), lambda i: (i, 0)), pl.BlockSpec((1, d), lambda i: (0, 0))],
        out_specs=pl.BlockSpec((tm, d), lambda i: (i, 0)),
        compiler_params=_params(("arbitrary",), blk),
        name="rmsnorm",
    )(x, g.reshape(1, d))


def _rmsnorm_gates_kernel(x_ref, g_ref, wg_ref, b_ref, o_ref, gate_ref):
    x = x_ref[...]
    ms = jnp.mean(x * x, axis=-1, keepdims=True)
    y = x * lax.rsqrt(ms + NORM_EPS) * g_ref[...]
    o_ref[...] = y.astype(o_ref.dtype)
    y_hi = y.astype(BF16)
    y_lo = (y - y_hi.astype(F32)).astype(BF16)
    w = wg_ref[...]
    w_hi = w.astype(BF16)
    w_lo = (w - w_hi.astype(F32)).astype(BF16)
    gate_ref[...] = (jnp.dot(y_hi, w_hi, preferred_element_type=F32) + jnp.dot(y_hi, w_lo, preferred_element_type=F32)
                     + jnp.dot(y_lo, w_hi, preferred_element_type=F32) + b_ref[...])


def _rmsnorm_gates(x, g, w_gates, bias):
    m, d = x.shape
    tm = min(NORM_ROWS, m)
    assert m % tm == 0 and w_gates.shape == (d, V7X_LANES)
    blk = 2 * tm * d * 6 + 2 * d * V7X_LANES * 4 + 4 * tm * d * 4
    return pl.pallas_call(
        _rmsnorm_gates_kernel,
        out_shape=(jax.ShapeDtypeStruct((m, d), BF16), jax.ShapeDtypeStruct((m, V7X_LANES), F32)),
        grid=(m // tm,),
        in_specs=[pl.BlockSpec((tm, d), lambda i: (i, 0)), pl.BlockSpec((1, d), lambda i: (0, 0)),
                  pl.BlockSpec((d, V7X_LANES), lambda i: (0, 0)), pl.BlockSpec((1, V7X_LANES), lambda i: (0, 0))],
        out_specs=(pl.BlockSpec((tm, d), lambda i: (i, 0)), pl.BlockSpec((tm, V7X_LANES), lambda i: (i, 0))),
        compiler_params=_params(("arbitrary",), blk),
        name="rmsnorm_gates",
    )(x, g.reshape(1, d), w_gates, bias)


def _mm_kernel(*refs, has_scale, has_res, b_is_nk):
    a_ref, b_ref, o_ref = refs[0], refs[1], refs[-1]
    b = b_ref[...]
    if has_scale:
        b = b * refs[2][...]
    contract = (((1,), (1 if b_is_nk else 0,)), ((), ()))
    acc = lax.dot_general(a_ref[...], b.astype(BF16), contract, preferred_element_type=F32)
    if has_res:
        acc = acc + refs[-2][...]
    o_ref[...] = acc.astype(o_ref.dtype)


def _matmul(a, b, out_dtype, residual=None, n=None, layer=None, col_scale=None, b_is_nk=False,
            rows=MM_ROWS, cols=MM_COLS):
    m, k = a.shape
    n_axis, k_axis = (-2, -1) if b_is_nk else (-1, -2)
    n = b.shape[n_axis] if n is None else n
    tm, tn = min(rows, m), min(cols, n)
    assert m % tm == 0 and n % tn == 0 and n <= b.shape[n_axis] and b.shape[k_axis] == k
    assert not (b_is_nk and col_scale is not None)
    osz = jnp.dtype(out_dtype).itemsize
    blk = 2 * (tm * k * 2 + k * tn * b.dtype.itemsize + tm * tn * osz) + tm * tn * 4 + k * tn * 6
    b_block, b_index = ((tn, k), lambda j: (j, 0)) if b_is_nk else ((k, tn), lambda j: (0, j))
    if layer is None:
        b_spec = pl.BlockSpec(b_block, lambda i, j: b_index(j))
    else:
        b_spec = pl.BlockSpec((None,) + b_block, lambda i, j: (layer,) + b_index(j))
    in_specs = [pl.BlockSpec((tm, k), lambda i, j: (i, 0)), b_spec]
    args = [a, b]
    if col_scale is not None:
        in_specs.append(pl.BlockSpec((1, tn), lambda i, j: (0, j)))
        args.append(col_scale)
    if residual is not None:
        in_specs.append(pl.BlockSpec((tm, tn), lambda i, j: (i, j)))
        args.append(residual)
        blk += 2 * tm * tn * 4
    return pl.pallas_call(
        functools.partial(_mm_kernel, has_scale=col_scale is not None, has_res=residual is not None, b_is_nk=b_is_nk),
        out_shape=jax.ShapeDtypeStruct((m, n), out_dtype),
        grid=(m // tm, n // tn),
        in_specs=in_specs,
        out_specs=pl.BlockSpec((tm, tn), lambda i, j: (i, j)),
        compiler_params=_params(("arbitrary", "arbitrary"), blk),
        name="matmul_res" if residual is not None else "matmul",
    )(*args)


def _mlstm_kernel(q_ref, k_ref, v_ref, o_ref, g_ref, gain_ref, out_ref, c_ref, n_ref, m_ref, *, dk, dv):
    L = q_ref.shape[0]
    heads = ML_HEADS

    @pl.when(pl.program_id(1) == 0)
    def _():
        c_ref[...] = jnp.zeros_like(c_ref)
        n_ref[...] = jnp.zeros_like(n_ref)
        m_ref[...] = jnp.zeros_like(m_ref)

    rows = lax.broadcasted_iota(jnp.int32, (L, L), 0)
    cols = lax.broadcasted_iota(jnp.int32, (L, L), 1)
    causal = cols <= rows
    tril = causal.astype(F32)

    g = g_ref[...]
    gcap = GATE_SOFTCAP * jnp.tanh(g / GATE_SOFTCAP)
    log_f = jnp.minimum(gcap, 0.0) - jnp.log(1.0 + jnp.exp(-jnp.abs(gcap)))
    bcum = jnp.dot(tril, log_f, precision=lax.Precision.HIGHEST, preferred_element_type=F32)
    lane = lax.broadcasted_iota(jnp.int32, g.shape, 1)
    gates = jnp.where(lane < heads, gcap, bcum)
    gates_t = gates.T

    scale = dk ** -0.5
    nt = (((1,), (1,)), ((), ()))
    tn = (((0,), (0,)), ((), ()))
    for hd in range(heads):
        q = q_ref[:, hd * dk:(hd + 1) * dk]
        k = k_ref[:, hd * dk:(hd + 1) * dk]
        v = v_ref[:, hd * dv:(hd + 1) * dv]
        i_col = gates[:, hd:hd + 1]
        b_col = gates[:, heads + hd:heads + hd + 1]
        i_row = gates_t[hd:hd + 1, :]
        b_row = gates_t[heads + hd:heads + hd + 1, :]
        m_prev = m_ref[hd, :, 0:1]
        c_prev = c_ref[hd]
        n_prev = n_ref[hd]

        dmat = jnp.where(causal, b_col - b_row + i_row, -jnp.inf)
        inter = b_col + m_prev
        m_row = jnp.maximum(jnp.max(dmat, axis=-1, keepdims=True), inter)
        a_inter = jnp.exp(inter - m_row)
        s = lax.dot_general(q, k, nt, preferred_element_type=F32) * scale * jnp.exp(dmat - m_row)
        num = (a_inter * jnp.dot(q, c_prev.astype(BF16), preferred_element_type=F32)
               + jnp.dot(s.astype(BF16), v, preferred_element_type=F32))
        den = (a_inter * jnp.sum(q.astype(F32) * n_prev, axis=-1, keepdims=True)
               + jnp.sum(s, axis=-1, keepdims=True))
        h = num / jnp.maximum(jnp.abs(den), jnp.exp(-m_row))

        b_last = b_col[L - 1:L, :]
        dec = b_last - b_col + i_col
        m_new = jnp.maximum(b_last + m_prev, jnp.max(dec, axis=0, keepdims=True))
        a_old = jnp.exp(b_last + m_prev - m_new)
        kw = k.astype(F32) * (jnp.exp(dec - m_new) * scale)
        c_ref[hd] = a_old * c_prev + lax.dot_general(kw.astype(BF16), v, tn, preferred_element_type=F32)
        n_ref[hd] = a_old * n_prev + jnp.sum(kw, axis=0, keepdims=True)
        m_ref[hd] = jnp.broadcast_to(m_new, m_ref.shape[1:])

        h = h * lax.rsqrt(jnp.mean(h * h, axis=-1, keepdims=True) + NORM_EPS)
        h = h * gain_ref[:, hd * dv:(hd + 1) * dv]
        h = h * _sigmoid(o_ref[:, hd * dv:(hd + 1) * dv].astype(F32))
        out_ref[:, hd * dv:(hd + 1) * dv] = h.astype(out_ref.dtype)


def _mlstm_scan(proj, gates, head_gain, bsz, seq):
    m, width = proj.shape
    d = head_gain.shape[0]
    dv = d // ML_HEADS
    dk = dv // 2
    hk = ML_HEADS * dk
    assert width == 2 * hk + 2 * d and hk * 2 == d
    L = min(ML_CHUNK, seq)
    assert seq % L == 0
    nc = seq // L
    row = lambda b, c: b * nc + c
    blk = 2 * (L * (2 * hk + 2 * d) * 2 + L * V7X_LANES * 4 + d * 4 + L * d * 2) \
        + ML_HEADS * dk * dv * 4 + 16 * L * L * 4 + 8 * L * dv * 4
    return pl.pallas_call(
        functools.partial(_mlstm_kernel, dk=dk, dv=dv),
        out_shape=jax.ShapeDtypeStruct((m, d), BF16),
        grid=(bsz, nc),
        in_specs=[pl.BlockSpec((L, hk), lambda b, c: (row(b, c), 0)),
                  pl.BlockSpec((L, hk), lambda b, c: (row(b, c), 1)),
                  pl.BlockSpec((L, d), lambda b, c: (row(b, c), 1)),
                  pl.BlockSpec((L, d), lambda b, c: (row(b, c), 2)),
                  pl.BlockSpec((L, V7X_LANES), lambda b, c: (row(b, c), 0)),
                  pl.BlockSpec((1, d), lambda b, c: (0, 0))],
        out_specs=pl.BlockSpec((L, d), lambda b, c: (row(b, c), 0)),
        scratch_shapes=[pltpu.VMEM((ML_HEADS, dk, dv), F32),
                        pltpu.VMEM((ML_HEADS, 1, dk), F32),
                        pltpu.VMEM((ML_HEADS, 1, V7X_LANES), F32)],
        compiler_params=_params(("arbitrary", "arbitrary"), blk),
        name="mlstm_scan",
    )(proj, proj, proj, proj, gates, head_gain.reshape(1, d))


def _moba_kernel(q_ref, k_ref, v_ref, o_ref, kx_ref, vt_ref, km_ref, s_ref, acc_ref, *,
                 nb, bs, dh, hps, grp, qps):
    step = pl.program_id(2)
    dh_ext = vt_ref.shape[2]
    nbp = vt_ref.shape[1]
    nt = (((1,), (1,)), ((), ()))
    cw = grp * bs

    @pl.when(step == 0)
    def _():
        lane = lax.broadcasted_iota(jnp.int32, (bs, dh), 1)

        def prep(n, carry):
            r0 = pl.multiple_of(n * bs, bs)
            onehot = (lane == n).astype(BF16)
            for h in range(hps):
                vb = v_ref[pl.ds(r0, bs), h * dh:(h + 1) * dh].astype(F32)
                vt_ref[h, n, 0:dh, :] = vb.T.astype(BF16)
                vt_ref[h, n, dh:dh_ext, :] = jnp.ones((dh_ext - dh, bs), BF16)
                kb = k_ref[pl.ds(r0, bs), h * dh:(h + 1) * dh]
                kx_ref[h, pl.ds(r0, bs), 0:dh] = kb
                kx_ref[h, pl.ds(r0, bs), dh:2 * dh] = onehot
                km_ref[h, pl.ds(n, 1), :] = jnp.mean(kb.astype(F32), axis=0, keepdims=True)
            return carry
        lax.fori_loop(0, nb, prep, 0)
        masked = (lane == nb).astype(BF16)
        for h in range(hps):
            for n in range(nb, nbp):
                vt_ref[h, n] = jnp.zeros((dh_ext, bs), BF16)
                kx_ref[h, n * bs:(n + 1) * bs, 0:dh] = jnp.zeros((bs, dh), BF16)
                kx_ref[h, n * bs:(n + 1) * bs, dh:2 * dh] = masked

    blk_idx = lax.broadcasted_iota(jnp.int32, (nb, bs), 0)
    row_idx = lax.broadcasted_iota(jnp.int32, (dh, bs), 0)
    kidx = lax.broadcasted_iota(jnp.int32, (bs, bs), 0)
    qidx = lax.broadcasted_iota(jnp.int32, (bs, bs), 1)

    streams = [(h, r) for h in range(hps) for r in range(qps)]
    qxs, m0s = [], []
    for t, (h, r) in enumerate(streams):
        qb = step * qps + r
        q = q_ref[r * bs:(r + 1) * bs, h * dh:(h + 1) * dh]

        km = km_ref[h]
        km_hi = km.astype(BF16)
        r1 = km - km_hi.astype(F32)
        km_mid = r1.astype(BF16)
        km_lo = (r1 - km_mid.astype(F32)).astype(BF16)
        gate = (lax.dot_general(km_hi, q, nt, preferred_element_type=F32)
                + lax.dot_general(km_mid, q, nt, preferred_element_type=F32)
                + lax.dot_general(km_lo, q, nt, preferred_element_type=F32))

        removed = blk_idx >= qb
        sel = jnp.zeros((nb, bs), F32)
        for rank in range(MB_TOPK):
            gm = jnp.where(removed, -jnp.inf, gate)
            mx = jnp.max(gm, axis=0, keepdims=True)
            cand = jnp.logical_and(jnp.logical_not(removed), gm == mx)
            idx = jnp.min(jnp.where(cand, blk_idx, nb), axis=0, keepdims=True)
            hit = blk_idx == idx
            sel = jnp.where(jnp.logical_and(hit, qb > rank), 1.0, sel)
            removed = jnp.logical_or(removed, hit)

        sel_pad = jnp.concatenate([sel, jnp.zeros((dh - nb, bs), F32)], axis=0)
        neg = jnp.where(jnp.logical_or(sel_pad > 0.0, row_idx > nb), 0.0, -MB_MASK)
        qxs.append(jnp.concatenate([q, neg.T.astype(BF16)], axis=1))

        k_own = k_ref[pl.ds(pl.multiple_of(qb * bs, bs), bs), h * dh:(h + 1) * dh]
        s = lax.dot_general(k_own, q, nt, preferred_element_type=F32)
        s = jnp.where(kidx <= qidx, s, -jnp.inf)
        m0 = jnp.max(s, axis=0, keepdims=True)
        p = jnp.exp2(s - m0)
        acc_ref[t] = jnp.dot(vt_ref[h, qb], p.astype(BF16), preferred_element_type=F32)
        m0s.append(m0)

    def score(c, slot):
        r0 = pl.multiple_of(c * cw, cw)
        for t, (h, r) in enumerate(streams):
            s_ref[slot, t] = lax.dot_general(kx_ref[h, pl.ds(r0, cw), :], qxs[t], nt, preferred_element_type=F32)

    def consume(c, slot, ms):
        new_ms = []
        for t, (h, r) in enumerate(streams):
            maxes, pvs = [], []
            for g in range(grp):
                sg = s_ref[slot, t, g * bs:(g + 1) * bs, :]
                mg = jnp.max(sg, axis=0, keepdims=True)
                p = jnp.exp2(sg - mg).astype(BF16)
                pvs.append(jnp.dot(vt_ref[h, c * grp + g], p, preferred_element_type=F32))
                maxes.append(mg)
            mx = ms[t]
            for mg in maxes:
                mx = jnp.maximum(mx, mg)
            acc = acc_ref[t] * jnp.exp2(ms[t] - mx)
            for mg, pv in zip(maxes, pvs):
                acc = acc + pv * jnp.exp2(mg - mx)
            acc_ref[t] = acc
            new_ms.append(mx)
        return tuple(new_ms)

    def body(u, ms):
        c = 2 * u
        score(c + 1, 1)
        ms = consume(c, 0, ms)
        score(c + 2, 0)
        return consume(c + 1, 1, ms)

    n_chunks = lax.div(step * qps + (qps - 1) + (grp - 1), grp)
    score(0, 0)
    ms = lax.fori_loop(0, lax.div(n_chunks, 2), body, tuple(m0s))

    @pl.when(lax.rem(n_chunks, 2) == 1)
    def _():
        consume(n_chunks - 1, 0, ms)

    for t, (h, r) in enumerate(streams):
        acc = acc_ref[t]
        out = acc[0:dh, :] / acc[dh:dh + 1, :]
        o_ref[r * bs:(r + 1) * bs, h * dh:(h + 1) * dh] = out.T.astype(o_ref.dtype)


def _moba_attention(qkv, bsz, seq):
    m, width = qkv.shape
    d = width // 3
    dh = d // MB_HEADS
    bs = MB_BLOCK
    hps = MB_HEADS_PER_STEP
    nb = seq // bs
    grp = MB_GROUP
    nbp = (nb + grp - 1) // grp * grp + 2 * grp
    qps = min(MB_QBLOCKS_PER_STEP, nb)
    assert seq % bs == 0 and dh == V7X_LANES and MB_HEADS % hps == 0 and nb < dh and nb % qps == 0
    dh_ext = dh + V7X_BF16_SUBLANES
    wblk = hps * dh
    nhb = MB_HEADS // hps
    ns = nb // qps
    nstream = hps * qps
    blk = (2 * (2 * qps * bs * wblk * 2 + 2 * seq * wblk * 2) + hps * nbp * bs * (2 * dh + dh_ext) * 2
           + 2 * nstream * grp * bs * bs * 4 + nstream * (4 * grp + 8) * bs * bs * 4)
    return pl.pallas_call(
        functools.partial(_moba_kernel, nb=nb, bs=bs, dh=dh, hps=hps, grp=grp, qps=qps),
        out_shape=jax.ShapeDtypeStruct((m, d), BF16),
        grid=(bsz, nhb, ns),
        in_specs=[pl.BlockSpec((qps * bs, wblk), lambda b, h, i: (b * ns + i, h)),
                  pl.BlockSpec((seq, wblk), lambda b, h, i: (b, nhb + h)),
                  pl.BlockSpec((seq, wblk), lambda b, h, i: (b, 2 * nhb + h))],
        out_specs=pl.BlockSpec((qps * bs, wblk), lambda b, h, i: (b * ns + i, h)),
        scratch_shapes=[pltpu.VMEM((hps, nbp * bs, 2 * dh), BF16),
                        pltpu.VMEM((hps, nbp, dh_ext, bs), BF16),
                        pltpu.VMEM((hps, nb, dh), F32),
                        pltpu.VMEM((2, nstream, grp * bs, bs), F32),
                        pltpu.VMEM((nstream, dh_ext, bs), F32)],
        compiler_params=_params(("arbitrary", "arbitrary", "arbitrary"), blk),
        name="moba_attention",
    )(qkv, qkv, qkv)


def _ffn_up_kernel(xprev_ref, x_ref, wg_ref, wu_ref, cwg_ref, cwu_ref, cbg_ref, cbu_ref, o_ref,
                   xperm_ref, stage_ref, hg_ref, hu_ref, unperm_ref, *, tiles_per_seq):
    tm, d = x_ref.shape
    rows = tm + CONV_HALO
    sub = V7X_SUBLANES
    seg = rows // sub
    tn = o_ref.shape[1]
    nslab = stage_ref.shape[0]

    @pl.when(pl.program_id(1) == 0)
    def _():
        seq_start = (pl.program_id(0) % tiles_per_seq) == 0
        prev = xprev_ref[...]
        prev = jnp.where(seq_start, jnp.zeros_like(prev), prev)
        for g in range(d // (nslab * V7X_LANES)):
            for sl in range(nslab):
                c0 = (g * nslab + sl) * V7X_LANES
                stage_ref[sl, 0:CONV_HALO, :] = prev[:, c0:c0 + V7X_LANES].astype(F32)
                stage_ref[sl, CONV_HALO:, :] = x_ref[:, c0:c0 + V7X_LANES].astype(F32)

            def gather(i, carry):
                v = 2 * i
                for sl in range(nslab):
                    c0 = (g * nslab + sl) * V7X_LANES
                    a = stage_ref[sl, pl.ds(v, sub, stride=seg), :]
                    b = stage_ref[sl, pl.ds(v + 1, sub, stride=seg), :]
                    r0 = pl.multiple_of(v * sub, 2 * sub)
                    xperm_ref[pl.ds(r0, 2 * sub), c0:c0 + V7X_LANES] = jnp.concatenate([a, b], axis=0).astype(BF16)
                return carry
            lax.fori_loop(0, seg // 2, gather, 0)

    xp = xperm_ref[...]
    hg_ref[...] = jnp.dot(xp, wg_ref[...].astype(BF16), preferred_element_type=F32)
    hu_ref[...] = jnp.dot(xp, wu_ref[...].astype(BF16), preferred_element_type=F32)

    def conv(h_ref, cw_ref, cb_ref):
        last = pltpu.roll(h_ref[rows - sub:rows, :], 1, 0)
        last2 = pltpu.roll(h_ref[rows - 2 * sub:rows - sub, :], 1, 0)
        back1 = jnp.concatenate([last, h_ref[0:rows - sub, :]], axis=0)
        back2 = jnp.concatenate([last2, last, h_ref[0:rows - 2 * sub, :]], axis=0)
        return (cb_ref[...] + cw_ref[2:3, :] * h_ref[...] + cw_ref[1:2, :] * back1 + cw_ref[0:1, :] * back2)

    gate = conv(hg_ref, cwg_ref, cbg_ref)
    up = conv(hu_ref, cwu_ref, cbu_ref)
    act = gate * _sigmoid(gate) * up
    for v in range(seg):
        for sl in range(tn // V7X_LANES):
            unperm_ref[sl, pl.ds(v, sub, stride=seg), :] = act[v * sub:(v + 1) * sub,
                                                               sl * V7X_LANES:(sl + 1) * V7X_LANES]
    o_ref[...] = jnp.concatenate([unperm_ref[sl, CONV_HALO:, :] for sl in range(tn // V7X_LANES)],
                                 axis=1).astype(o_ref.dtype)


def _ffn_up(h, w_up, layer, conv_w, conv_b, seq):
    m, d = h.shape
    f = w_up.shape[-1] // 2
    tm = min(FFN_ROWS, seq)
    tn = min(FFN_COLS, f)
    rows = tm + CONV_HALO
    assert seq % tm == 0 and f % tn == 0 and tm % CONV_HALO == 0 and CONV_WIDTH == 3
    assert rows % (2 * V7X_SUBLANES) == 0 and d % (FFN_STAGE_SLABS * V7X_LANES) == 0 and tn % V7X_LANES == 0
    nj = f // tn
    halo_blocks = tm // CONV_HALO
    blk = (2 * (tm * d * 2 + CONV_HALO * d * 2 + 2 * d * tn * 4 + tm * tn * 2) + 2 * d * tn * 6
           + rows * d * 2 + FFN_STAGE_SLABS * rows * V7X_LANES * 4 + 3 * rows * tn * 4 + 8 * tm * tn * 4)
    return pl.pallas_call(
        functools.partial(_ffn_up_kernel, tiles_per_seq=seq // tm),
        out_shape=jax.ShapeDtypeStruct((m, f), BF16),
        grid=(m // tm, nj),
        in_specs=[pl.BlockSpec((CONV_HALO, d), lambda i, j: (jnp.maximum(i * halo_blocks - 1, 0), 0)),
                  pl.BlockSpec((tm, d), lambda i, j: (i, 0)),
                  pl.BlockSpec((None, d, tn), lambda i, j: (layer, 0, j)),
                  pl.BlockSpec((None, d, tn), lambda i, j: (layer, 0, nj + j)),
                  pl.BlockSpec((CONV_WIDTH, tn), lambda i, j: (0, j)),
                  pl.BlockSpec((CONV_WIDTH, tn), lambda i, j: (0, nj + j)),
                  pl.BlockSpec((1, tn), lambda i, j: (0, j)),
                  pl.BlockSpec((1, tn), lambda i, j: (0, nj + j))],
        out_specs=pl.BlockSpec((tm, tn), lambda i, j: (i, j)),
        scratch_shapes=[pltpu.VMEM((rows, d), BF16),
                        pltpu.VMEM((FFN_STAGE_SLABS, rows, V7X_LANES), F32),
                        pltpu.VMEM((rows, tn), F32),
                        pltpu.VMEM((rows, tn), F32),
                        pltpu.VMEM((tn // V7X_LANES, rows, V7X_LANES), F32)],
        compiler_params=_params(("arbitrary", "arbitrary"), blk),
        name="ffn_up_conv_gate",
    )(h, h, w_up, w_up, conv_w, conv_w, conv_b.reshape(1, 2 * f), conv_b.reshape(1, 2 * f))


def _conv_ffn(x, norm_g, w_up, layer, conv_w, conv_b, w_down, seq):
    h = _rmsnorm(x, norm_g, BF16)
    act = _ffn_up(h, w_up, layer, conv_w, conv_b, seq)
    return _matmul(act, w_down, F32, residual=x, layer=layer, rows=DOWN_ROWS, cols=DOWN_COLS)


def kernel(x, norm_mix, norm_ffn, a_w_in, a_gate_bias, a_head_norm, a_w_out, b_w_qkv, b_w_out,
           ffn_w_up, ffn_conv_w, ffn_conv_b, ffn_w_down, final_norm):
    bsz, seq, d = x.shape
    m = bsz * seq
    x = x.reshape(m, d)

    w_down = ffn_w_down.astype(BF16)

    w_in_t = jnp.swapaxes(a_w_in, 1, 2)
    n_main = w_in_t.shape[1] - 2 * ML_HEADS
    w_gates = jnp.pad(w_in_t[0, n_main:, :].T, ((0, 0), (0, V7X_LANES - 2 * ML_HEADS)))
    gate_bias = jnp.pad(a_gate_bias[0], (0, V7X_LANES - 2 * ML_HEADS)).reshape(1, V7X_LANES)
    h, gates = _rmsnorm_gates(x, norm_mix[0], w_gates, gate_bias)
    proj = _matmul(h, w_in_t, BF16, n=n_main, layer=0, b_is_nk=True)
    mixed = _mlstm_scan(proj, gates, a_head_norm[0], bsz, seq)
    x = _matmul(mixed, a_w_out, F32, residual=x, layer=0)
    x = _conv_ffn(x, norm_ffn[0], ffn_w_up, 0, ffn_conv_w[0], ffn_conv_b[0], w_down, seq)

    h = _rmsnorm(x, norm_mix[1], BF16)
    q_scale = jnp.where(jnp.arange(3 * d) < d, (d // MB_HEADS) ** -0.5 * LOG2_E, 1.0).astype(F32)
    qkv = _matmul(h, b_w_qkv, BF16, layer=0, col_scale=q_scale.reshape(1, 3 * d))
    attn = _moba_attention(qkv, bsz, seq)
    x = _matmul(attn, b_w_out, F32, residual=x, layer=0)
    x = _conv_ffn(x, norm_ffn[1], ffn_w_up, 1, ffn_conv_w[1], ffn_conv_b[1], w_down, seq)

    return _rmsnorm(x, final_norm, F32).reshape(bsz, seq, d)
```

```python
import functools
import math

import jax
import jax.numpy as jnp
from jax import lax
from jax.experimental import pallas as pl
from jax.experimental.pallas import tpu as pltpu

F32 = jnp.float32
BF16 = jnp.bfloat16

NORM_EPS = 1e-6
ML_HEADS = 8
GATE_SOFTCAP = 15.0
MB_HEADS = 32
MB_BLOCK = 256
MB_TOPK = 3
CONV_WIDTH = 3

V7X_LANES = 128
V7X_SUBLANES = 8
V7X_BF16_SUBLANES = 16
V7X_VMEM_BYTES = 64 * 1024 * 1024
VMEM_RESERVE_BYTES = 6 * 1024 * 1024

ML_CHUNK = 256
NORM_ROWS = 256
MM_ROWS = 1024
MM_COLS = 512
DOWN_ROWS = 512
DOWN_COLS = 512
FFN_ROWS = 1024
FFN_STAGE_SLABS = 4
FFN_COLS = 256
CONV_HALO = V7X_BF16_SUBLANES
MB_GROUP = 2
MB_HEADS_PER_STEP = 2
MB_QBLOCKS_PER_STEP = 2

LOG2_E = math.log2(math.e)
MB_MASK = 2.0 ** 100


def _vmem_limit(block_bytes):
    want = int(block_bytes) + VMEM_RESERVE_BYTES
    return max(min(want, V7X_VMEM_BYTES - VMEM_RESERVE_BYTES), 16 * 1024 * 1024)


def _params(sem, block_bytes):
    return pltpu.CompilerParams(dimension_semantics=sem, vmem_limit_bytes=_vmem_limit(block_bytes))


def _sigmoid(x):
    return 1.0 / (1.0 + jnp.exp(-x))


def _rmsnorm_kernel(x_ref, g_ref, o_ref):
    x = x_ref[...]
    ms = jnp.mean(x * x, axis=-1, keepdims=True)
    o_ref[...] = (x * lax.rsqrt(ms + NORM_EPS) * g_ref[...]).astype(o_ref.dtype)


def _rmsnorm(x, g, out_dtype):
    m, d = x.shape
    tm = min(NORM_ROWS, m)
    assert m % tm == 0
    blk = 2 * tm * d * (4 + jnp.dtype(out_dtype).itemsize) + 3 * tm * d * 4
    return pl.pallas_call(
        _rmsnorm_kernel,
        out_shape=jax.ShapeDtypeStruct((m, d), out_dtype),
        grid=(m // tm,),
        in_specs=[pl.BlockSpec((tm, d), lambda i: (i, 0)), pl.BlockSpec((1, d), lambda i: (0, 0))],
        out_specs=pl.BlockSpec((tm, d), lambda i: (i, 0)),
        compiler_params=_params(("arbitrary",), blk),
        name="rmsnorm",
    )(x, g.reshape(1, d))


def _rmsnorm_gates_kernel(x_ref, g_ref, wg_ref, b_ref, o_ref, gate_ref):
    x = x_ref[...]
    ms = jnp.mean(x * x, axis=-1, keepdims=True)
    y = x * lax.rsqrt(ms + NORM_EPS) * g_ref[...]
    o_ref[...] = y.astype(o_ref.dtype)
    y_hi = y.astype(BF16)
    y_lo = (y - y_hi.astype(F32)).astype(BF16)
    w = wg_ref[...]
    w_hi = w.astype(BF16)
    w_lo = (w - w_hi.astype(F32)).astype(BF16)
    gate_ref[...] = (jnp.dot(y_hi, w_hi, preferred_element_type=F32) + jnp.dot(y_hi, w_lo, preferred_element_type=F32)
                     + jnp.dot(y_lo, w_hi, preferred_element_type=F32) + b_ref[...])


def _rmsnorm_gates(x, g, w_gates, bias):
    m, d = x.shape
    tm = min(NORM_ROWS, m)
    assert m % tm == 0 and w_gates.shape == (d, V7X_LANES)
    blk = 2 * tm * d * 6 + 2 * d * V7X_LANES * 4 + 4 * tm * d * 4
    return pl.pallas_call(
        _rmsnorm_gates_kernel,
        out_shape=(jax.ShapeDtypeStruct((m, d), BF16), jax.ShapeDtypeStruct((m, V7X_LANES), F32)),
        grid=(m // tm,),
        in_specs=[pl.BlockSpec((tm, d), lambda i: (i, 0)), pl.BlockSpec((1, d), lambda i: (0, 0)),
                  pl.BlockSpec((d, V7X_LANES), lambda i: (0, 0)), pl.BlockSpec((1, V7X_LANES), lambda i: (0, 0))],
        out_specs=(pl.BlockSpec((tm, d), lambda i: (i, 0)), pl.BlockSpec((tm, V7X_LANES), lambda i: (i, 0))),
        compiler_params=_params(("arbitrary",), blk),
        name="rmsnorm_gates",
    )(x, g.reshape(1, d), w_gates, bias)


def _row_rsqrt(ss, width):
    return lax.rsqrt(jnp.sum(ss, axis=-1, keepdims=True) * (1.0 / width) + NORM_EPS)


def _mm_kernel(*refs, has_scale, has_rowss, has_res, has_gain, b_is_nk):
    refs = list(refs)
    a_ref, b_ref = refs[0], refs[1]
    pos = 2
    b = b_ref[...]
    if has_scale:
        b = b * refs[pos][...]
        pos += 1
    contract = (((1,), (1 if b_is_nk else 0,)), ((), ()))
    acc = lax.dot_general(a_ref[...], b.astype(BF16), contract, preferred_element_type=F32)
    if has_rowss:
        acc = acc * _row_rsqrt(refs[pos][...], a_ref.shape[1])
        pos += 1
    if has_res:
        acc = acc + refs[pos][...]
        pos += 1
    if has_gain:
        gain_ref, o_ref, og_ref, ss_ref = refs[pos:pos + 4]
        og_ref[...] = (acc * gain_ref[...]).astype(og_ref.dtype)
        sq = acc * acc
        part = sq[:, 0:V7X_LANES]
        for c in range(1, acc.shape[1] // V7X_LANES):
            part = part + sq[:, c * V7X_LANES:(c + 1) * V7X_LANES]

        @pl.when(pl.program_id(1) == 0)
        def _():
            ss_ref[...] = jnp.zeros_like(ss_ref)
        ss_ref[...] += part
    else:
        o_ref = refs[pos]
    o_ref[...] = acc.astype(o_ref.dtype)


def _matmul(a, b, out_dtype, residual=None, n=None, layer=None, col_scale=None, b_is_nk=False,
            row_ss=None, next_gain=None, rows=MM_ROWS, cols=MM_COLS):
    m, k = a.shape
    n_axis, k_axis = (-2, -1) if b_is_nk else (-1, -2)
    n = b.shape[n_axis] if n is None else n
    tm, tn = min(rows, m), min(cols, n)
    assert m % tm == 0 and n % tn == 0 and n <= b.shape[n_axis] and b.shape[k_axis] == k
    assert not (b_is_nk and col_scale is not None) and tn % V7X_LANES == 0
    osz = jnp.dtype(out_dtype).itemsize
    blk = 2 * (tm * k * 2 + k * tn * b.dtype.itemsize + tm * tn * osz) + tm * tn * 4 + k * tn * 6
    b_block, b_index = ((tn, k), lambda j: (j, 0)) if b_is_nk else ((k, tn), lambda j: (0, j))
    if layer is None:
        b_spec = pl.BlockSpec(b_block, lambda i, j: b_index(j))
    else:
        b_spec = pl.BlockSpec((None,) + b_block, lambda i, j: (layer,) + b_index(j))
    tile = pl.BlockSpec((tm, tn), lambda i, j: (i, j))
    row_stat = pl.BlockSpec((tm, V7X_LANES), lambda i, j: (i, 0))
    in_specs = [pl.BlockSpec((tm, k), lambda i, j: (i, 0)), b_spec]
    args = [a, b]
    if col_scale is not None:
        in_specs.append(pl.BlockSpec((1, tn), lambda i, j: (0, j)))
        args.append(col_scale)
    if row_ss is not None:
        in_specs.append(row_stat)
        args.append(row_ss)
    if residual is not None:
        in_specs.append(tile)
        args.append(residual)
        blk += 2 * tm * tn * 4
    out_shape = jax.ShapeDtypeStruct((m, n), out_dtype)
    out_specs = tile
    if next_gain is not None:
        in_specs.append(pl.BlockSpec((1, tn), lambda i, j: (0, j)))
        args.append(next_gain.reshape(1, n))
        out_shape = (out_shape, jax.ShapeDtypeStruct((m, n), BF16), jax.ShapeDtypeStruct((m, V7X_LANES), F32))
        out_specs = (tile, tile, row_stat)
        blk += 2 * tm * tn * 2 + 4 * tm * tn * 4
    return pl.pallas_call(
        functools.partial(_mm_kernel, has_scale=col_scale is not None, has_rowss=row_ss is not None,
                          has_res=residual is not None, has_gain=next_gain is not None, b_is_nk=b_is_nk),
        out_shape=out_shape,
        grid=(m // tm, n // tn),
        in_specs=in_specs,
        out_specs=out_specs,
        compiler_params=_params(("arbitrary", "arbitrary"), blk),
        name="matmul_res" if residual is not None else "matmul",
    )(*args)


def _mlstm_kernel(q_ref, k_ref, v_ref, o_ref, g_ref, gain_ref, out_ref, c_ref, n_ref, m_ref, *, dk, dv):
    L = q_ref.shape[0]
    heads = ML_HEADS

    @pl.when(pl.program_id(1) == 0)
    def _():
        c_ref[...] = jnp.zeros_like(c_ref)
        n_ref[...] = jnp.zeros_like(n_ref)
        m_ref[...] = jnp.zeros_like(m_ref)

    rows = lax.broadcasted_iota(jnp.int32, (L, L), 0)
    cols = lax.broadcasted_iota(jnp.int32, (L, L), 1)
    causal = cols <= rows
    tril = causal.astype(F32)

    g = g_ref[...]
    gcap = GATE_SOFTCAP * jnp.tanh(g / GATE_SOFTCAP)
    log_f = jnp.minimum(gcap, 0.0) - jnp.log(1.0 + jnp.exp(-jnp.abs(gcap)))
    bcum = jnp.dot(tril, log_f, precision=lax.Precision.HIGHEST, preferred_element_type=F32)
    lane = lax.broadcasted_iota(jnp.int32, g.shape, 1)
    gates = jnp.where(lane < heads, gcap, bcum)
    gates_t = gates.T

    scale = dk ** -0.5
    nt = (((1,), (1,)), ((), ()))
    tn = (((0,), (0,)), ((), ()))
    for hd in range(heads):
        q = q_ref[:, hd * dk:(hd + 1) * dk]
        k = k_ref[:, hd * dk:(hd + 1) * dk]
        v = v_ref[:, hd * dv:(hd + 1) * dv]
        i_col = gates[:, hd:hd + 1]
        b_col = gates[:, heads + hd:heads + hd + 1]
        i_row = gates_t[hd:hd + 1, :]
        b_row = gates_t[heads + hd:heads + hd + 1, :]
        m_prev = m_ref[hd, :, 0:1]
        c_prev = c_ref[hd]
        n_prev = n_ref[hd]

        dmat = jnp.where(causal, b_col - b_row + i_row, -jnp.inf)
        inter = b_col + m_prev
        m_row = jnp.maximum(jnp.max(dmat, axis=-1, keepdims=True), inter)
        a_inter = jnp.exp(inter - m_row)
        s = lax.dot_general(q, k, nt, preferred_element_type=F32) * scale * jnp.exp(dmat - m_row)
        num = (a_inter * jnp.dot(q, c_prev.astype(BF16), preferred_element_type=F32)
               + jnp.dot(s.astype(BF16), v, preferred_element_type=F32))
        den = (a_inter * jnp.sum(q.astype(F32) * n_prev, axis=-1, keepdims=True)
               + jnp.sum(s, axis=-1, keepdims=True))
        h = num / jnp.maximum(jnp.abs(den), jnp.exp(-m_row))

        b_last = b_col[L - 1:L, :]
        dec = b_last - b_col + i_col
        m_new = jnp.maximum(b_last + m_prev, jnp.max(dec, axis=0, keepdims=True))
        a_old = jnp.exp(b_last + m_prev - m_new)
        kw = k.astype(F32) * (jnp.exp(dec - m_new) * scale)
        c_ref[hd] = a_old * c_prev + lax.dot_general(kw.astype(BF16), v, tn, preferred_element_type=F32)
        n_ref[hd] = a_old * n_prev + jnp.sum(kw, axis=0, keepdims=True)
        m_ref[hd] = jnp.broadcast_to(m_new, m_ref.shape[1:])

        h = h * lax.rsqrt(jnp.mean(h * h, axis=-1, keepdims=True) + NORM_EPS)
        h = h * gain_ref[:, hd * dv:(hd + 1) * dv]
        h = h * _sigmoid(o_ref[:, hd * dv:(hd + 1) * dv].astype(F32))
        out_ref[:, hd * dv:(hd + 1) * dv] = h.astype(out_ref.dtype)


def _mlstm_scan(proj, gates, head_gain, bsz, seq):
    m, width = proj.shape
    d = head_gain.shape[0]
    dv = d // ML_HEADS
    dk = dv // 2
    hk = ML_HEADS * dk
    assert width == 2 * hk + 2 * d and hk * 2 == d
    L = min(ML_CHUNK, seq)
    assert seq % L == 0
    nc = seq // L
    row = lambda b, c: b * nc + c
    blk = 2 * (L * (2 * hk + 2 * d) * 2 + L * V7X_LANES * 4 + d * 4 + L * d * 2) \
        + ML_HEADS * dk * dv * 4 + 16 * L * L * 4 + 8 * L * dv * 4
    return pl.pallas_call(
        functools.partial(_mlstm_kernel, dk=dk, dv=dv),
        out_shape=jax.ShapeDtypeStruct((m, d), BF16),
        grid=(bsz, nc),
        in_specs=[pl.BlockSpec((L, hk), lambda b, c: (row(b, c), 0)),
                  pl.BlockSpec((L, hk), lambda b, c: (row(b, c), 1)),
                  pl.BlockSpec((L, d), lambda b, c: (row(b, c), 1)),
                  pl.BlockSpec((L, d), lambda b, c: (row(b, c), 2)),
                  pl.BlockSpec((L, V7X_LANES), lambda b, c: (row(b, c), 0)),
                  pl.BlockSpec((1, d), lambda b, c: (0, 0))],
        out_specs=pl.BlockSpec((L, d), lambda b, c: (row(b, c), 0)),
        scratch_shapes=[pltpu.VMEM((ML_HEADS, dk, dv), F32),
                        pltpu.VMEM((ML_HEADS, 1, dk), F32),
                        pltpu.VMEM((ML_HEADS, 1, V7X_LANES), F32)],
        compiler_params=_params(("arbitrary", "arbitrary"), blk),
        name="mlstm_scan",
    )(proj, proj, proj, proj, gates, head_gain.reshape(1, d))


def _moba_kernel(q_ref, k_ref, v_ref, o_ref, kx_ref, vt_ref, km_ref, s_ref, acc_ref, *,
                 nb, bs, dh, hps, grp, qps):
    step = pl.program_id(2)
    dh_ext = vt_ref.shape[2]
    nbp = vt_ref.shape[1]
    nt = (((1,), (1,)), ((), ()))
    cw = grp * bs

    @pl.when(step == 0)
    def _():
        lane = lax.broadcasted_iota(jnp.int32, (bs, dh), 1)

        def prep(n, carry):
            r0 = pl.multiple_of(n * bs, bs)
            onehot = (lane == n).astype(BF16)
            for h in range(hps):
                vb = v_ref[pl.ds(r0, bs), h * dh:(h + 1) * dh].astype(F32)
                vt_ref[h, n, 0:dh, :] = vb.T.astype(BF16)
                vt_ref[h, n, dh:dh_ext, :] = jnp.ones((dh_ext - dh, bs), BF16)
                kb = k_ref[pl.ds(r0, bs), h * dh:(h + 1) * dh]
                kx_ref[h, pl.ds(r0, bs), 0:dh] = kb
                kx_ref[h, pl.ds(r0, bs), dh:2 * dh] = onehot
                km_ref[h, pl.ds(n, 1), :] = jnp.mean(kb.astype(F32), axis=0, keepdims=True)
            return carry
        lax.fori_loop(0, nb, prep, 0)
        masked = (lane == nb).astype(BF16)
        for h in range(hps):
            for n in range(nb, nbp):
                vt_ref[h, n] = jnp.zeros((dh_ext, bs), BF16)
                kx_ref[h, n * bs:(n + 1) * bs, 0:dh] = jnp.zeros((bs, dh), BF16)
                kx_ref[h, n * bs:(n + 1) * bs, dh:2 * dh] = masked

    blk_idx = lax.broadcasted_iota(jnp.int32, (nb, bs), 0)
    row_idx = lax.broadcasted_iota(jnp.int32, (dh, bs), 0)
    kidx = lax.broadcasted_iota(jnp.int32, (bs, bs), 0)
    qidx = lax.broadcasted_iota(jnp.int32, (bs, bs), 1)

    streams = [(h, r) for h in range(hps) for r in range(qps)]
    qxs, m0s = [], []
    for t, (h, r) in enumerate(streams):
        qb = step * qps + r
        q = q_ref[r * bs:(r + 1) * bs, h * dh:(h + 1) * dh]

        km = km_ref[h]
        km_hi = km.astype(BF16)
        r1 = km - km_hi.astype(F32)
        km_mid = r1.astype(BF16)
        km_lo = (r1 - km_mid.astype(F32)).astype(BF16)
        gate = (lax.dot_general(km_hi, q, nt, preferred_element_type=F32)
                + lax.dot_general(km_mid, q, nt, preferred_element_type=F32)
                + lax.dot_general(km_lo, q, nt, preferred_element_type=F32))

        removed = blk_idx >= qb
        sel = jnp.zeros((nb, bs), F32)
        for rank in range(MB_TOPK):
            gm = jnp.where(removed, -jnp.inf, gate)
            mx = jnp.max(gm, axis=0, keepdims=True)
            cand = jnp.logical_and(jnp.logical_not(removed), gm == mx)
            idx = jnp.min(jnp.where(cand, blk_idx, nb), axis=0, keepdims=True)
            hit = blk_idx == idx
            sel = jnp.where(jnp.logical_and(hit, qb > rank), 1.0, sel)
            removed = jnp.logical_or(removed, hit)

        sel_pad = jnp.concatenate([sel, jnp.zeros((dh - nb, bs), F32)], axis=0)
        neg = jnp.where(jnp.logical_or(sel_pad > 0.0, row_idx > nb), 0.0, -MB_MASK)
        qxs.append(jnp.concatenate([q, neg.T.astype(BF16)], axis=1))

        k_own = k_ref[pl.ds(pl.multiple_of(qb * bs, bs), bs), h * dh:(h + 1) * dh]
        s = lax.dot_general(k_own, q, nt, preferred_element_type=F32)
        s = jnp.where(kidx <= qidx, s, -jnp.inf)
        m0 = jnp.max(s, axis=0, keepdims=True)
        p = jnp.exp2(s - m0)
        acc_ref[t] = jnp.dot(vt_ref[h, qb], p.astype(BF16), preferred_element_type=F32)
        m0s.append(m0)

    def score(c, slot):
        r0 = pl.multiple_of(c * cw, cw)
        for t, (h, r) in enumerate(streams):
            s_ref[slot, t] = lax.dot_general(kx_ref[h, pl.ds(r0, cw), :], qxs[t], nt, preferred_element_type=F32)

    def consume(c, slot, ms):
        new_ms = []
        for t, (h, r) in enumerate(streams):
            maxes, pvs = [], []
            for g in range(grp):
                sg = s_ref[slot, t, g * bs:(g + 1) * bs, :]
                mg = jnp.max(sg, axis=0, keepdims=True)
                p = jnp.exp2(sg - mg).astype(BF16)
                pvs.append(jnp.dot(vt_ref[h, c * grp + g], p, preferred_element_type=F32))
                maxes.append(mg)
            mx = ms[t]
            for mg in maxes:
                mx = jnp.maximum(mx, mg)
            acc = acc_ref[t] * jnp.exp2(ms[t] - mx)
            for mg, pv in zip(maxes, pvs):
                acc = acc + pv * jnp.exp2(mg - mx)
            acc_ref[t] = acc
            new_ms.append(mx)
        return tuple(new_ms)

    def body(u, ms):
        c = 2 * u
        score(c + 1, 1)
        ms = consume(c, 0, ms)
        score(c + 2, 0)
        return consume(c + 1, 1, ms)

    n_chunks = lax.div(step * qps + (qps - 1) + (grp - 1), grp)
    score(0, 0)
    ms = lax.fori_loop(0, lax.div(n_chunks, 2), body, tuple(m0s))

    @pl.when(lax.rem(n_chunks, 2) == 1)
    def _():
        consume(n_chunks - 1, 0, ms)

    for t, (h, r) in enumerate(streams):
        acc = acc_ref[t]
        out = acc[0:dh, :] / acc[dh:dh + 1, :]
        o_ref[r * bs:(r + 1) * bs, h * dh:(h + 1) * dh] = out.T.astype(o_ref.dtype)


def _moba_attention(qkv, bsz, seq):
    m, width = qkv.shape
    d = width // 3
    dh = d // MB_HEADS
    bs = MB_BLOCK
    hps = MB_HEADS_PER_STEP
    nb = seq // bs
    grp = MB_GROUP
    nbp = (nb + grp - 1) // grp * grp + 2 * grp
    qps = min(MB_QBLOCKS_PER_STEP, nb)
    assert seq % bs == 0 and dh == V7X_LANES and MB_HEADS % hps == 0 and nb < dh and nb % qps == 0
    dh_ext = dh + V7X_BF16_SUBLANES
    wblk = hps * dh
    nhb = MB_HEADS // hps
    ns = nb // qps
    nstream = hps * qps
    blk = (2 * (2 * qps * bs * wblk * 2 + 2 * seq * wblk * 2) + hps * nbp * bs * (2 * dh + dh_ext) * 2
           + 2 * nstream * grp * bs * bs * 4 + nstream * (4 * grp + 8) * bs * bs * 4)
    return pl.pallas_call(
        functools.partial(_moba_kernel, nb=nb, bs=bs, dh=dh, hps=hps, grp=grp, qps=qps),
        out_shape=jax.ShapeDtypeStruct((m, d), BF16),
        grid=(bsz, nhb, ns),
        in_specs=[pl.BlockSpec((qps * bs, wblk), lambda b, h, i: (b * ns + i, h)),
                  pl.BlockSpec((seq, wblk), lambda b, h, i: (b, nhb + h)),
                  pl.BlockSpec((seq, wblk), lambda b, h, i: (b, 2 * nhb + h))],
        out_specs=pl.BlockSpec((qps * bs, wblk), lambda b, h, i: (b * ns + i, h)),
        scratch_shapes=[pltpu.VMEM((hps, nbp * bs, 2 * dh), BF16),
                        pltpu.VMEM((hps, nbp, dh_ext, bs), BF16),
                        pltpu.VMEM((hps, nb, dh), F32),
                        pltpu.VMEM((2, nstream, grp * bs, bs), F32),
                        pltpu.VMEM((nstream, dh_ext, bs), F32)],
        compiler_params=_params(("arbitrary", "arbitrary", "arbitrary"), blk),
        name="moba_attention",
    )(qkv, qkv, qkv)


def _ffn_up_kernel(xprev_ref, x_ref, ssprev_ref, ss_ref, wg_ref, wu_ref, cwg_ref, cwu_ref, cbg_ref, cbu_ref, o_ref,
                   xperm_ref, stage_ref, hg_ref, hu_ref, unperm_ref, *, tiles_per_seq):
    tm, d = x_ref.shape
    rows = tm + CONV_HALO
    sub = V7X_SUBLANES
    seg = rows // sub
    tn = o_ref.shape[1]
    nslab = stage_ref.shape[0]

    @pl.when(pl.program_id(1) == 0)
    def _():
        seq_start = (pl.program_id(0) % tiles_per_seq) == 0
        r_prev = jnp.where(seq_start, 0.0, _row_rsqrt(ssprev_ref[...], d))
        r_tile = _row_rsqrt(ss_ref[...], d)
        for g in range(d // (nslab * V7X_LANES)):
            for sl in range(nslab):
                c0 = (g * nslab + sl) * V7X_LANES
                stage_ref[sl, 0:CONV_HALO, :] = xprev_ref[:, c0:c0 + V7X_LANES].astype(F32) * r_prev
                stage_ref[sl, CONV_HALO:, :] = x_ref[:, c0:c0 + V7X_LANES].astype(F32) * r_tile

            def gather(i, carry):
                v = 2 * i
                for sl in range(nslab):
                    c0 = (g * nslab + sl) * V7X_LANES
                    a = stage_ref[sl, pl.ds(v, sub, stride=seg), :]
                    b = stage_ref[sl, pl.ds(v + 1, sub, stride=seg), :]
                    r0 = pl.multiple_of(v * sub, 2 * sub)
                    xperm_ref[pl.ds(r0, 2 * sub), c0:c0 + V7X_LANES] = jnp.concatenate([a, b], axis=0).astype(BF16)
                return carry
            lax.fori_loop(0, seg // 2, gather, 0)

    xp = xperm_ref[...]
    hg_ref[...] = jnp.dot(xp, wg_ref[...].astype(BF16), preferred_element_type=F32)
    hu_ref[...] = jnp.dot(xp, wu_ref[...].astype(BF16), preferred_element_type=F32)

    def conv(h_ref, cw_ref, cb_ref):
        last = pltpu.roll(h_ref[rows - sub:rows, :], 1, 0)
        last2 = pltpu.roll(h_ref[rows - 2 * sub:rows - sub, :], 1, 0)
        back1 = jnp.concatenate([last, h_ref[0:rows - sub, :]], axis=0)
        back2 = jnp.concatenate([last2, last, h_ref[0:rows - 2 * sub, :]], axis=0)
        return (cb_ref[...] + cw_ref[2:3, :] * h_ref[...] + cw_ref[1:2, :] * back1 + cw_ref[0:1, :] * back2)

    gate = conv(hg_ref, cwg_ref, cbg_ref)
    up = conv(hu_ref, cwu_ref, cbu_ref)
    act = gate * _sigmoid(gate) * up
    for v in range(seg):
        for sl in range(tn // V7X_LANES):
            unperm_ref[sl, pl.ds(v, sub, stride=seg), :] = act[v * sub:(v + 1) * sub,
                                                               sl * V7X_LANES:(sl + 1) * V7X_LANES]
    o_ref[...] = jnp.concatenate([unperm_ref[sl, CONV_HALO:, :] for sl in range(tn // V7X_LANES)],
                                 axis=1).astype(o_ref.dtype)


def _ffn_up(h, row_ss, w_up, layer, conv_w, conv_b, seq):
    m, d = h.shape
    f = w_up.shape[-1] // 2
    tm = min(FFN_ROWS, seq)
    tn = min(FFN_COLS, f)
    rows = tm + CONV_HALO
    assert seq % tm == 0 and f % tn == 0 and tm % CONV_HALO == 0 and CONV_WIDTH == 3
    assert rows % (2 * V7X_SUBLANES) == 0 and d % (FFN_STAGE_SLABS * V7X_LANES) == 0 and tn % V7X_LANES == 0
    nj = f // tn
    halo_blocks = tm // CONV_HALO
    blk = (2 * (tm * d * 2 + CONV_HALO * d * 2 + 2 * d * tn * 4 + tm * tn * 2) + 2 * d * tn * 6
           + rows * d * 2 + FFN_STAGE_SLABS * rows * V7X_LANES * 4 + 3 * rows * tn * 4 + 8 * tm * tn * 4)
    return pl.pallas_call(
        functools.partial(_ffn_up_kernel, tiles_per_seq=seq // tm),
        out_shape=jax.ShapeDtypeStruct((m, f), BF16),
        grid=(m // tm, nj),
        in_specs=[pl.BlockSpec((CONV_HALO, d), lambda i, j: (jnp.maximum(i * halo_blocks - 1, 0), 0)),
                  pl.BlockSpec((tm, d), lambda i, j: (i, 0)),
                  pl.BlockSpec((CONV_HALO, V7X_LANES), lambda i, j: (jnp.maximum(i * halo_blocks - 1, 0), 0)),
                  pl.BlockSpec((tm, V7X_LANES), lambda i, j: (i, 0)),
                  pl.BlockSpec((None, d, tn), lambda i, j: (layer, 0, j)),
                  pl.BlockSpec((None, d, tn), lambda i, j: (layer, 0, nj + j)),
                  pl.BlockSpec((CONV_WIDTH, tn), lambda i, j: (0, j)),
                  pl.BlockSpec((CONV_WIDTH, tn), lambda i, j: (0, nj + j)),
                  pl.BlockSpec((1, tn), lambda i, j: (0, j)),
                  pl.BlockSpec((1, tn), lambda i, j: (0, nj + j))],
        out_specs=pl.BlockSpec((tm, tn), lambda i, j: (i, j)),
        scratch_shapes=[pltpu.VMEM((rows, d), BF16),
                        pltpu.VMEM((FFN_STAGE_SLABS, rows, V7X_LANES), F32),
                        pltpu.VMEM((rows, tn), F32),
                        pltpu.VMEM((rows, tn), F32),
                        pltpu.VMEM((tn // V7X_LANES, rows, V7X_LANES), F32)],
        compiler_params=_params(("arbitrary", "arbitrary"), blk),
        name="ffn_up_conv_gate",
    )(h, h, row_ss, row_ss, w_up, w_up, conv_w, conv_w, conv_b.reshape(1, 2 * f), conv_b.reshape(1, 2 * f))


def _conv_ffn(x, xg, row_ss, w_up, layer, conv_w, conv_b, w_down, seq, next_gain=None):
    act = _ffn_up(xg, row_ss, w_up, layer, conv_w, conv_b, seq)
    return _matmul(act, w_down, F32, residual=x, layer=layer, next_gain=next_gain, rows=DOWN_ROWS, cols=DOWN_COLS)


def kernel(x, norm_mix, norm_ffn, a_w_in, a_gate_bias, a_head_norm, a_w_out, b_w_qkv, b_w_out,
           ffn_w_up, ffn_conv_w, ffn_conv_b, ffn_w_down, final_norm):
    bsz, seq, d = x.shape
    m = bsz * seq
    x = x.reshape(m, d)

    w_down = ffn_w_down.astype(BF16)

    w_in_t = jnp.swapaxes(a_w_in, 1, 2)
    n_main = w_in_t.shape[1] - 2 * ML_HEADS
    w_gates = jnp.pad(w_in_t[0, n_main:, :].T, ((0, 0), (0, V7X_LANES - 2 * ML_HEADS)))
    gate_bias = jnp.pad(a_gate_bias[0], (0, V7X_LANES - 2 * ML_HEADS)).reshape(1, V7X_LANES)
    h, gates = _rmsnorm_gates(x, norm_mix[0], w_gates, gate_bias)
    proj = _matmul(h, w_in_t, BF16, n=n_main, layer=0, b_is_nk=True)
    mixed = _mlstm_scan(proj, gates, a_head_norm[0], bsz, seq)
    x, xg, row_ss = _matmul(mixed, a_w_out, F32, residual=x, layer=0, next_gain=norm_ffn[0])
    x, xg, row_ss = _conv_ffn(x, xg, row_ss, ffn_w_up, 0, ffn_conv_w[0], ffn_conv_b[0], w_down, seq,
                              next_gain=norm_mix[1])

    q_scale = jnp.where(jnp.arange(3 * d) < d, (d // MB_HEADS) ** -0.5 * LOG2_E, 1.0).astype(F32)
    qkv = _matmul(xg, b_w_qkv, BF16, layer=0, col_scale=q_scale.reshape(1, 3 * d), row_ss=row_ss)
    attn = _moba_attention(qkv, bsz, seq)
    x, xg, row_ss = _matmul(attn, b_w_out, F32, residual=x, layer=0, next_gain=norm_ffn[1])
    x = _conv_ffn(x, xg, row_ss, ffn_w_up, 1, ffn_conv_w[1], ffn_conv_b[1], w_down, seq)

    return _rmsnorm(x, final_norm, F32).reshape(bsz, seq, d)
```

```python
import functools
import math

import jax
import jax.numpy as jnp
from jax import lax
from jax.experimental import pallas as pl
from jax.experimental.pallas import tpu as pltpu

F32 = jnp.float32
BF16 = jnp.bfloat16

NORM_EPS = 1e-6
ML_HEADS = 8
GATE_SOFTCAP = 15.0
MB_HEADS = 32
MB_BLOCK = 256
MB_TOPK = 3
CONV_WIDTH = 3

V7X_LANES = 128
V7X_SUBLANES = 8
V7X_BF16_SUBLANES = 16
V7X_VMEM_BYTES = 64 * 1024 * 1024
VMEM_RESERVE_BYTES = 6 * 1024 * 1024

ML_CHUNK = 256
NORM_ROWS = 256
MM_ROWS = 1024
MM_COLS = 512
DOWN_ROWS = 512
DOWN_COLS = 512
WEIGHT_RING_SLOTS = 3
FFN_ROWS = 1024
FFN_STAGE_SLABS = 4
FFN_COLS = 256
CONV_HALO = V7X_BF16_SUBLANES
MB_GROUP = 2
MB_HEADS_PER_STEP = 2
MB_QBLOCKS_PER_STEP = 2

LOG2_E = math.log2(math.e)
MB_MASK = 2.0 ** 100


def _vmem_limit(block_bytes):
    want = int(block_bytes) + VMEM_RESERVE_BYTES
    return max(min(want, V7X_VMEM_BYTES - VMEM_RESERVE_BYTES), 16 * 1024 * 1024)


def _params(sem, block_bytes):
    return pltpu.CompilerParams(dimension_semantics=sem, vmem_limit_bytes=_vmem_limit(block_bytes))


def _sigmoid(x):
    return 1.0 / (1.0 + jnp.exp(-x))


def _rmsnorm_kernel(x_ref, g_ref, o_ref):
    x = x_ref[...]
    ms = jnp.mean(x * x, axis=-1, keepdims=True)
    o_ref[...] = (x * lax.rsqrt(ms + NORM_EPS) * g_ref[...]).astype(o_ref.dtype)


def _rmsnorm(x, g, out_dtype):
    m, d = x.shape
    tm = min(NORM_ROWS, m)
    assert m % tm == 0
    blk = 2 * tm * d * (4 + jnp.dtype(out_dtype).itemsize) + 3 * tm * d * 4
    return pl.pallas_call(
        _rmsnorm_kernel,
        out_shape=jax.ShapeDtypeStruct((m, d), out_dtype),
        grid=(m // tm,),
        in_specs=[pl.BlockSpec((tm, d), lambda i: (i, 0)), pl.BlockSpec((1, d), lambda i: (0, 0))],
        out_specs=pl.BlockSpec((tm, d), lambda i: (i, 0)),
        compiler_params=_params(("arbitrary",), blk),
        name="rmsnorm",
    )(x, g.reshape(1, d))


def _rmsnorm_gates_kernel(x_ref, g_ref, wg_ref, b_ref, o_ref, gate_ref):
    x = x_ref[...]
    ms = jnp.mean(x * x, axis=-1, keepdims=True)
    y = x * lax.rsqrt(ms + NORM_EPS) * g_ref[...]
    o_ref[...] = y.astype(o_ref.dtype)
    y_hi = y.astype(BF16)
    y_lo = (y - y_hi.astype(F32)).astype(BF16)
    w = wg_ref[...]
    w_hi = w.astype(BF16)
    w_lo = (w - w_hi.astype(F32)).astype(BF16)
    gate_ref[...] = (jnp.dot(y_hi, w_hi, preferred_element_type=F32) + jnp.dot(y_hi, w_lo, preferred_element_type=F32)
                     + jnp.dot(y_lo, w_hi, preferred_element_type=F32) + b_ref[...])


def _rmsnorm_gates(x, g, w_gates, bias):
    m, d = x.shape
    tm = min(NORM_ROWS, m)
    assert m % tm == 0 and w_gates.shape == (d, V7X_LANES)
    blk = 2 * tm * d * 6 + 2 * d * V7X_LANES * 4 + 4 * tm * d * 4
    return pl.pallas_call(
        _rmsnorm_gates_kernel,
        out_shape=(jax.ShapeDtypeStruct((m, d), BF16), jax.ShapeDtypeStruct((m, V7X_LANES), F32)),
        grid=(m // tm,),
        in_specs=[pl.BlockSpec((tm, d), lambda i: (i, 0)), pl.BlockSpec((1, d), lambda i: (0, 0)),
                  pl.BlockSpec((d, V7X_LANES), lambda i: (0, 0)), pl.BlockSpec((1, V7X_LANES), lambda i: (0, 0))],
        out_specs=(pl.BlockSpec((tm, d), lambda i: (i, 0)), pl.BlockSpec((tm, V7X_LANES), lambda i: (i, 0))),
        compiler_params=_params(("arbitrary",), blk),
        name="rmsnorm_gates",
    )(x, g.reshape(1, d), w_gates, bias)


def _row_rsqrt(ss, width):
    return lax.rsqrt(jnp.sum(ss, axis=-1, keepdims=True) * (1.0 / width) + NORM_EPS)


def _ring_tile(b_hbm, wbuf, sem, layer, tn, b_is_nk, first_tile=0):
    nj = pl.num_programs(1)
    total = pl.num_programs(0) * nj
    s = pl.program_id(0) * nj + pl.program_id(1)
    ahead = WEIGHT_RING_SLOTS - 1

    def copy(t):
        col = pl.multiple_of((lax.rem(t, nj) + first_tile) * tn, tn)
        src = b_hbm if layer is None else b_hbm.at[layer]
        src = src.at[pl.ds(col, tn), :] if b_is_nk else src.at[:, pl.ds(col, tn)]
        slot = lax.rem(t, WEIGHT_RING_SLOTS)
        return pltpu.make_async_copy(src, wbuf.at[slot], sem.at[slot])

    @pl.when(s == 0)
    def _():
        for t in range(ahead):
            @pl.when(t < total)
            def _():
                copy(t).start()

    @pl.when(s + ahead < total)
    def _():
        copy(s + ahead).start()

    copy(s).wait()
    return wbuf[lax.rem(s, WEIGHT_RING_SLOTS)]


def _mm_kernel(*refs, has_scale, has_rowss, has_res, has_gain, b_is_nk, ring_layer, ring):
    refs = list(refs)
    a_ref, b_ref = refs[0], refs[1]
    pos = 2
    if ring:
        b = _ring_tile(b_ref, refs[-2], refs[-1], ring_layer, refs[-2].shape[2 if not b_is_nk else 1], b_is_nk)
    else:
        b = b_ref[...]
    if has_scale:
        b = b * refs[pos][...]
        pos += 1
    contract = (((1,), (1 if b_is_nk else 0,)), ((), ()))
    acc = lax.dot_general(a_ref[...], b.astype(BF16), contract, preferred_element_type=F32)
    if has_rowss:
        acc = acc * _row_rsqrt(refs[pos][...], a_ref.shape[1])
        pos += 1
    if has_res:
        acc = acc + refs[pos][...]
        pos += 1
    if has_gain:
        gain_ref, o_ref, og_ref, ss_ref = refs[pos:pos + 4]
        og_ref[...] = (acc * gain_ref[...]).astype(og_ref.dtype)
        sq = acc * acc
        part = sq[:, 0:V7X_LANES]
        for c in range(1, acc.shape[1] // V7X_LANES):
            part = part + sq[:, c * V7X_LANES:(c + 1) * V7X_LANES]

        @pl.when(pl.program_id(1) == 0)
        def _():
            ss_ref[...] = jnp.zeros_like(ss_ref)
        ss_ref[...] += part
    else:
        o_ref = refs[pos]
    o_ref[...] = acc.astype(o_ref.dtype)


def _matmul(a, b, out_dtype, residual=None, n=None, layer=None, col_scale=None, b_is_nk=False,
            row_ss=None, next_gain=None, ring=False, rows=MM_ROWS, cols=MM_COLS):
    m, k = a.shape
    n_axis, k_axis = (-2, -1) if b_is_nk else (-1, -2)
    n = b.shape[n_axis] if n is None else n
    tm, tn = min(rows, m), min(cols, n)
    assert m % tm == 0 and n % tn == 0 and n <= b.shape[n_axis] and b.shape[k_axis] == k
    assert not (b_is_nk and col_scale is not None) and tn % V7X_LANES == 0
    osz = jnp.dtype(out_dtype).itemsize
    blk = 2 * (tm * k * 2 + k * tn * b.dtype.itemsize + tm * tn * osz) + tm * tn * 4 + k * tn * 6
    b_block, b_index = ((tn, k), lambda j: (j, 0)) if b_is_nk else ((k, tn), lambda j: (0, j))
    scratch = []
    if ring:
        b_spec = pl.BlockSpec(memory_space=pl.ANY)
        scratch = [pltpu.VMEM((WEIGHT_RING_SLOTS,) + b_block, b.dtype), pltpu.SemaphoreType.DMA((WEIGHT_RING_SLOTS,))]
        blk += (WEIGHT_RING_SLOTS - 2) * k * tn * b.dtype.itemsize
    elif layer is None:
        b_spec = pl.BlockSpec(b_block, lambda i, j: b_index(j))
    else:
        b_spec = pl.BlockSpec((None,) + b_block, lambda i, j: (layer,) + b_index(j))
    tile = pl.BlockSpec((tm, tn), lambda i, j: (i, j))
    row_stat = pl.BlockSpec((tm, V7X_LANES), lambda i, j: (i, 0))
    in_specs = [pl.BlockSpec((tm, k), lambda i, j: (i, 0)), b_spec]
    args = [a, b]
    if col_scale is not None:
        in_specs.append(pl.BlockSpec((1, tn), lambda i, j: (0, j)))
        args.append(col_scale)
    if row_ss is not None:
        in_specs.append(row_stat)
        args.append(row_ss)
    if residual is not None:
        in_specs.append(tile)
        args.append(residual)
        blk += 2 * tm * tn * 4
    out_shape = jax.ShapeDtypeStruct((m, n), out_dtype)
    out_specs = tile
    if next_gain is not None:
        in_specs.append(pl.BlockSpec((1, tn), lambda i, j: (0, j)))
        args.append(next_gain.reshape(1, n))
        out_shape = (out_shape, jax.ShapeDtypeStruct((m, n), BF16), jax.ShapeDtypeStruct((m, V7X_LANES), F32))
        out_specs = (tile, tile, row_stat)
        blk += 2 * tm * tn * 2 + 4 * tm * tn * 4
    return pl.pallas_call(
        functools.partial(_mm_kernel, has_scale=col_scale is not None, has_rowss=row_ss is not None,
                          has_res=residual is not None, has_gain=next_gain is not None, b_is_nk=b_is_nk,
                          ring_layer=layer, ring=ring),
        out_shape=out_shape,
        grid=(m // tm, n // tn),
        in_specs=in_specs,
        out_specs=out_specs,
        scratch_shapes=scratch,
        compiler_params=_params(("arbitrary", "arbitrary"), blk),
        name="matmul_res" if residual is not None else "matmul",
    )(*args)


def _mlstm_kernel(q_ref, k_ref, v_ref, o_ref, g_ref, gain_ref, out_ref, c_ref, n_ref, m_ref, *, dk, dv):
    L = q_ref.shape[0]
    heads = ML_HEADS

    @pl.when(pl.program_id(1) == 0)
    def _():
        c_ref[...] = jnp.zeros_like(c_ref)
        n_ref[...] = jnp.zeros_like(n_ref)
        m_ref[...] = jnp.zeros_like(m_ref)

    rows = lax.broadcasted_iota(jnp.int32, (L, L), 0)
    cols = lax.broadcasted_iota(jnp.int32, (L, L), 1)
    causal = cols <= rows
    tril = causal.astype(F32)

    g = g_ref[...]
    gcap = GATE_SOFTCAP * jnp.tanh(g / GATE_SOFTCAP)
    log_f = jnp.minimum(gcap, 0.0) - jnp.log(1.0 + jnp.exp(-jnp.abs(gcap)))
    bcum = jnp.dot(tril, log_f, precision=lax.Precision.HIGHEST, preferred_element_type=F32)
    lane = lax.broadcasted_iota(jnp.int32, g.shape, 1)
    gates = jnp.where(lane < heads, gcap, bcum)
    gates_t = gates.T

    scale = dk ** -0.5
    nt = (((1,), (1,)), ((), ()))
    tn = (((0,), (0,)), ((), ()))
    for hd in range(heads):
        q = q_ref[:, hd * dk:(hd + 1) * dk]
        k = k_ref[:, hd * dk:(hd + 1) * dk]
        v = v_ref[:, hd * dv:(hd + 1) * dv]
        i_col = gates[:, hd:hd + 1]
        b_col = gates[:, heads + hd:heads + hd + 1]
        i_row = gates_t[hd:hd + 1, :]
        b_row = gates_t[heads + hd:heads + hd + 1, :]
        m_prev = m_ref[hd, :, 0:1]
        c_prev = c_ref[hd]
        n_prev = n_ref[hd]

        dmat = jnp.where(causal, b_col - b_row + i_row, -jnp.inf)
        inter = b_col + m_prev
        m_row = jnp.maximum(jnp.max(dmat, axis=-1, keepdims=True), inter)
        a_inter = jnp.exp(inter - m_row)
        s = lax.dot_general(q, k, nt, preferred_element_type=F32) * scale * jnp.exp(dmat - m_row)
        num = (a_inter * jnp.dot(q, c_prev.astype(BF16), preferred_element_type=F32)
               + jnp.dot(s.astype(BF16), v, preferred_element_type=F32))
        den = (a_inter * jnp.sum(q.astype(F32) * n_prev, axis=-1, keepdims=True)
               + jnp.sum(s, axis=-1, keepdims=True))
        h = num / jnp.maximum(jnp.abs(den), jnp.exp(-m_row))

        b_last = b_col[L - 1:L, :]
        dec = b_last - b_col + i_col
        m_new = jnp.maximum(b_last + m_prev, jnp.max(dec, axis=0, keepdims=True))
        a_old = jnp.exp(b_last + m_prev - m_new)
        kw = k.astype(F32) * (jnp.exp(dec - m_new) * scale)
        c_ref[hd] = a_old * c_prev + lax.dot_general(kw.astype(BF16), v, tn, preferred_element_type=F32)
        n_ref[hd] = a_old * n_prev + jnp.sum(kw, axis=0, keepdims=True)
        m_ref[hd] = jnp.broadcast_to(m_new, m_ref.shape[1:])

        h = h * lax.rsqrt(jnp.mean(h * h, axis=-1, keepdims=True) + NORM_EPS)
        h = h * gain_ref[:, hd * dv:(hd + 1) * dv]
        h = h * _sigmoid(o_ref[:, hd * dv:(hd + 1) * dv].astype(F32))
        out_ref[:, hd * dv:(hd + 1) * dv] = h.astype(out_ref.dtype)


def _mlstm_scan(proj, gates, head_gain, bsz, seq):
    m, width = proj.shape
    d = head_gain.shape[0]
    dv = d // ML_HEADS
    dk = dv // 2
    hk = ML_HEADS * dk
    assert width == 2 * hk + 2 * d and hk * 2 == d
    L = min(ML_CHUNK, seq)
    assert seq % L == 0
    nc = seq // L
    row = lambda b, c: b * nc + c
    blk = 2 * (L * (2 * hk + 2 * d) * 2 + L * V7X_LANES * 4 + d * 4 + L * d * 2) \
        + ML_HEADS * dk * dv * 4 + 16 * L * L * 4 + 8 * L * dv * 4
    return pl.pallas_call(
        functools.partial(_mlstm_kernel, dk=dk, dv=dv),
        out_shape=jax.ShapeDtypeStruct((m, d), BF16),
        grid=(bsz, nc),
        in_specs=[pl.BlockSpec((L, hk), lambda b, c: (row(b, c), 0)),
                  pl.BlockSpec((L, hk), lambda b, c: (row(b, c), 1)),
                  pl.BlockSpec((L, d), lambda b, c: (row(b, c), 1)),
                  pl.BlockSpec((L, d), lambda b, c: (row(b, c), 2)),
                  pl.BlockSpec((L, V7X_LANES), lambda b, c: (row(b, c), 0)),
                  pl.BlockSpec((1, d), lambda b, c: (0, 0))],
        out_specs=pl.BlockSpec((L, d), lambda b, c: (row(b, c), 0)),
        scratch_shapes=[pltpu.VMEM((ML_HEADS, dk, dv), F32),
                        pltpu.VMEM((ML_HEADS, 1, dk), F32),
                        pltpu.VMEM((ML_HEADS, 1, V7X_LANES), F32)],
        compiler_params=_params(("arbitrary", "arbitrary"), blk),
        name="mlstm_scan",
    )(proj, proj, proj, proj, gates, head_gain.reshape(1, d))


def _moba_kernel(q_ref, k_ref, v_ref, o_ref, kx_ref, vt_ref, km_ref, s_ref, acc_ref, *,
                 nb, bs, dh, hps, grp, qps):
    step = pl.program_id(2)
    dh_ext = vt_ref.shape[2]
    nbp = vt_ref.shape[1]
    nt = (((1,), (1,)), ((), ()))
    cw = grp * bs

    @pl.when(step == 0)
    def _():
        lane = lax.broadcasted_iota(jnp.int32, (bs, dh), 1)

        def prep(n, carry):
            r0 = pl.multiple_of(n * bs, bs)
            onehot = (lane == n).astype(BF16)
            for h in range(hps):
                vb = v_ref[pl.ds(r0, bs), h * dh:(h + 1) * dh].astype(F32)
                vt_ref[h, n, 0:dh, :] = vb.T.astype(BF16)
                vt_ref[h, n, dh:dh_ext, :] = jnp.ones((dh_ext - dh, bs), BF16)
                kb = k_ref[pl.ds(r0, bs), h * dh:(h + 1) * dh]
                kx_ref[h, pl.ds(r0, bs), 0:dh] = kb
                kx_ref[h, pl.ds(r0, bs), dh:2 * dh] = onehot
                km_ref[h, pl.ds(n, 1), :] = jnp.mean(kb.astype(F32), axis=0, keepdims=True)
            return carry
        lax.fori_loop(0, nb, prep, 0)
        masked = (lane == nb).astype(BF16)
        for h in range(hps):
            for n in range(nb, nbp):
                vt_ref[h, n] = jnp.zeros((dh_ext, bs), BF16)
                kx_ref[h, n * bs:(n + 1) * bs, 0:dh] = jnp.zeros((bs, dh), BF16)
                kx_ref[h, n * bs:(n + 1) * bs, dh:2 * dh] = masked

    blk_idx = lax.broadcasted_iota(jnp.int32, (nb, bs), 0)
    row_idx = lax.broadcasted_iota(jnp.int32, (dh, bs), 0)
    kidx = lax.broadcasted_iota(jnp.int32, (bs, bs), 0)
    qidx = lax.broadcasted_iota(jnp.int32, (bs, bs), 1)

    streams = [(h, r) for h in range(hps) for r in range(qps)]
    qxs, m0s = [], []
    for t, (h, r) in enumerate(streams):
        qb = step * qps + r
        q = q_ref[r * bs:(r + 1) * bs, h * dh:(h + 1) * dh]

        km = km_ref[h]
        km_hi = km.astype(BF16)
        r1 = km - km_hi.astype(F32)
        km_mid = r1.astype(BF16)
        km_lo = (r1 - km_mid.astype(F32)).astype(BF16)
        gate = (lax.dot_general(km_hi, q, nt, preferred_element_type=F32)
                + lax.dot_general(km_mid, q, nt, preferred_element_type=F32)
                + lax.dot_general(km_lo, q, nt, preferred_element_type=F32))

        removed = blk_idx >= qb
        sel = jnp.zeros((nb, bs), F32)
        for rank in range(MB_TOPK):
            gm = jnp.where(removed, -jnp.inf, gate)
            mx = jnp.max(gm, axis=0, keepdims=True)
            cand = jnp.logical_and(jnp.logical_not(removed), gm == mx)
            idx = jnp.min(jnp.where(cand, blk_idx, nb), axis=0, keepdims=True)
            hit = blk_idx == idx
            sel = jnp.where(jnp.logical_and(hit, qb > rank), 1.0, sel)
            removed = jnp.logical_or(removed, hit)

        sel_pad = jnp.concatenate([sel, jnp.zeros((dh - nb, bs), F32)], axis=0)
        neg = jnp.where(jnp.logical_or(sel_pad > 0.0, row_idx > nb), 0.0, -MB_MASK)
        qxs.append(jnp.concatenate([q, neg.T.astype(BF16)], axis=1))

        k_own = k_ref[pl.ds(pl.multiple_of(qb * bs, bs), bs), h * dh:(h + 1) * dh]
        s = lax.dot_general(k_own, q, nt, preferred_element_type=F32)
        s = jnp.where(kidx <= qidx, s, -jnp.inf)
        m0 = jnp.max(s, axis=0, keepdims=True)
        p = jnp.exp2(s - m0)
        acc_ref[t] = jnp.dot(vt_ref[h, qb], p.astype(BF16), preferred_element_type=F32)
        m0s.append(m0)

    def score(c, slot):
        r0 = pl.multiple_of(c * cw, cw)
        for t, (h, r) in enumerate(streams):
            s_ref[slot, t] = lax.dot_general(kx_ref[h, pl.ds(r0, cw), :], qxs[t], nt, preferred_element_type=F32)

    def consume(c, slot, ms):
        new_ms = []
        for t, (h, r) in enumerate(streams):
            maxes, pvs = [], []
            for g in range(grp):
                sg = s_ref[slot, t, g * bs:(g + 1) * bs, :]
                mg = jnp.max(sg, axis=0, keepdims=True)
                p = jnp.exp2(sg - mg).astype(BF16)
                pvs.append(jnp.dot(vt_ref[h, c * grp + g], p, preferred_element_type=F32))
                maxes.append(mg)
            mx = ms[t]
            for mg in maxes:
                mx = jnp.maximum(mx, mg)
            acc = acc_ref[t] * jnp.exp2(ms[t] - mx)
            for mg, pv in zip(maxes, pvs):
                acc = acc + pv * jnp.exp2(mg - mx)
            acc_ref[t] = acc
            new_ms.append(mx)
        return tuple(new_ms)

    def body(u, ms):
        c = 2 * u
        score(c + 1, 1)
        ms = consume(c, 0, ms)
        score(c + 2, 0)
        return consume(c + 1, 1, ms)

    n_chunks = lax.div(step * qps + (qps - 1) + (grp - 1), grp)
    score(0, 0)
    ms = lax.fori_loop(0, lax.div(n_chunks, 2), body, tuple(m0s))

    @pl.when(lax.rem(n_chunks, 2) == 1)
    def _():
        consume(n_chunks - 1, 0, ms)

    for t, (h, r) in enumerate(streams):
        acc = acc_ref[t]
        out = acc[0:dh, :] / acc[dh:dh + 1, :]
        o_ref[r * bs:(r + 1) * bs, h * dh:(h + 1) * dh] = out.T.astype(o_ref.dtype)


def _moba_attention(qkv, bsz, seq):
    m, width = qkv.shape
    d = width // 3
    dh = d // MB_HEADS
    bs = MB_BLOCK
    hps = MB_HEADS_PER_STEP
    nb = seq // bs
    grp = MB_GROUP
    nbp = (nb + grp - 1) // grp * grp + 2 * grp
    qps = min(MB_QBLOCKS_PER_STEP, nb)
    assert seq % bs == 0 and dh == V7X_LANES and MB_HEADS % hps == 0 and nb < dh and nb % qps == 0
    dh_ext = dh + V7X_BF16_SUBLANES
    wblk = hps * dh
    nhb = MB_HEADS // hps
    ns = nb // qps
    nstream = hps * qps
    blk = (2 * (2 * qps * bs * wblk * 2 + 2 * seq * wblk * 2) + hps * nbp * bs * (2 * dh + dh_ext) * 2
           + 2 * nstream * grp * bs * bs * 4 + nstream * (4 * grp + 8) * bs * bs * 4)
    return pl.pallas_call(
        functools.partial(_moba_kernel, nb=nb, bs=bs, dh=dh, hps=hps, grp=grp, qps=qps),
        out_shape=jax.ShapeDtypeStruct((m, d), BF16),
        grid=(bsz, nhb, ns),
        in_specs=[pl.BlockSpec((qps * bs, wblk), lambda b, h, i: (b * ns + i, h)),
                  pl.BlockSpec((seq, wblk), lambda b, h, i: (b, nhb + h)),
                  pl.BlockSpec((seq, wblk), lambda b, h, i: (b, 2 * nhb + h))],
        out_specs=pl.BlockSpec((qps * bs, wblk), lambda b, h, i: (b * ns + i, h)),
        scratch_shapes=[pltpu.VMEM((hps, nbp * bs, 2 * dh), BF16),
                        pltpu.VMEM((hps, nbp, dh_ext, bs), BF16),
                        pltpu.VMEM((hps, nb, dh), F32),
                        pltpu.VMEM((2, nstream, grp * bs, bs), F32),
                        pltpu.VMEM((nstream, dh_ext, bs), F32)],
        compiler_params=_params(("arbitrary", "arbitrary", "arbitrary"), blk),
        name="moba_attention",
    )(qkv, qkv, qkv)


def _ffn_up_kernel(xprev_ref, x_ref, ssprev_ref, ss_ref, w_hbm, cwg_ref, cwu_ref, cbg_ref, cbu_ref, o_ref,
                   xperm_ref, stage_ref, hg_ref, hu_ref, unperm_ref, wgbuf_ref, wubuf_ref, wgsem, wusem, *,
                   tiles_per_seq, layer):
    tm, d = x_ref.shape
    rows = tm + CONV_HALO
    sub = V7X_SUBLANES
    seg = rows // sub
    tn = o_ref.shape[1]
    nslab = stage_ref.shape[0]

    @pl.when(pl.program_id(1) == 0)
    def _():
        seq_start = (pl.program_id(0) % tiles_per_seq) == 0
        r_prev = jnp.where(seq_start, 0.0, _row_rsqrt(ssprev_ref[...], d))
        r_tile = _row_rsqrt(ss_ref[...], d)
        for g in range(d // (nslab * V7X_LANES)):
            for sl in range(nslab):
                c0 = (g * nslab + sl) * V7X_LANES
                stage_ref[sl, 0:CONV_HALO, :] = xprev_ref[:, c0:c0 + V7X_LANES].astype(F32) * r_prev
                stage_ref[sl, CONV_HALO:, :] = x_ref[:, c0:c0 + V7X_LANES].astype(F32) * r_tile

            def gather(i, carry):
                v = 2 * i
                for sl in range(nslab):
                    c0 = (g * nslab + sl) * V7X_LANES
                    a = stage_ref[sl, pl.ds(v, sub, stride=seg), :]
                    b = stage_ref[sl, pl.ds(v + 1, sub, stride=seg), :]
                    r0 = pl.multiple_of(v * sub, 2 * sub)
                    xperm_ref[pl.ds(r0, 2 * sub), c0:c0 + V7X_LANES] = jnp.concatenate([a, b], axis=0).astype(BF16)
                return carry
            lax.fori_loop(0, seg // 2, gather, 0)

    nj = pl.num_programs(1)
    wg = _ring_tile(w_hbm, wgbuf_ref, wgsem, layer, tn, False)
    wu = _ring_tile(w_hbm, wubuf_ref, wusem, layer, tn, False, first_tile=nj)
    xp = xperm_ref[...]
    hg_ref[...] = jnp.dot(xp, wg.astype(BF16), preferred_element_type=F32)
    hu_ref[...] = jnp.dot(xp, wu.astype(BF16), preferred_element_type=F32)

    def conv(h_ref, cw_ref, cb_ref):
        last = pltpu.roll(h_ref[rows - sub:rows, :], 1, 0)
        last2 = pltpu.roll(h_ref[rows - 2 * sub:rows - sub, :], 1, 0)
        back1 = jnp.concatenate([last, h_ref[0:rows - sub, :]], axis=0)
        back2 = jnp.concatenate([last2, last, h_ref[0:rows - 2 * sub, :]], axis=0)
        return (cb_ref[...] + cw_ref[2:3, :] * h_ref[...] + cw_ref[1:2, :] * back1 + cw_ref[0:1, :] * back2)

    gate = conv(hg_ref, cwg_ref, cbg_ref)
    up = conv(hu_ref, cwu_ref, cbu_ref)
    act = gate * _sigmoid(gate) * up
    for v in range(seg):
        for sl in range(tn // V7X_LANES):
            unperm_ref[sl, pl.ds(v, sub, stride=seg), :] = act[v * sub:(v + 1) * sub,
                                                               sl * V7X_LANES:(sl + 1) * V7X_LANES]
    o_ref[...] = jnp.concatenate([unperm_ref[sl, CONV_HALO:, :] for sl in range(tn // V7X_LANES)],
                                 axis=1).astype(o_ref.dtype)


def _ffn_up(h, row_ss, w_up, layer, conv_w, conv_b, seq):
    m, d = h.shape
    f = w_up.shape[-1] // 2
    tm = min(FFN_ROWS, seq)
    tn = min(FFN_COLS, f)
    rows = tm + CONV_HALO
    assert seq % tm == 0 and f % tn == 0 and tm % CONV_HALO == 0 and CONV_WIDTH == 3
    assert rows % (2 * V7X_SUBLANES) == 0 and d % (FFN_STAGE_SLABS * V7X_LANES) == 0 and tn % V7X_LANES == 0
    nj = f // tn
    halo_blocks = tm // CONV_HALO
    blk = (2 * (tm * d * 2 + CONV_HALO * d * 2 + tm * tn * 2) + 2 * WEIGHT_RING_SLOTS * d * tn * 4 + 2 * d * tn * 6
           + rows * d * 2 + FFN_STAGE_SLABS * rows * V7X_LANES * 4 + 3 * rows * tn * 4 + 8 * tm * tn * 4)
    return pl.pallas_call(
        functools.partial(_ffn_up_kernel, tiles_per_seq=seq // tm, layer=layer),
        out_shape=jax.ShapeDtypeStruct((m, f), BF16),
        grid=(m // tm, nj),
        in_specs=[pl.BlockSpec((CONV_HALO, d), lambda i, j: (jnp.maximum(i * halo_blocks - 1, 0), 0)),
                  pl.BlockSpec((tm, d), lambda i, j: (i, 0)),
                  pl.BlockSpec((CONV_HALO, V7X_LANES), lambda i, j: (jnp.maximum(i * halo_blocks - 1, 0), 0)),
                  pl.BlockSpec((tm, V7X_LANES), lambda i, j: (i, 0)),
                  pl.BlockSpec(memory_space=pl.ANY),
                  pl.BlockSpec((CONV_WIDTH, tn), lambda i, j: (0, j)),
                  pl.BlockSpec((CONV_WIDTH, tn), lambda i, j: (0, nj + j)),
                  pl.BlockSpec((1, tn), lambda i, j: (0, j)),
                  pl.BlockSpec((1, tn), lambda i, j: (0, nj + j))],
        out_specs=pl.BlockSpec((tm, tn), lambda i, j: (i, j)),
        scratch_shapes=[pltpu.VMEM((rows, d), BF16),
                        pltpu.VMEM((FFN_STAGE_SLABS, rows, V7X_LANES), F32),
                        pltpu.VMEM((rows, tn), F32),
                        pltpu.VMEM((rows, tn), F32),
                        pltpu.VMEM((tn // V7X_LANES, rows, V7X_LANES), F32),
                        pltpu.VMEM((WEIGHT_RING_SLOTS, d, tn), w_up.dtype),
                        pltpu.VMEM((WEIGHT_RING_SLOTS, d, tn), w_up.dtype),
                        pltpu.SemaphoreType.DMA((WEIGHT_RING_SLOTS,)),
                        pltpu.SemaphoreType.DMA((WEIGHT_RING_SLOTS,))],
        compiler_params=_params(("arbitrary", "arbitrary"), blk),
        name="ffn_up_conv_gate",
    )(h, h, row_ss, row_ss, w_up, conv_w, conv_w, conv_b.reshape(1, 2 * f), conv_b.reshape(1, 2 * f))


def _conv_ffn(x, xg, row_ss, w_up, layer, conv_w, conv_b, w_down, seq, next_gain=None):
    act = _ffn_up(xg, row_ss, w_up, layer, conv_w, conv_b, seq)
    return _matmul(act, w_down, F32, residual=x, layer=layer, next_gain=next_gain, rows=DOWN_ROWS, cols=DOWN_COLS)


def kernel(x, norm_mix, norm_ffn, a_w_in, a_gate_bias, a_head_norm, a_w_out, b_w_qkv, b_w_out,
           ffn_w_up, ffn_conv_w, ffn_conv_b, ffn_w_down, final_norm):
    bsz, seq, d = x.shape
    m = bsz * seq
    x = x.reshape(m, d)

    w_down = ffn_w_down.astype(BF16)

    w_in_t = jnp.swapaxes(a_w_in, 1, 2)
    n_main = w_in_t.shape[1] - 2 * ML_HEADS
    w_gates = jnp.pad(w_in_t[0, n_main:, :].T, ((0, 0), (0, V7X_LANES - 2 * ML_HEADS)))
    gate_bias = jnp.pad(a_gate_bias[0], (0, V7X_LANES - 2 * ML_HEADS)).reshape(1, V7X_LANES)
    h, gates = _rmsnorm_gates(x, norm_mix[0], w_gates, gate_bias)
    proj = _matmul(h, w_in_t, BF16, n=n_main, layer=0, b_is_nk=True, ring=True)
    mixed = _mlstm_scan(proj, gates, a_head_norm[0], bsz, seq)
    x, xg, row_ss = _matmul(mixed, a_w_out, F32, residual=x, layer=0, next_gain=norm_ffn[0])
    x, xg, row_ss = _conv_ffn(x, xg, row_ss, ffn_w_up, 0, ffn_conv_w[0], ffn_conv_b[0], w_down, seq,
                              next_gain=norm_mix[1])

    q_scale = jnp.where(jnp.arange(3 * d) < d, (d // MB_HEADS) ** -0.5 * LOG2_E, 1.0).astype(F32)
    qkv = _matmul(xg, b_w_qkv, BF16, layer=0, col_scale=q_scale.reshape(1, 3 * d), row_ss=row_ss, ring=True)
    attn = _moba_attention(qkv, bsz, seq)
    x, xg, row_ss = _matmul(attn, b_w_out, F32, residual=x, layer=0, next_gain=norm_ffn[1])
    x = _conv_ffn(x, xg, row_ss, ffn_w_up, 1, ffn_conv_w[1], ffn_conv_b[1], w_down, seq)

    return _rmsnorm(x, final_norm, F32).reshape(bsz, seq, d)
```

```python
import functools
import math

import jax
import jax.numpy as jnp
from jax import lax
from jax.experimental import pallas as pl
from jax.experimental.pallas import tpu as pltpu

F32 = jnp.float32
BF16 = jnp.bfloat16

NORM_EPS = 1e-6
ML_HEADS = 8
GATE_SOFTCAP = 15.0
MB_HEADS = 32
MB_BLOCK = 256
MB_TOPK = 3
CONV_WIDTH = 3

V7X_LANES = 128
V7X_SUBLANES = 8
V7X_BF16_SUBLANES = 16
V7X_VMEM_BYTES = 64 * 1024 * 1024
VMEM_RESERVE_BYTES = 6 * 1024 * 1024

ML_CHUNK = 256
NORM_ROWS = 256
MM_ROWS = 1024
MM_COLS = 512
DOWN_ROWS = 512
DOWN_COLS = 512
WEIGHT_RING_SLOTS = 3
FFN_ROWS = 1024
FFN_STAGE_SLABS = 4
FFN_COLS = 256
CONV_HALO = V7X_BF16_SUBLANES
MB_GROUP = 2
MB_HEADS_PER_STEP = 2
MB_QBLOCKS_PER_STEP = 2

LOG2_E = math.log2(math.e)
MB_MASK = 2.0 ** 100


def _vmem_limit(block_bytes):
    want = int(block_bytes) + VMEM_RESERVE_BYTES
    return max(min(want, V7X_VMEM_BYTES - VMEM_RESERVE_BYTES), 16 * 1024 * 1024)


def _params(sem, block_bytes):
    return pltpu.CompilerParams(dimension_semantics=sem, vmem_limit_bytes=_vmem_limit(block_bytes))


def _sigmoid(x):
    return 1.0 / (1.0 + jnp.exp(-x))


def _rmsnorm_kernel(x_ref, g_ref, o_ref):
    x = x_ref[...]
    ms = jnp.mean(x * x, axis=-1, keepdims=True)
    o_ref[...] = (x * lax.rsqrt(ms + NORM_EPS) * g_ref[...]).astype(o_ref.dtype)


def _rmsnorm(x, g, out_dtype):
    m, d = x.shape
    tm = min(NORM_ROWS, m)
    assert m % tm == 0
    blk = 2 * tm * d * (4 + jnp.dtype(out_dtype).itemsize) + 3 * tm * d * 4
    return pl.pallas_call(
        _rmsnorm_kernel,
        out_shape=jax.ShapeDtypeStruct((m, d), out_dtype),
        grid=(m // tm,),
        in_specs=[pl.BlockSpec((tm, d), lambda i: (i, 0)), pl.BlockSpec((1, d), lambda i: (0, 0))],
        out_specs=pl.BlockSpec((tm, d), lambda i: (i, 0)),
        compiler_params=_params(("arbitrary",), blk),
        name="rmsnorm",
    )(x, g.reshape(1, d))


def _rmsnorm_gates_kernel(x_ref, g_ref, wg_ref, b_ref, o_ref, gate_ref):
    x = x_ref[...]
    ms = jnp.mean(x * x, axis=-1, keepdims=True)
    y = x * lax.rsqrt(ms + NORM_EPS) * g_ref[...]
    o_ref[...] = y.astype(o_ref.dtype)
    y_hi = y.astype(BF16)
    y_lo = (y - y_hi.astype(F32)).astype(BF16)
    w = wg_ref[...]
    w_hi = w.astype(BF16)
    w_lo = (w - w_hi.astype(F32)).astype(BF16)
    gate_ref[...] = (jnp.dot(y_hi, w_hi, preferred_element_type=F32) + jnp.dot(y_hi, w_lo, preferred_element_type=F32)
                     + jnp.dot(y_lo, w_hi, preferred_element_type=F32) + b_ref[...])


def _rmsnorm_gates(x, g, w_gates, bias):
    m, d = x.shape
    tm = min(NORM_ROWS, m)
    assert m % tm == 0 and w_gates.shape == (d, V7X_LANES)
    blk = 2 * tm * d * 6 + 2 * d * V7X_LANES * 4 + 4 * tm * d * 4
    return pl.pallas_call(
        _rmsnorm_gates_kernel,
        out_shape=(jax.ShapeDtypeStruct((m, d), BF16), jax.ShapeDtypeStruct((m, V7X_LANES), F32)),
        grid=(m // tm,),
        in_specs=[pl.BlockSpec((tm, d), lambda i: (i, 0)), pl.BlockSpec((1, d), lambda i: (0, 0)),
                  pl.BlockSpec((d, V7X_LANES), lambda i: (0, 0)), pl.BlockSpec((1, V7X_LANES), lambda i: (0, 0))],
        out_specs=(pl.BlockSpec((tm, d), lambda i: (i, 0)), pl.BlockSpec((tm, V7X_LANES), lambda i: (i, 0))),
        compiler_params=_params(("arbitrary",), blk),
        name="rmsnorm_gates",
    )(x, g.reshape(1, d), w_gates, bias)


def _row_rsqrt(ss, width):
    return lax.rsqrt(jnp.sum(ss, axis=-1, keepdims=True) * (1.0 / width) + NORM_EPS)


def _ring_tile(b_hbm, wbuf, sem, layer, tn, b_is_nk):
    nj = pl.num_programs(1)
    total = pl.num_programs(0) * nj
    s = pl.program_id(0) * nj + pl.program_id(1)
    ahead = WEIGHT_RING_SLOTS - 1

    def copy(t):
        col = pl.multiple_of(lax.rem(t, nj) * tn, tn)
        src = b_hbm if layer is None else b_hbm.at[layer]
        src = src.at[pl.ds(col, tn), :] if b_is_nk else src.at[:, pl.ds(col, tn)]
        slot = lax.rem(t, WEIGHT_RING_SLOTS)
        return pltpu.make_async_copy(src, wbuf.at[slot], sem.at[slot])

    @pl.when(s == 0)
    def _():
        for t in range(ahead):
            @pl.when(t < total)
            def _():
                copy(t).start()

    @pl.when(s + ahead < total)
    def _():
        copy(s + ahead).start()

    copy(s).wait()
    return wbuf[lax.rem(s, WEIGHT_RING_SLOTS)]


def _mm_kernel(*refs, has_scale, has_rowss, has_res, has_gain, b_is_nk, ring_layer, ring):
    refs = list(refs)
    a_ref, b_ref = refs[0], refs[1]
    pos = 2
    if ring:
        b = _ring_tile(b_ref, refs[-2], refs[-1], ring_layer, refs[-2].shape[2 if not b_is_nk else 1], b_is_nk)
    else:
        b = b_ref[...]
    if has_scale:
        b = b * refs[pos][...]
        pos += 1
    contract = (((1,), (1 if b_is_nk else 0,)), ((), ()))
    acc = lax.dot_general(a_ref[...], b.astype(BF16), contract, preferred_element_type=F32)
    if has_rowss:
        acc = acc * _row_rsqrt(refs[pos][...], a_ref.shape[1])
        pos += 1
    if has_res:
        acc = acc + refs[pos][...]
        pos += 1
    if has_gain:
        gain_ref, o_ref, og_ref, ss_ref = refs[pos:pos + 4]
        og_ref[...] = (acc * gain_ref[...]).astype(og_ref.dtype)
        sq = acc * acc
        part = sq[:, 0:V7X_LANES]
        for c in range(1, acc.shape[1] // V7X_LANES):
            part = part + sq[:, c * V7X_LANES:(c + 1) * V7X_LANES]

        @pl.when(pl.program_id(1) == 0)
        def _():
            ss_ref[...] = jnp.zeros_like(ss_ref)
        ss_ref[...] += part
    else:
        o_ref = refs[pos]
    o_ref[...] = acc.astype(o_ref.dtype)


def _matmul(a, b, out_dtype, residual=None, n=None, layer=None, col_scale=None, b_is_nk=False,
            row_ss=None, next_gain=None, ring=False, rows=MM_ROWS, cols=MM_COLS):
    m, k = a.shape
    n_axis, k_axis = (-2, -1) if b_is_nk else (-1, -2)
    n = b.shape[n_axis] if n is None else n
    tm, tn = min(rows, m), min(cols, n)
    assert m % tm == 0 and n % tn == 0 and n <= b.shape[n_axis] and b.shape[k_axis] == k
    assert not (b_is_nk and col_scale is not None) and tn % V7X_LANES == 0
    osz = jnp.dtype(out_dtype).itemsize
    blk = 2 * (tm * k * 2 + k * tn * b.dtype.itemsize + tm * tn * osz) + tm * tn * 4 + k * tn * 6
    b_block, b_index = ((tn, k), lambda j: (j, 0)) if b_is_nk else ((k, tn), lambda j: (0, j))
    scratch = []
    if ring:
        b_spec = pl.BlockSpec(memory_space=pl.ANY)
        scratch = [pltpu.VMEM((WEIGHT_RING_SLOTS,) + b_block, b.dtype), pltpu.SemaphoreType.DMA((WEIGHT_RING_SLOTS,))]
        blk += (WEIGHT_RING_SLOTS - 2) * k * tn * b.dtype.itemsize
    elif layer is None:
        b_spec = pl.BlockSpec(b_block, lambda i, j: b_index(j))
    else:
        b_spec = pl.BlockSpec((None,) + b_block, lambda i, j: (layer,) + b_index(j))
    tile = pl.BlockSpec((tm, tn), lambda i, j: (i, j))
    row_stat = pl.BlockSpec((tm, V7X_LANES), lambda i, j: (i, 0))
    in_specs = [pl.BlockSpec((tm, k), lambda i, j: (i, 0)), b_spec]
    args = [a, b]
    if col_scale is not None:
        in_specs.append(pl.BlockSpec((1, tn), lambda i, j: (0, j)))
        args.append(col_scale)
    if row_ss is not None:
        in_specs.append(row_stat)
        args.append(row_ss)
    if residual is not None:
        in_specs.append(tile)
        args.append(residual)
        blk += 2 * tm * tn * 4
    out_shape = jax.ShapeDtypeStruct((m, n), out_dtype)
    out_specs = tile
    if next_gain is not None:
        in_specs.append(pl.BlockSpec((1, tn), lambda i, j: (0, j)))
        args.append(next_gain.reshape(1, n))
        out_shape = (out_shape, jax.ShapeDtypeStruct((m, n), BF16), jax.ShapeDtypeStruct((m, V7X_LANES), F32))
        out_specs = (tile, tile, row_stat)
        blk += 2 * tm * tn * 2 + 4 * tm * tn * 4
    return pl.pallas_call(
        functools.partial(_mm_kernel, has_scale=col_scale is not None, has_rowss=row_ss is not None,
                          has_res=residual is not None, has_gain=next_gain is not None, b_is_nk=b_is_nk,
                          ring_layer=layer, ring=ring),
        out_shape=out_shape,
        grid=(m // tm, n // tn),
        in_specs=in_specs,
        out_specs=out_specs,
        scratch_shapes=scratch,
        compiler_params=_params(("arbitrary", "arbitrary"), blk),
        name="matmul_res" if residual is not None else "matmul",
    )(*args)


def _mlstm_kernel(q_ref, k_ref, v_ref, o_ref, g_ref, gain_ref, out_ref, c_ref, n_ref, m_ref, *, dk, dv):
    L = q_ref.shape[0]
    heads = ML_HEADS

    @pl.when(pl.program_id(1) == 0)
    def _():
        c_ref[...] = jnp.zeros_like(c_ref)
        n_ref[...] = jnp.zeros_like(n_ref)
        m_ref[...] = jnp.zeros_like(m_ref)

    rows = lax.broadcasted_iota(jnp.int32, (L, L), 0)
    cols = lax.broadcasted_iota(jnp.int32, (L, L), 1)
    causal = cols <= rows
    tril = causal.astype(F32)

    g = g_ref[...]
    gcap = GATE_SOFTCAP * jnp.tanh(g / GATE_SOFTCAP)
    log_f = jnp.minimum(gcap, 0.0) - jnp.log(1.0 + jnp.exp(-jnp.abs(gcap)))
    bcum = jnp.dot(tril, log_f, precision=lax.Precision.HIGHEST, preferred_element_type=F32)
    lane = lax.broadcasted_iota(jnp.int32, g.shape, 1)
    gates = jnp.where(lane < heads, gcap, bcum)
    gates_t = gates.T

    scale = dk ** -0.5
    nt = (((1,), (1,)), ((), ()))
    tn = (((0,), (0,)), ((), ()))
    for hd in range(heads):
        q = q_ref[:, hd * dk:(hd + 1) * dk]
        k = k_ref[:, hd * dk:(hd + 1) * dk]
        v = v_ref[:, hd * dv:(hd + 1) * dv]
        i_col = gates[:, hd:hd + 1]
        b_col = gates[:, heads + hd:heads + hd + 1]
        i_row = gates_t[hd:hd + 1, :]
        b_row = gates_t[heads + hd:heads + hd + 1, :]
        m_prev = m_ref[hd, :, 0:1]
        c_prev = c_ref[hd]
        n_prev = n_ref[hd]

        dmat = jnp.where(causal, b_col - b_row + i_row, -jnp.inf)
        inter = b_col + m_prev
        m_row = jnp.maximum(jnp.max(dmat, axis=-1, keepdims=True), inter)
        a_inter = jnp.exp(inter - m_row)
        s = lax.dot_general(q, k, nt, preferred_element_type=F32) * scale * jnp.exp(dmat - m_row)
        num = (a_inter * jnp.dot(q, c_prev.astype(BF16), preferred_element_type=F32)
               + jnp.dot(s.astype(BF16), v, preferred_element_type=F32))
        den = (a_inter * jnp.sum(q.astype(F32) * n_prev, axis=-1, keepdims=True)
               + jnp.sum(s, axis=-1, keepdims=True))
        h = num / jnp.maximum(jnp.abs(den), jnp.exp(-m_row))

        b_last = b_col[L - 1:L, :]
        dec = b_last - b_col + i_col
        m_new = jnp.maximum(b_last + m_prev, jnp.max(dec, axis=0, keepdims=True))
        a_old = jnp.exp(b_last + m_prev - m_new)
        kw = k.astype(F32) * (jnp.exp(dec - m_new) * scale)
        c_ref[hd] = a_old * c_prev + lax.dot_general(kw.astype(BF16), v, tn, preferred_element_type=F32)
        n_ref[hd] = a_old * n_prev + jnp.sum(kw, axis=0, keepdims=True)
        m_ref[hd] = jnp.broadcast_to(m_new, m_ref.shape[1:])

        h = h * lax.rsqrt(jnp.mean(h * h, axis=-1, keepdims=True) + NORM_EPS)
        h = h * gain_ref[:, hd * dv:(hd + 1) * dv]
        h = h * _sigmoid(o_ref[:, hd * dv:(hd + 1) * dv].astype(F32))
        out_ref[:, hd * dv:(hd + 1) * dv] = h.astype(out_ref.dtype)


def _mlstm_scan(proj, gates, head_gain, bsz, seq):
    m, width = proj.shape
    d = head_gain.shape[0]
    dv = d // ML_HEADS
    dk = dv // 2
    hk = ML_HEADS * dk
    assert width == 2 * hk + 2 * d and hk * 2 == d
    L = min(ML_CHUNK, seq)
    assert seq % L == 0
    nc = seq // L
    row = lambda b, c: b * nc + c
    blk = 2 * (L * (2 * hk + 2 * d) * 2 + L * V7X_LANES * 4 + d * 4 + L * d * 2) \
        + ML_HEADS * dk * dv * 4 + 16 * L * L * 4 + 8 * L * dv * 4
    return pl.pallas_call(
        functools.partial(_mlstm_kernel, dk=dk, dv=dv),
        out_shape=jax.ShapeDtypeStruct((m, d), BF16),
        grid=(bsz, nc),
        in_specs=[pl.BlockSpec((L, hk), lambda b, c: (row(b, c), 0)),
                  pl.BlockSpec((L, hk), lambda b, c: (row(b, c), 1)),
                  pl.BlockSpec((L, d), lambda b, c: (row(b, c), 1)),
                  pl.BlockSpec((L, d), lambda b, c: (row(b, c), 2)),
                  pl.BlockSpec((L, V7X_LANES), lambda b, c: (row(b, c), 0)),
                  pl.BlockSpec((1, d), lambda b, c: (0, 0))],
        out_specs=pl.BlockSpec((L, d), lambda b, c: (row(b, c), 0)),
        scratch_shapes=[pltpu.VMEM((ML_HEADS, dk, dv), F32),
                        pltpu.VMEM((ML_HEADS, 1, dk), F32),
                        pltpu.VMEM((ML_HEADS, 1, V7X_LANES), F32)],
        compiler_params=_params(("arbitrary", "arbitrary"), blk),
        name="mlstm_scan",
    )(proj, proj, proj, proj, gates, head_gain.reshape(1, d))


def _moba_kernel(q_ref, qnext_ref, k_ref, v_ref, o_ref, kx_ref, vt_ref, km_ref, qx_ref, s_ref, acc_ref, *,
                 nb, bs, dh, hps, grp, qps):
    step = pl.program_id(2)
    dh_ext = vt_ref.shape[2]
    nbp = vt_ref.shape[1]
    nt = (((1,), (1,)), ((), ()))
    cw = grp * bs

    blk_idx = lax.broadcasted_iota(jnp.int32, (nb, bs), 0)
    row_idx = lax.broadcasted_iota(jnp.int32, (dh, bs), 0)
    streams = [(h, r) for h in range(hps) for r in range(qps)]

    def select(q, qb, h):
        km = km_ref[h]
        km_hi = km.astype(BF16)
        r1 = km - km_hi.astype(F32)
        km_mid = r1.astype(BF16)
        km_lo = (r1 - km_mid.astype(F32)).astype(BF16)
        gate = (lax.dot_general(km_hi, q, nt, preferred_element_type=F32)
                + lax.dot_general(km_mid, q, nt, preferred_element_type=F32)
                + lax.dot_general(km_lo, q, nt, preferred_element_type=F32))
        removed = blk_idx >= qb
        sel = jnp.zeros((nb, bs), F32)
        for rank in range(MB_TOPK):
            gm = jnp.where(removed, -jnp.inf, gate)
            mx = jnp.max(gm, axis=0, keepdims=True)
            cand = jnp.logical_and(jnp.logical_not(removed), gm == mx)
            idx = jnp.min(jnp.where(cand, blk_idx, nb), axis=0, keepdims=True)
            hit = blk_idx == idx
            sel = jnp.where(jnp.logical_and(hit, qb > rank), 1.0, sel)
            removed = jnp.logical_or(removed, hit)
        sel_pad = jnp.concatenate([sel, jnp.zeros((dh - nb, bs), F32)], axis=0)
        neg = jnp.where(jnp.logical_or(sel_pad > 0.0, row_idx > nb), 0.0, -MB_MASK)
        return jnp.concatenate([q, neg.T.astype(BF16)], axis=1)

    @pl.when(step == 0)
    def _():
        lane = lax.broadcasted_iota(jnp.int32, (bs, dh), 1)

        def prep(n, carry):
            r0 = pl.multiple_of(n * bs, bs)
            onehot = (lane == n).astype(BF16)
            for h in range(hps):
                vb = v_ref[pl.ds(r0, bs), h * dh:(h + 1) * dh].astype(F32)
                vt_ref[h, n, 0:dh, :] = vb.T.astype(BF16)
                vt_ref[h, n, dh:dh_ext, :] = jnp.ones((dh_ext - dh, bs), BF16)
                kb = k_ref[pl.ds(r0, bs), h * dh:(h + 1) * dh]
                kx_ref[h, pl.ds(r0, bs), 0:dh] = kb
                kx_ref[h, pl.ds(r0, bs), dh:2 * dh] = onehot
                km_ref[h, pl.ds(n, 1), :] = jnp.mean(kb.astype(F32), axis=0, keepdims=True)
            return carry
        lax.fori_loop(0, nb, prep, 0)
        masked = (lane == nb).astype(BF16)
        for h in range(hps):
            for n in range(nb, nbp):
                vt_ref[h, n] = jnp.zeros((dh_ext, bs), BF16)
                kx_ref[h, n * bs:(n + 1) * bs, 0:dh] = jnp.zeros((bs, dh), BF16)
                kx_ref[h, n * bs:(n + 1) * bs, dh:2 * dh] = masked
        for t, (h, r) in enumerate(streams):
            qx_ref[t] = select(q_ref[r * bs:(r + 1) * bs, h * dh:(h + 1) * dh], r, h)

    kidx = lax.broadcasted_iota(jnp.int32, (bs, bs), 0)
    qidx = lax.broadcasted_iota(jnp.int32, (bs, bs), 1)

    qxs = [qx_ref[t] for t in range(len(streams))]
    m0s = []
    for t, (h, r) in enumerate(streams):
        qb = step * qps + r
        q = q_ref[r * bs:(r + 1) * bs, h * dh:(h + 1) * dh]

        k_own = k_ref[pl.ds(pl.multiple_of(qb * bs, bs), bs), h * dh:(h + 1) * dh]
        s = lax.dot_general(k_own, q, nt, preferred_element_type=F32)
        s = jnp.where(kidx <= qidx, s, -jnp.inf)
        m0 = jnp.max(s, axis=0, keepdims=True)
        p = jnp.exp2(s - m0)
        acc_ref[t] = jnp.dot(vt_ref[h, qb], p.astype(BF16), preferred_element_type=F32)
        m0s.append(m0)

        qx_ref[t] = select(qnext_ref[r * bs:(r + 1) * bs, h * dh:(h + 1) * dh], (step + 1) * qps + r, h)

    def score(c, slot):
        r0 = pl.multiple_of(c * cw, cw)
        for t, (h, r) in enumerate(streams):
            s_ref[slot, t] = lax.dot_general(kx_ref[h, pl.ds(r0, cw), :], qxs[t], nt, preferred_element_type=F32)

    def consume(c, slot, ms):
        new_ms = []
        for t, (h, r) in enumerate(streams):
            maxes, pvs = [], []
            for g in range(grp):
                sg = s_ref[slot, t, g * bs:(g + 1) * bs, :]
                mg = jnp.max(sg, axis=0, keepdims=True)
                p = jnp.exp2(sg - mg).astype(BF16)
                pvs.append(jnp.dot(vt_ref[h, c * grp + g], p, preferred_element_type=F32))
                maxes.append(mg)
            mx = ms[t]
            for mg in maxes:
                mx = jnp.maximum(mx, mg)
            acc = acc_ref[t] * jnp.exp2(ms[t] - mx)
            for mg, pv in zip(maxes, pvs):
                acc = acc + pv * jnp.exp2(mg - mx)
            acc_ref[t] = acc
            new_ms.append(mx)
        return tuple(new_ms)

    def body(u, ms):
        c = 2 * u
        score(c + 1, 1)
        ms = consume(c, 0, ms)
        score(c + 2, 0)
        return consume(c + 1, 1, ms)

    n_chunks = lax.div(step * qps + (qps - 1) + (grp - 1), grp)
    score(0, 0)
    ms = lax.fori_loop(0, lax.div(n_chunks, 2), body, tuple(m0s))

    @pl.when(lax.rem(n_chunks, 2) == 1)
    def _():
        consume(n_chunks - 1, 0, ms)

    for t, (h, r) in enumerate(streams):
        acc = acc_ref[t]
        out = acc[0:dh, :] / acc[dh:dh + 1, :]
        o_ref[r * bs:(r + 1) * bs, h * dh:(h + 1) * dh] = out.T.astype(o_ref.dtype)


def _moba_attention(qkv, bsz, seq):
    m, width = qkv.shape
    d = width // 3
    dh = d // MB_HEADS
    bs = MB_BLOCK
    hps = MB_HEADS_PER_STEP
    nb = seq // bs
    grp = MB_GROUP
    nbp = (nb + grp - 1) // grp * grp + 2 * grp
    qps = min(MB_QBLOCKS_PER_STEP, nb)
    assert seq % bs == 0 and dh == V7X_LANES and MB_HEADS % hps == 0 and nb < dh and nb % qps == 0
    dh_ext = dh + V7X_BF16_SUBLANES
    wblk = hps * dh
    nhb = MB_HEADS // hps
    ns = nb // qps
    nstream = hps * qps
    blk = (2 * (2 * qps * bs * wblk * 2 + 2 * seq * wblk * 2) + hps * nbp * bs * (2 * dh + dh_ext) * 2
           + 2 * nstream * grp * bs * bs * 4 + nstream * (4 * grp + 8) * bs * bs * 4)
    return pl.pallas_call(
        functools.partial(_moba_kernel, nb=nb, bs=bs, dh=dh, hps=hps, grp=grp, qps=qps),
        out_shape=jax.ShapeDtypeStruct((m, d), BF16),
        grid=(bsz, nhb, ns),
        in_specs=[pl.BlockSpec((qps * bs, wblk), lambda b, h, i: (b * ns + i, h)),
                  pl.BlockSpec((qps * bs, wblk), lambda b, h, i: (b * ns + jnp.minimum(i + 1, ns - 1), h)),
                  pl.BlockSpec((seq, wblk), lambda b, h, i: (b, nhb + h)),
                  pl.BlockSpec((seq, wblk), lambda b, h, i: (b, 2 * nhb + h))],
        out_specs=pl.BlockSpec((qps * bs, wblk), lambda b, h, i: (b * ns + i, h)),
        scratch_shapes=[pltpu.VMEM((hps, nbp * bs, 2 * dh), BF16),
                        pltpu.VMEM((hps, nbp, dh_ext, bs), BF16),
                        pltpu.VMEM((hps, nb, dh), F32),
                        pltpu.VMEM((nstream, bs, 2 * dh), BF16),
                        pltpu.VMEM((2, nstream, grp * bs, bs), F32),
                        pltpu.VMEM((nstream, dh_ext, bs), F32)],
        compiler_params=_params(("arbitrary", "arbitrary", "arbitrary"), blk),
        name="moba_attention",
    )(qkv, qkv, qkv, qkv)


def _ffn_up_kernel(xprev_ref, x_ref, ssprev_ref, ss_ref, wg_ref, wu_ref, cwg_ref, cwu_ref, cbg_ref, cbu_ref, o_ref,
                   xperm_ref, stage_ref, hg_ref, hu_ref, unperm_ref, *, tiles_per_seq):
    tm, d = x_ref.shape
    rows = tm + CONV_HALO
    sub = V7X_SUBLANES
    seg = rows // sub
    tn = o_ref.shape[1]
    nslab = stage_ref.shape[0]

    @pl.when(pl.program_id(1) == 0)
    def _():
        seq_start = (pl.program_id(0) % tiles_per_seq) == 0
        r_prev = jnp.where(seq_start, 0.0, _row_rsqrt(ssprev_ref[...], d))
        r_tile = _row_rsqrt(ss_ref[...], d)
        for g in range(d // (nslab * V7X_LANES)):
            for sl in range(nslab):
                c0 = (g * nslab + sl) * V7X_LANES
                stage_ref[sl, 0:CONV_HALO, :] = xprev_ref[:, c0:c0 + V7X_LANES].astype(F32) * r_prev
                stage_ref[sl, CONV_HALO:, :] = x_ref[:, c0:c0 + V7X_LANES].astype(F32) * r_tile

            def gather(i, carry):
                v = 2 * i
                for sl in range(nslab):
                    c0 = (g * nslab + sl) * V7X_LANES
                    a = stage_ref[sl, pl.ds(v, sub, stride=seg), :]
                    b = stage_ref[sl, pl.ds(v + 1, sub, stride=seg), :]
                    r0 = pl.multiple_of(v * sub, 2 * sub)
                    xperm_ref[pl.ds(r0, 2 * sub), c0:c0 + V7X_LANES] = jnp.concatenate([a, b], axis=0).astype(BF16)
                return carry
            lax.fori_loop(0, seg // 2, gather, 0)

    xp = xperm_ref[...]
    hg_ref[...] = jnp.dot(xp, wg_ref[...].astype(BF16), preferred_element_type=F32)
    hu_ref[...] = jnp.dot(xp, wu_ref[...].astype(BF16), preferred_element_type=F32)

    def conv(h_ref, cw_ref, cb_ref):
        last = pltpu.roll(h_ref[rows - sub:rows, :], 1, 0)
        last2 = pltpu.roll(h_ref[rows - 2 * sub:rows - sub, :], 1, 0)
        back1 = jnp.concatenate([last, h_ref[0:rows - sub, :]], axis=0)
        back2 = jnp.concatenate([last2, last, h_ref[0:rows - 2 * sub, :]], axis=0)
        return (cb_ref[...] + cw_ref[2:3, :] * h_ref[...] + cw_ref[1:2, :] * back1 + cw_ref[0:1, :] * back2)

    gate = conv(hg_ref, cwg_ref, cbg_ref)
    up = conv(hu_ref, cwu_ref, cbu_ref)
    act = gate * _sigmoid(gate) * up
    for v in range(seg):
        for sl in range(tn // V7X_LANES):
            unperm_ref[sl, pl.ds(v, sub, stride=seg), :] = act[v * sub:(v + 1) * sub,
                                                               sl * V7X_LANES:(sl + 1) * V7X_LANES]
    o_ref[...] = jnp.concatenate([unperm_ref[sl, CONV_HALO:, :] for sl in range(tn // V7X_LANES)],
                                 axis=1).astype(o_ref.dtype)


def _ffn_up(h, row_ss, w_up, layer, conv_w, conv_b, seq):
    m, d = h.shape
    f = w_up.shape[-1] // 2
    tm = min(FFN_ROWS, seq)
    tn = min(FFN_COLS, f)
    rows = tm + CONV_HALO
    assert seq % tm == 0 and f % tn == 0 and tm % CONV_HALO == 0 and CONV_WIDTH == 3
    assert rows % (2 * V7X_SUBLANES) == 0 and d % (FFN_STAGE_SLABS * V7X_LANES) == 0 and tn % V7X_LANES == 0
    nj = f // tn
    halo_blocks = tm // CONV_HALO
    blk = (2 * (tm * d * 2 + CONV_HALO * d * 2 + 2 * d * tn * 4 + tm * tn * 2) + 2 * d * tn * 6
           + rows * d * 2 + FFN_STAGE_SLABS * rows * V7X_LANES * 4 + 3 * rows * tn * 4 + 8 * tm * tn * 4)
    return pl.pallas_call(
        functools.partial(_ffn_up_kernel, tiles_per_seq=seq // tm),
        out_shape=jax.ShapeDtypeStruct((m, f), BF16),
        grid=(m // tm, nj),
        in_specs=[pl.BlockSpec((CONV_HALO, d), lambda i, j: (jnp.maximum(i * halo_blocks - 1, 0), 0)),
                  pl.BlockSpec((tm, d), lambda i, j: (i, 0)),
                  pl.BlockSpec((CONV_HALO, V7X_LANES), lambda i, j: (jnp.maximum(i * halo_blocks - 1, 0), 0)),
                  pl.BlockSpec((tm, V7X_LANES), lambda i, j: (i, 0)),
                  pl.BlockSpec((None, d, tn), lambda i, j: (layer, 0, j)),
                  pl.BlockSpec((None, d, tn), lambda i, j: (layer, 0, nj + j)),
                  pl.BlockSpec((CONV_WIDTH, tn), lambda i, j: (0, j)),
                  pl.BlockSpec((CONV_WIDTH, tn), lambda i, j: (0, nj + j)),
                  pl.BlockSpec((1, tn), lambda i, j: (0, j)),
                  pl.BlockSpec((1, tn), lambda i, j: (0, nj + j))],
        out_specs=pl.BlockSpec((tm, tn), lambda i, j: (i, j)),
        scratch_shapes=[pltpu.VMEM((rows, d), BF16),
                        pltpu.VMEM((FFN_STAGE_SLABS, rows, V7X_LANES), F32),
                        pltpu.VMEM((rows, tn), F32),
                        pltpu.VMEM((rows, tn), F32),
                        pltpu.VMEM((tn // V7X_LANES, rows, V7X_LANES), F32)],
        compiler_params=_params(("arbitrary", "arbitrary"), blk),
        name="ffn_up_conv_gate",
    )(h, h, row_ss, row_ss, w_up, w_up, conv_w, conv_w, conv_b.reshape(1, 2 * f), conv_b.reshape(1, 2 * f))


def _conv_ffn(x, xg, row_ss, w_up, layer, conv_w, conv_b, w_down, seq, next_gain=None):
    act = _ffn_up(xg, row_ss, w_up, layer, conv_w, conv_b, seq)
    return _matmul(act, w_down, F32, residual=x, layer=layer, next_gain=next_gain, rows=DOWN_ROWS, cols=DOWN_COLS)


def kernel(x, norm_mix, norm_ffn, a_w_in, a_gate_bias, a_head_norm, a_w_out, b_w_qkv, b_w_out,
           ffn_w_up, ffn_conv_w, ffn_conv_b, ffn_w_down, final_norm):
    bsz, seq, d = x.shape
    m = bsz * seq
    x = x.reshape(m, d)

    w_down = ffn_w_down.astype(BF16)

    w_in_t = jnp.swapaxes(a_w_in, 1, 2)
    n_main = w_in_t.shape[1] - 2 * ML_HEADS
    w_gates = jnp.pad(w_in_t[0, n_main:, :].T, ((0, 0), (0, V7X_LANES - 2 * ML_HEADS)))
    gate_bias = jnp.pad(a_gate_bias[0], (0, V7X_LANES - 2 * ML_HEADS)).reshape(1, V7X_LANES)
    h, gates = _rmsnorm_gates(x, norm_mix[0], w_gates, gate_bias)
    proj = _matmul(h, w_in_t, BF16, n=n_main, layer=0, b_is_nk=True, ring=True)
    mixed = _mlstm_scan(proj, gates, a_head_norm[0], bsz, seq)
    x, xg, row_ss = _matmul(mixed, a_w_out, F32, residual=x, layer=0, next_gain=norm_ffn[0])
    x, xg, row_ss = _conv_ffn(x, xg, row_ss, ffn_w_up, 0, ffn_conv_w[0], ffn_conv_b[0], w_down, seq,
                              next_gain=norm_mix[1])

    q_scale = jnp.where(jnp.arange(3 * d) < d, (d // MB_HEADS) ** -0.5 * LOG2_E, 1.0).astype(F32)
    qkv = _matmul(xg, b_w_qkv, BF16, layer=0, col_scale=q_scale.reshape(1, 3 * d), row_ss=row_ss, ring=True)
    attn = _moba_attention(qkv, bsz, seq)
    x, xg, row_ss = _matmul(attn, b_w_out, F32, residual=x, layer=0, next_gain=norm_ffn[1])
    x = _conv_ffn(x, xg, row_ss, ffn_w_up, 1, ffn_conv_w[1], ffn_conv_b[1], w_down, seq)

    return _rmsnorm(x, final_norm, F32).reshape(bsz, seq, d)
```

```python
import functools
import math

import jax
import jax.numpy as jnp
from jax import lax
from jax.experimental import pallas as pl
from jax.experimental.pallas import tpu as pltpu

F32 = jnp.float32
BF16 = jnp.bfloat16

NORM_EPS = 1e-6
ML_HEADS = 8
GATE_SOFTCAP = 15.0
MB_HEADS = 32
MB_BLOCK = 256
MB_TOPK = 3
CONV_WIDTH = 3

V7X_LANES = 128
V7X_SUBLANES = 8
V7X_BF16_SUBLANES = 16
V7X_VMEM_BYTES = 64 * 1024 * 1024
VMEM_RESERVE_BYTES = 6 * 1024 * 1024

ML_CHUNK = 256
NORM_ROWS = 512
MM_ROWS = 1024
MM_COLS = 512
DOWN_ROWS = 512
DOWN_COLS = 512
WEIGHT_RING_SLOTS = 3
FFN_ROWS = 1024
FFN_STAGE_SLABS = 4
FFN_COLS = 256
CONV_HALO = V7X_BF16_SUBLANES
MB_GROUP = 2
MB_HEADS_PER_STEP = 2
MB_QBLOCKS_PER_STEP = 2

LOG2_E = math.log2(math.e)
MB_MASK = 2.0 ** 100


def _vmem_limit(block_bytes):
    want = int(block_bytes) + VMEM_RESERVE_BYTES
    return max(min(want, V7X_VMEM_BYTES - VMEM_RESERVE_BYTES), 16 * 1024 * 1024)


def _params(sem, block_bytes):
    return pltpu.CompilerParams(dimension_semantics=sem, vmem_limit_bytes=_vmem_limit(block_bytes))


def _sigmoid(x):
    return 1.0 / (1.0 + jnp.exp(-x))


def _rmsnorm_kernel(x_ref, g_ref, o_ref):
    x = x_ref[...]
    ms = jnp.mean(x * x, axis=-1, keepdims=True)
    o_ref[...] = (x * lax.rsqrt(ms + NORM_EPS) * g_ref[...]).astype(o_ref.dtype)


def _rmsnorm(x, g, out_dtype):
    m, d = x.shape
    tm = min(NORM_ROWS, m)
    assert m % tm == 0
    blk = 2 * tm * d * (4 + jnp.dtype(out_dtype).itemsize) + 3 * tm * d * 4
    return pl.pallas_call(
        _rmsnorm_kernel,
        out_shape=jax.ShapeDtypeStruct((m, d), out_dtype),
        grid=(m // tm,),
        in_specs=[pl.BlockSpec((tm, d), lambda i: (i, 0)), pl.BlockSpec((1, d), lambda i: (0, 0))],
        out_specs=pl.BlockSpec((tm, d), lambda i: (i, 0)),
        compiler_params=_params(("arbitrary",), blk),
        name="rmsnorm",
    )(x, g.reshape(1, d))


def _rmsnorm_gates_kernel(x_ref, g_ref, wg_ref, b_ref, o_ref, gate_ref):
    x = x_ref[...]
    ms = jnp.mean(x * x, axis=-1, keepdims=True)
    y = x * lax.rsqrt(ms + NORM_EPS) * g_ref[...]
    o_ref[...] = y.astype(o_ref.dtype)
    y_hi = y.astype(BF16)
    y_lo = (y - y_hi.astype(F32)).astype(BF16)
    w = wg_ref[...]
    w_hi = w.astype(BF16)
    w_lo = (w - w_hi.astype(F32)).astype(BF16)
    gate_ref[...] = (jnp.dot(y_hi, w_hi, preferred_element_type=F32) + jnp.dot(y_hi, w_lo, preferred_element_type=F32)
                     + jnp.dot(y_lo, w_hi, preferred_element_type=F32) + b_ref[...])


def _rmsnorm_gates(x, g, w_gates, bias):
    m, d = x.shape
    tm = min(NORM_ROWS, m)
    assert m % tm == 0 and w_gates.shape == (d, V7X_LANES)
    blk = 2 * tm * d * 6 + 2 * d * V7X_LANES * 4 + 4 * tm * d * 4
    return pl.pallas_call(
        _rmsnorm_gates_kernel,
        out_shape=(jax.ShapeDtypeStruct((m, d), BF16), jax.ShapeDtypeStruct((m, V7X_LANES), F32)),
        grid=(m // tm,),
        in_specs=[pl.BlockSpec((tm, d), lambda i: (i, 0)), pl.BlockSpec((1, d), lambda i: (0, 0)),
                  pl.BlockSpec((d, V7X_LANES), lambda i: (0, 0)), pl.BlockSpec((1, V7X_LANES), lambda i: (0, 0))],
        out_specs=(pl.BlockSpec((tm, d), lambda i: (i, 0)), pl.BlockSpec((tm, V7X_LANES), lambda i: (i, 0))),
        compiler_params=_params(("arbitrary",), blk),
        name="rmsnorm_gates",
    )(x, g.reshape(1, d), w_gates, bias)


def _row_rsqrt(ss, width):
    return lax.rsqrt(jnp.sum(ss, axis=-1, keepdims=True) * (1.0 / width) + NORM_EPS)


def _ring_tile(b_hbm, wbuf, sem, layer, tn, b_is_nk):
    nj = pl.num_programs(1)
    total = pl.num_programs(0) * nj
    s = pl.program_id(0) * nj + pl.program_id(1)
    ahead = WEIGHT_RING_SLOTS - 1

    def copy(t):
        col = pl.multiple_of(lax.rem(t, nj) * tn, tn)
        src = b_hbm if layer is None else b_hbm.at[layer]
        src = src.at[pl.ds(col, tn), :] if b_is_nk else src.at[:, pl.ds(col, tn)]
        slot = lax.rem(t, WEIGHT_RING_SLOTS)
        return pltpu.make_async_copy(src, wbuf.at[slot], sem.at[slot])

    @pl.when(s == 0)
    def _():
        for t in range(ahead):
            @pl.when(t < total)
            def _():
                copy(t).start()

    @pl.when(s + ahead < total)
    def _():
        copy(s + ahead).start()

    copy(s).wait()
    return wbuf[lax.rem(s, WEIGHT_RING_SLOTS)]


def _mm_kernel(*refs, has_scale, has_rowss, has_res, has_gain, b_is_nk, ring_layer, ring):
    refs = list(refs)
    a_ref, b_ref = refs[0], refs[1]
    pos = 2
    if ring:
        b = _ring_tile(b_ref, refs[-2], refs[-1], ring_layer, refs[-2].shape[2 if not b_is_nk else 1], b_is_nk)
    else:
        b = b_ref[...]
    if has_scale:
        b = b * refs[pos][...]
        pos += 1
    contract = (((1,), (1 if b_is_nk else 0,)), ((), ()))
    acc = lax.dot_general(a_ref[...], b.astype(BF16), contract, preferred_element_type=F32)
    if has_rowss:
        acc = acc * _row_rsqrt(refs[pos][...], a_ref.shape[1])
        pos += 1
    if has_res:
        acc = acc + refs[pos][...]
        pos += 1
    if has_gain:
        gain_ref, o_ref, og_ref, ss_ref = refs[pos:pos + 4]
        og_ref[...] = (acc * gain_ref[...]).astype(og_ref.dtype)
        sq = acc * acc
        part = sq[:, 0:V7X_LANES]
        for c in range(1, acc.shape[1] // V7X_LANES):
            part = part + sq[:, c * V7X_LANES:(c + 1) * V7X_LANES]

        @pl.when(pl.program_id(1) == 0)
        def _():
            ss_ref[...] = jnp.zeros_like(ss_ref)
        ss_ref[...] += part
    else:
        o_ref = refs[pos]
    o_ref[...] = acc.astype(o_ref.dtype)


def _matmul(a, b, out_dtype, residual=None, n=None, layer=None, col_scale=None, b_is_nk=False,
            row_ss=None, next_gain=None, ring=False, rows=MM_ROWS, cols=MM_COLS):
    m, k = a.shape
    n_axis, k_axis = (-2, -1) if b_is_nk else (-1, -2)
    n = b.shape[n_axis] if n is None else n
    tm, tn = min(rows, m), min(cols, n)
    assert m % tm == 0 and n % tn == 0 and n <= b.shape[n_axis] and b.shape[k_axis] == k
    assert not (b_is_nk and col_scale is not None) and tn % V7X_LANES == 0
    osz = jnp.dtype(out_dtype).itemsize
    blk = 2 * (tm * k * 2 + k * tn * b.dtype.itemsize + tm * tn * osz) + tm * tn * 4 + k * tn * 6
    b_block, b_index = ((tn, k), lambda j: (j, 0)) if b_is_nk else ((k, tn), lambda j: (0, j))
    scratch = []
    if ring:
        b_spec = pl.BlockSpec(memory_space=pl.ANY)
        scratch = [pltpu.VMEM((WEIGHT_RING_SLOTS,) + b_block, b.dtype), pltpu.SemaphoreType.DMA((WEIGHT_RING_SLOTS,))]
        blk += (WEIGHT_RING_SLOTS - 2) * k * tn * b.dtype.itemsize
    elif layer is None:
        b_spec = pl.BlockSpec(b_block, lambda i, j: b_index(j))
    else:
        b_spec = pl.BlockSpec((None,) + b_block, lambda i, j: (layer,) + b_index(j))
    tile = pl.BlockSpec((tm, tn), lambda i, j: (i, j))
    row_stat = pl.BlockSpec((tm, V7X_LANES), lambda i, j: (i, 0))
    in_specs = [pl.BlockSpec((tm, k), lambda i, j: (i, 0)), b_spec]
    args = [a, b]
    if col_scale is not None:
        in_specs.append(pl.BlockSpec((1, tn), lambda i, j: (0, j)))
        args.append(col_scale)
    if row_ss is not None:
        in_specs.append(row_stat)
        args.append(row_ss)
    if residual is not None:
        in_specs.append(tile)
        args.append(residual)
        blk += 2 * tm * tn * 4
    out_shape = jax.ShapeDtypeStruct((m, n), out_dtype)
    out_specs = tile
    if next_gain is not None:
        in_specs.append(pl.BlockSpec((1, tn), lambda i, j: (0, j)))
        args.append(next_gain.reshape(1, n))
        out_shape = (out_shape, jax.ShapeDtypeStruct((m, n), BF16), jax.ShapeDtypeStruct((m, V7X_LANES), F32))
        out_specs = (tile, tile, row_stat)
        blk += 2 * tm * tn * 2 + 4 * tm * tn * 4
    return pl.pallas_call(
        functools.partial(_mm_kernel, has_scale=col_scale is not None, has_rowss=row_ss is not None,
                          has_res=residual is not None, has_gain=next_gain is not None, b_is_nk=b_is_nk,
                          ring_layer=layer, ring=ring),
        out_shape=out_shape,
        grid=(m // tm, n // tn),
        in_specs=in_specs,
        out_specs=out_specs,
        scratch_shapes=scratch,
        compiler_params=_params(("arbitrary", "arbitrary"), blk),
        name="matmul_res" if residual is not None else "matmul",
    )(*args)


def _mlstm_kernel(q_ref, k_ref, v_ref, o_ref, g_ref, gain_ref, out_ref, c_ref, n_ref, m_ref, *, dk, dv):
    L = q_ref.shape[0]
    heads = ML_HEADS

    @pl.when(pl.program_id(1) == 0)
    def _():
        c_ref[...] = jnp.zeros_like(c_ref)
        n_ref[...] = jnp.zeros_like(n_ref)
        m_ref[...] = jnp.zeros_like(m_ref)

    rows = lax.broadcasted_iota(jnp.int32, (L, L), 0)
    cols = lax.broadcasted_iota(jnp.int32, (L, L), 1)
    causal = cols <= rows
    tril = causal.astype(F32)

    g = g_ref[...]
    gcap = GATE_SOFTCAP * jnp.tanh(g / GATE_SOFTCAP)
    log_f = jnp.minimum(gcap, 0.0) - jnp.log(1.0 + jnp.exp(-jnp.abs(gcap)))
    bcum = jnp.dot(tril, log_f, precision=lax.Precision.HIGHEST, preferred_element_type=F32)
    lane = lax.broadcasted_iota(jnp.int32, g.shape, 1)
    gates = jnp.where(lane < heads, gcap, bcum)
    gates_t = gates.T

    scale = dk ** -0.5
    nt = (((1,), (1,)), ((), ()))
    tn = (((0,), (0,)), ((), ()))
    for hd in range(heads):
        q = q_ref[:, hd * dk:(hd + 1) * dk]
        k = k_ref[:, hd * dk:(hd + 1) * dk]
        v = v_ref[:, hd * dv:(hd + 1) * dv]
        i_col = gates[:, hd:hd + 1]
        b_col = gates[:, heads + hd:heads + hd + 1]
        i_row = gates_t[hd:hd + 1, :]
        b_row = gates_t[heads + hd:heads + hd + 1, :]
        m_prev = m_ref[hd, :, 0:1]
        c_prev = c_ref[hd]
        n_prev = n_ref[hd]

        dmat = jnp.where(causal, b_col - b_row + i_row, -jnp.inf)
        inter = b_col + m_prev
        m_row = jnp.maximum(jnp.max(dmat, axis=-1, keepdims=True), inter)
        a_inter = jnp.exp(inter - m_row)
        s = lax.dot_general(q, k, nt, preferred_element_type=F32) * scale * jnp.exp(dmat - m_row)
        num = (a_inter * jnp.dot(q, c_prev.astype(BF16), preferred_element_type=F32)
               + jnp.dot(s.astype(BF16), v, preferred_element_type=F32))
        den = (a_inter * jnp.sum(q.astype(F32) * n_prev, axis=-1, keepdims=True)
               + jnp.sum(s, axis=-1, keepdims=True))
        h = num / jnp.maximum(jnp.abs(den), jnp.exp(-m_row))

        b_last = b_col[L - 1:L, :]
        dec = b_last - b_col + i_col
        m_new = jnp.maximum(b_last + m_prev, jnp.max(dec, axis=0, keepdims=True))
        a_old = jnp.exp(b_last + m_prev - m_new)
        kw = k.astype(F32) * (jnp.exp(dec - m_new) * scale)
        c_ref[hd] = a_old * c_prev + lax.dot_general(kw.astype(BF16), v, tn, preferred_element_type=F32)
        n_ref[hd] = a_old * n_prev + jnp.sum(kw, axis=0, keepdims=True)
        m_ref[hd] = jnp.broadcast_to(m_new, m_ref.shape[1:])

        h = h * lax.rsqrt(jnp.mean(h * h, axis=-1, keepdims=True) + NORM_EPS)
        h = h * gain_ref[:, hd * dv:(hd + 1) * dv]
        h = h * _sigmoid(o_ref[:, hd * dv:(hd + 1) * dv].astype(F32))
        out_ref[:, hd * dv:(hd + 1) * dv] = h.astype(out_ref.dtype)


def _mlstm_scan(proj, gates, head_gain, bsz, seq):
    m, width = proj.shape
    d = head_gain.shape[0]
    dv = d // ML_HEADS
    dk = dv // 2
    hk = ML_HEADS * dk
    assert width == 2 * hk + 2 * d and hk * 2 == d
    L = min(ML_CHUNK, seq)
    assert seq % L == 0
    nc = seq // L
    row = lambda b, c: b * nc + c
    blk = 2 * (L * (2 * hk + 2 * d) * 2 + L * V7X_LANES * 4 + d * 4 + L * d * 2) \
        + ML_HEADS * dk * dv * 4 + 16 * L * L * 4 + 8 * L * dv * 4
    return pl.pallas_call(
        functools.partial(_mlstm_kernel, dk=dk, dv=dv),
        out_shape=jax.ShapeDtypeStruct((m, d), BF16),
        grid=(bsz, nc),
        in_specs=[pl.BlockSpec((L, hk), lambda b, c: (row(b, c), 0)),
                  pl.BlockSpec((L, hk), lambda b, c: (row(b, c), 1)),
                  pl.BlockSpec((L, d), lambda b, c: (row(b, c), 1)),
                  pl.BlockSpec((L, d), lambda b, c: (row(b, c), 2)),
                  pl.BlockSpec((L, V7X_LANES), lambda b, c: (row(b, c), 0)),
                  pl.BlockSpec((1, d), lambda b, c: (0, 0))],
        out_specs=pl.BlockSpec((L, d), lambda b, c: (row(b, c), 0)),
        scratch_shapes=[pltpu.VMEM((ML_HEADS, dk, dv), F32),
                        pltpu.VMEM((ML_HEADS, 1, dk), F32),
                        pltpu.VMEM((ML_HEADS, 1, V7X_LANES), F32)],
        compiler_params=_params(("arbitrary", "arbitrary"), blk),
        name="mlstm_scan",
    )(proj, proj, proj, proj, gates, head_gain.reshape(1, d))


def _moba_kernel(q_ref, qnext_ref, k_ref, v_ref, o_ref, kx_ref, vt_ref, km_ref, qx_ref, s_ref, acc_ref, *,
                 nb, bs, dh, hps, grp, qps):
    step = pl.program_id(2)
    dh_ext = vt_ref.shape[2]
    nbp = vt_ref.shape[1]
    nt = (((1,), (1,)), ((), ()))
    cw = grp * bs

    blk_idx = lax.broadcasted_iota(jnp.int32, (nb, bs), 0)
    row_idx = lax.broadcasted_iota(jnp.int32, (dh, bs), 0)
    streams = [(h, r) for h in range(hps) for r in range(qps)]

    def select(q, qb, h):
        km = km_ref[h]
        km_hi = km.astype(BF16)
        r1 = km - km_hi.astype(F32)
        km_mid = r1.astype(BF16)
        km_lo = (r1 - km_mid.astype(F32)).astype(BF16)
        g3 = lax.dot_general(jnp.concatenate([km_hi, km_mid, km_lo], axis=0), q, nt, preferred_element_type=F32)
        gate = g3[0:nb] + g3[nb:2 * nb] + g3[2 * nb:3 * nb]
        removed = blk_idx >= qb
        sel = jnp.zeros((nb, bs), F32)
        for rank in range(MB_TOPK):
            gm = jnp.where(removed, -jnp.inf, gate)
            mx = jnp.max(gm, axis=0, keepdims=True)
            cand = jnp.logical_and(jnp.logical_not(removed), gm == mx)
            idx = jnp.min(jnp.where(cand, blk_idx, nb), axis=0, keepdims=True)
            hit = blk_idx == idx
            sel = jnp.where(jnp.logical_and(hit, qb > rank), 1.0, sel)
            removed = jnp.logical_or(removed, hit)
        sel_pad = jnp.concatenate([sel, jnp.zeros((dh - nb, bs), F32)], axis=0)
        neg = jnp.where(jnp.logical_or(sel_pad > 0.0, row_idx > nb), 0.0, -MB_MASK)
        return jnp.concatenate([q, neg.T.astype(BF16)], axis=1)

    @pl.when(step == 0)
    def _():
        lane = lax.broadcasted_iota(jnp.int32, (bs, dh), 1)

        def prep(n, carry):
            r0 = pl.multiple_of(n * bs, bs)
            onehot = (lane == n).astype(BF16)
            for h in range(hps):
                vb = v_ref[pl.ds(r0, bs), h * dh:(h + 1) * dh].astype(F32)
                vt_ref[h, n, 0:dh, :] = vb.T.astype(BF16)
                vt_ref[h, n, dh:dh_ext, :] = jnp.ones((dh_ext - dh, bs), BF16)
                kb = k_ref[pl.ds(r0, bs), h * dh:(h + 1) * dh]
                kx_ref[h, pl.ds(r0, bs), 0:dh] = kb
                kx_ref[h, pl.ds(r0, bs), dh:2 * dh] = onehot
                km_ref[h, pl.ds(n, 1), :] = jnp.mean(kb.astype(F32), axis=0, keepdims=True)
            return carry
        lax.fori_loop(0, nb, prep, 0)
        masked = (lane == nb).astype(BF16)
        for h in range(hps):
            for n in range(nb, nbp):
                vt_ref[h, n] = jnp.zeros((dh_ext, bs), BF16)
                kx_ref[h, n * bs:(n + 1) * bs, 0:dh] = jnp.zeros((bs, dh), BF16)
                kx_ref[h, n * bs:(n + 1) * bs, dh:2 * dh] = masked
        for t, (h, r) in enumerate(streams):
            qx_ref[t] = select(q_ref[r * bs:(r + 1) * bs, h * dh:(h + 1) * dh], r, h)

    kidx = lax.broadcasted_iota(jnp.int32, (bs, bs), 0)
    qidx = lax.broadcasted_iota(jnp.int32, (bs, bs), 1)

    qxs = [qx_ref[t] for t in range(len(streams))]
    m0s = []
    for t, (h, r) in enumerate(streams):
        qb = step * qps + r
        q = q_ref[r * bs:(r + 1) * bs, h * dh:(h + 1) * dh]

        k_own = k_ref[pl.ds(pl.multiple_of(qb * bs, bs), bs), h * dh:(h + 1) * dh]
        s = lax.dot_general(k_own, q, nt, preferred_element_type=F32)
        s = jnp.where(kidx <= qidx, s, -jnp.inf)
        m0 = jnp.max(s, axis=0, keepdims=True)
        p = jnp.exp2(s - m0)
        acc_ref[t] = jnp.dot(vt_ref[h, qb], p.astype(BF16), preferred_element_type=F32)
        m0s.append(m0)

        qx_ref[t] = select(qnext_ref[r * bs:(r + 1) * bs, h * dh:(h + 1) * dh], (step + 1) * qps + r, h)

    def score(c, slot):
        r0 = pl.multiple_of(c * cw, cw)
        for t, (h, r) in enumerate(streams):
            s_ref[slot, t] = lax.dot_general(kx_ref[h, pl.ds(r0, cw), :], qxs[t], nt, preferred_element_type=F32)

    def consume(c, slot, ms):
        new_ms = []
        for t, (h, r) in enumerate(streams):
            maxes, pvs = [], []
            for g in range(grp):
                sg = s_ref[slot, t, g * bs:(g + 1) * bs, :]
                mg = jnp.max(sg, axis=0, keepdims=True)
                p = jnp.exp2(sg - mg).astype(BF16)
                pvs.append(jnp.dot(vt_ref[h, c * grp + g], p, preferred_element_type=F32))
                maxes.append(mg)
            mx = ms[t]
            for mg in maxes:
                mx = jnp.maximum(mx, mg)
            acc = acc_ref[t] * jnp.exp2(ms[t] - mx)
            for mg, pv in zip(maxes, pvs):
                acc = acc + pv * jnp.exp2(mg - mx)
            acc_ref[t] = acc
            new_ms.append(mx)
        return tuple(new_ms)

    def body(u, ms):
        c = 2 * u
        score(c + 1, 1)
        ms = consume(c, 0, ms)
        score(c + 2, 0)
        return consume(c + 1, 1, ms)

    n_chunks = lax.div(step * qps + (qps - 1) + (grp - 1), grp)
    score(0, 0)
    ms = lax.fori_loop(0, lax.div(n_chunks, 2), body, tuple(m0s))

    @pl.when(lax.rem(n_chunks, 2) == 1)
    def _():
        consume(n_chunks - 1, 0, ms)

    for t, (h, r) in enumerate(streams):
        acc = acc_ref[t]
        out = acc[0:dh, :] / acc[dh:dh + 1, :]
        o_ref[r * bs:(r + 1) * bs, h * dh:(h + 1) * dh] = out.T.astype(o_ref.dtype)


def _moba_attention(qkv, bsz, seq):
    m, width = qkv.shape
    d = width // 3
    dh = d // MB_HEADS
    bs = MB_BLOCK
    hps = MB_HEADS_PER_STEP
    nb = seq // bs
    grp = MB_GROUP
    nbp = (nb + grp - 1) // grp * grp + 2 * grp
    qps = min(MB_QBLOCKS_PER_STEP, nb)
    assert seq % bs == 0 and dh == V7X_LANES and MB_HEADS % hps == 0 and nb < dh and nb % qps == 0
    dh_ext = dh + V7X_BF16_SUBLANES
    wblk = hps * dh
    nhb = MB_HEADS // hps
    ns = nb // qps
    nstream = hps * qps
    blk = (2 * (2 * qps * bs * wblk * 2 + 2 * seq * wblk * 2) + hps * nbp * bs * (2 * dh + dh_ext) * 2
           + 2 * nstream * grp * bs * bs * 4 + nstream * (4 * grp + 8) * bs * bs * 4)
    return pl.pallas_call(
        functools.partial(_moba_kernel, nb=nb, bs=bs, dh=dh, hps=hps, grp=grp, qps=qps),
        out_shape=jax.ShapeDtypeStruct((m, d), BF16),
        grid=(bsz, nhb, ns),
        in_specs=[pl.BlockSpec((qps * bs, wblk), lambda b, h, i: (b * ns + i, h)),
                  pl.BlockSpec((qps * bs, wblk), lambda b, h, i: (b * ns + jnp.minimum(i + 1, ns - 1), h)),
                  pl.BlockSpec((seq, wblk), lambda b, h, i: (b, nhb + h)),
                  pl.BlockSpec((seq, wblk), lambda b, h, i: (b, 2 * nhb + h))],
        out_specs=pl.BlockSpec((qps * bs, wblk), lambda b, h, i: (b * ns + i, h)),
        scratch_shapes=[pltpu.VMEM((hps, nbp * bs, 2 * dh), BF16),
                        pltpu.VMEM((hps, nbp, dh_ext, bs), BF16),
                        pltpu.VMEM((hps, nb, dh), F32),
                        pltpu.VMEM((nstream, bs, 2 * dh), BF16),
                        pltpu.VMEM((2, nstream, grp * bs, bs), F32),
                        pltpu.VMEM((nstream, dh_ext, bs), F32)],
        compiler_params=_params(("arbitrary", "arbitrary", "arbitrary"), blk),
        name="moba_attention",
    )(qkv, qkv, qkv, qkv)


def _ffn_up_kernel(xprev_ref, x_ref, ssprev_ref, ss_ref, wg_ref, wu_ref, cwg_ref, cwu_ref, cbg_ref, cbu_ref, o_ref,
                   xperm_ref, stage_ref, hg_ref, hu_ref, unperm_ref, *, tiles_per_seq):
    tm, d = x_ref.shape
    rows = tm + CONV_HALO
    sub = V7X_SUBLANES
    seg = rows // sub
    tn = o_ref.shape[1]
    nslab = stage_ref.shape[0]

    @pl.when(pl.program_id(1) == 0)
    def _():
        seq_start = (pl.program_id(0) % tiles_per_seq) == 0
        r_prev = jnp.where(seq_start, 0.0, _row_rsqrt(ssprev_ref[...], d))
        r_tile = _row_rsqrt(ss_ref[...], d)
        for g in range(d // (nslab * V7X_LANES)):
            for sl in range(nslab):
                c0 = (g * nslab + sl) * V7X_LANES
                stage_ref[sl, 0:CONV_HALO, :] = xprev_ref[:, c0:c0 + V7X_LANES].astype(F32) * r_prev
                stage_ref[sl, CONV_HALO:, :] = x_ref[:, c0:c0 + V7X_LANES].astype(F32) * r_tile

            def gather(i, carry):
                v = 2 * i
                for sl in range(nslab):
                    c0 = (g * nslab + sl) * V7X_LANES
                    a = stage_ref[sl, pl.ds(v, sub, stride=seg), :]
                    b = stage_ref[sl, pl.ds(v + 1, sub, stride=seg), :]
                    r0 = pl.multiple_of(v * sub, 2 * sub)
                    xperm_ref[pl.ds(r0, 2 * sub), c0:c0 + V7X_LANES] = jnp.concatenate([a, b], axis=0).astype(BF16)
                return carry
            lax.fori_loop(0, seg // 2, gather, 0)

    xp = xperm_ref[...]
    hg_ref[...] = jnp.dot(xp, wg_ref[...].astype(BF16), preferred_element_type=F32)
    hu_ref[...] = jnp.dot(xp, wu_ref[...].astype(BF16), preferred_element_type=F32)

    def conv(h_ref, cw_ref, cb_ref):
        last = pltpu.roll(h_ref[rows - sub:rows, :], 1, 0)
        last2 = pltpu.roll(h_ref[rows - 2 * sub:rows - sub, :], 1, 0)
        back1 = jnp.concatenate([last, h_ref[0:rows - sub, :]], axis=0)
        back2 = jnp.concatenate([last2, last, h_ref[0:rows - 2 * sub, :]], axis=0)
        return (cb_ref[...] + cw_ref[2:3, :] * h_ref[...] + cw_ref[1:2, :] * back1 + cw_ref[0:1, :] * back2)

    gate = conv(hg_ref, cwg_ref, cbg_ref)
    up = conv(hu_ref, cwu_ref, cbu_ref)
    act = gate * _sigmoid(gate) * up
    for v in range(seg):
        for sl in range(tn // V7X_LANES):
            unperm_ref[sl, pl.ds(v, sub, stride=seg), :] = act[v * sub:(v + 1) * sub,
                                                               sl * V7X_LANES:(sl + 1) * V7X_LANES]
    o_ref[...] = jnp.concatenate([unperm_ref[sl, CONV_HALO:, :] for sl in range(tn // V7X_LANES)],
                                 axis=1).astype(o_ref.dtype)


def _ffn_up(h, row_ss, w_up, layer, conv_w, conv_b, seq):
    m, d = h.shape
    f = w_up.shape[-1] // 2
    tm = min(FFN_ROWS, seq)
    tn = min(FFN_COLS, f)
    rows = tm + CONV_HALO
    assert seq % tm == 0 and f % tn == 0 and tm % CONV_HALO == 0 and CONV_WIDTH == 3
    assert rows % (2 * V7X_SUBLANES) == 0 and d % (FFN_STAGE_SLABS * V7X_LANES) == 0 and tn % V7X_LANES == 0
    nj = f // tn
    halo_blocks = tm // CONV_HALO
    blk = (2 * (tm * d * 2 + CONV_HALO * d * 2 + 2 * d * tn * 4 + tm * tn * 2) + 2 * d * tn * 6
           + rows * d * 2 + FFN_STAGE_SLABS * rows * V7X_LANES * 4 + 3 * rows * tn * 4 + 8 * tm * tn * 4)
    return pl.pallas_call(
        functools.partial(_ffn_up_kernel, tiles_per_seq=seq // tm),
        out_shape=jax.ShapeDtypeStruct((m, f), BF16),
        grid=(m // tm, nj),
        in_specs=[pl.BlockSpec((CONV_HALO, d), lambda i, j: (jnp.maximum(i * halo_blocks - 1, 0), 0)),
                  pl.BlockSpec((tm, d), lambda i, j: (i, 0)),
                  pl.BlockSpec((CONV_HALO, V7X_LANES), lambda i, j: (jnp.maximum(i * halo_blocks - 1, 0), 0)),
                  pl.BlockSpec((tm, V7X_LANES), lambda i, j: (i, 0)),
                  pl.BlockSpec((None, d, tn), lambda i, j: (layer, 0, j)),
                  pl.BlockSpec((None, d, tn), lambda i, j: (layer, 0, nj + j)),
                  pl.BlockSpec((CONV_WIDTH, tn), lambda i, j: (0, j)),
                  pl.BlockSpec((CONV_WIDTH, tn), lambda i, j: (0, nj + j)),
                  pl.BlockSpec((1, tn), lambda i, j: (0, j)),
                  pl.BlockSpec((1, tn), lambda i, j: (0, nj + j))],
        out_specs=pl.BlockSpec((tm, tn), lambda i, j: (i, j)),
        scratch_shapes=[pltpu.VMEM((rows, d), BF16),
                        pltpu.VMEM((FFN_STAGE_SLABS, rows, V7X_LANES), F32),
                        pltpu.VMEM((rows, tn), F32),
                        pltpu.VMEM((rows, tn), F32),
                        pltpu.VMEM((tn // V7X_LANES, rows, V7X_LANES), F32)],
        compiler_params=_params(("arbitrary", "arbitrary"), blk),
        name="ffn_up_conv_gate",
    )(h, h, row_ss, row_ss, w_up, w_up, conv_w, conv_w, conv_b.reshape(1, 2 * f), conv_b.reshape(1, 2 * f))


def _conv_ffn(x, xg, row_ss, w_up, layer, conv_w, conv_b, w_down, seq, next_gain=None):
    act = _ffn_up(xg, row_ss, w_up, layer, conv_w, conv_b, seq)
    return _matmul(act, w_down, F32, residual=x, layer=layer, next_gain=next_gain, rows=DOWN_ROWS, cols=DOWN_COLS)


def kernel(x, norm_mix, norm_ffn, a_w_in, a_gate_bias, a_head_norm, a_w_out, b_w_qkv, b_w_out,
           ffn_w_up, ffn_conv_w, ffn_conv_b, ffn_w_down, final_norm):
    bsz, seq, d = x.shape
    m = bsz * seq
    x = x.reshape(m, d)

    w_down = ffn_w_down.astype(BF16)

    w_in_t = jnp.swapaxes(a_w_in, 1, 2)
    n_main = w_in_t.shape[1] - 2 * ML_HEADS
    w_gates = jnp.pad(w_in_t[0, n_main:, :].T, ((0, 0), (0, V7X_LANES - 2 * ML_HEADS)))
    gate_bias = jnp.pad(a_gate_bias[0], (0, V7X_LANES - 2 * ML_HEADS)).reshape(1, V7X_LANES)
    h, gates = _rmsnorm_gates(x, norm_mix[0], w_gates, gate_bias)
    proj = _matmul(h, w_in_t, BF16, n=n_main, layer=0, b_is_nk=True, ring=True)
    mixed = _mlstm_scan(proj, gates, a_head_norm[0], bsz, seq)
    x, xg, row_ss = _matmul(mixed, a_w_out, F32, residual=x, layer=0, next_gain=norm_ffn[0])
    x, xg, row_ss = _conv_ffn(x, xg, row_ss, ffn_w_up, 0, ffn_conv_w[0], ffn_conv_b[0], w_down, seq,
                              next_gain=norm_mix[1])

    q_scale = jnp.where(jnp.arange(3 * d) < d, (d // MB_HEADS) ** -0.5 * LOG2_E, 1.0).astype(F32)
    qkv = _matmul(xg, b_w_qkv, BF16, layer=0, col_scale=q_scale.reshape(1, 3 * d), row_ss=row_ss, ring=True)
    attn = _moba_attention(qkv, bsz, seq)
    x, xg, row_ss = _matmul(attn, b_w_out, F32, residual=x, layer=0, next_gain=norm_ffn[1])
    x = _conv_ffn(x, xg, row_ss, ffn_w_up, 1, ffn_conv_w[1], ffn_conv_b[1], w_down, seq)

    return _rmsnorm(x, final_norm, F32).reshape(bsz, seq, d)
```

```python
import functools
import math

import jax
import jax.numpy as jnp
from jax import lax
from jax.experimental import pallas as pl
from jax.experimental.pallas import tpu as pltpu

F32 = jnp.float32
BF16 = jnp.bfloat16

NORM_EPS = 1e-6
ML_HEADS = 8
GATE_SOFTCAP = 15.0
MB_HEADS = 32
MB_BLOCK = 256
MB_TOPK = 3
CONV_WIDTH = 3

V7X_LANES = 128
V7X_SUBLANES = 8
V7X_BF16_SUBLANES = 16
V7X_VMEM_BYTES = 64 * 1024 * 1024
VMEM_RESERVE_BYTES = 6 * 1024 * 1024

ML_CHUNK = 256
NORM_ROWS = 512
MM_ROWS = 1024
MM_COLS = 512
DOWN_ROWS = 512
DOWN_COLS = 512
WEIGHT_RING_SLOTS = 3
FFN_ROWS = 1024
FFN_STAGE_SLABS = 4
FFN_COLS = 256
CONV_HALO = V7X_BF16_SUBLANES
MB_GROUP = 2
MB_HEADS_PER_STEP = 2
MB_QBLOCKS_PER_STEP = 2

LOG2_E = math.log2(math.e)
MB_MASK = 2.0 ** 100


def _vmem_limit(block_bytes):
    want = int(block_bytes) + VMEM_RESERVE_BYTES
    return max(min(want, V7X_VMEM_BYTES - VMEM_RESERVE_BYTES), 16 * 1024 * 1024)


def _params(sem, block_bytes):
    return pltpu.CompilerParams(dimension_semantics=sem, vmem_limit_bytes=_vmem_limit(block_bytes))


def _sigmoid(x):
    return 1.0 / (1.0 + jnp.exp(-x))


def _rmsnorm_kernel(x_ref, g_ref, o_ref):
    x = x_ref[...]
    ms = jnp.mean(x * x, axis=-1, keepdims=True)
    o_ref[...] = (x * lax.rsqrt(ms + NORM_EPS) * g_ref[...]).astype(o_ref.dtype)


def _rmsnorm(x, g, out_dtype):
    m, d = x.shape
    tm = min(NORM_ROWS, m)
    assert m % tm == 0
    blk = 2 * tm * d * (4 + jnp.dtype(out_dtype).itemsize) + 3 * tm * d * 4
    return pl.pallas_call(
        _rmsnorm_kernel,
        out_shape=jax.ShapeDtypeStruct((m, d), out_dtype),
        grid=(m // tm,),
        in_specs=[pl.BlockSpec((tm, d), lambda i: (i, 0)), pl.BlockSpec((1, d), lambda i: (0, 0))],
        out_specs=pl.BlockSpec((tm, d), lambda i: (i, 0)),
        compiler_params=_params(("arbitrary",), blk),
        name="rmsnorm",
    )(x, g.reshape(1, d))


def _rmsnorm_gates_kernel(x_ref, g_ref, wg_ref, b_ref, o_ref, gate_ref):
    x = x_ref[...]
    ms = jnp.mean(x * x, axis=-1, keepdims=True)
    y = x * lax.rsqrt(ms + NORM_EPS) * g_ref[...]
    o_ref[...] = y.astype(o_ref.dtype)
    y_hi = y.astype(BF16)
    y_lo = (y - y_hi.astype(F32)).astype(BF16)
    w = wg_ref[...]
    w_hi = w.astype(BF16)
    w_lo = (w - w_hi.astype(F32)).astype(BF16)
    gate_ref[...] = (jnp.dot(y_hi, w_hi, preferred_element_type=F32) + jnp.dot(y_hi, w_lo, preferred_element_type=F32)
                     + jnp.dot(y_lo, w_hi, preferred_element_type=F32) + b_ref[...])


def _rmsnorm_gates(x, g, w_gates, bias):
    m, d = x.shape
    tm = min(NORM_ROWS, m)
    assert m % tm == 0 and w_gates.shape == (d, V7X_LANES)
    blk = 2 * tm * d * 6 + 2 * d * V7X_LANES * 4 + 4 * tm * d * 4
    return pl.pallas_call(
        _rmsnorm_gates_kernel,
        out_shape=(jax.ShapeDtypeStruct((m, d), BF16), jax.ShapeDtypeStruct((m, V7X_LANES), F32)),
        grid=(m // tm,),
        in_specs=[pl.BlockSpec((tm, d), lambda i: (i, 0)), pl.BlockSpec((1, d), lambda i: (0, 0)),
                  pl.BlockSpec((d, V7X_LANES), lambda i: (0, 0)), pl.BlockSpec((1, V7X_LANES), lambda i: (0, 0))],
        out_specs=(pl.BlockSpec((tm, d), lambda i: (i, 0)), pl.BlockSpec((tm, V7X_LANES), lambda i: (i, 0))),
        compiler_params=_params(("arbitrary",), blk),
        name="rmsnorm_gates",
    )(x, g.reshape(1, d), w_gates, bias)


def _row_rsqrt(ss, width):
    return lax.rsqrt(jnp.sum(ss, axis=-1, keepdims=True) * (1.0 / width) + NORM_EPS)


def _ring_tile(b_hbm, wbuf, sem, layer, tn, b_is_nk):
    nj = pl.num_programs(1)
    total = pl.num_programs(0) * nj
    s = pl.program_id(0) * nj + pl.program_id(1)
    ahead = WEIGHT_RING_SLOTS - 1

    def copy(t):
        col = pl.multiple_of(lax.rem(t, nj) * tn, tn)
        src = b_hbm if layer is None else b_hbm.at[layer]
        src = src.at[pl.ds(col, tn), :] if b_is_nk else src.at[:, pl.ds(col, tn)]
        slot = lax.rem(t, WEIGHT_RING_SLOTS)
        return pltpu.make_async_copy(src, wbuf.at[slot], sem.at[slot])

    @pl.when(s == 0)
    def _():
        for t in range(ahead):
            @pl.when(t < total)
            def _():
                copy(t).start()

    @pl.when(s + ahead < total)
    def _():
        copy(s + ahead).start()

    copy(s).wait()
    return wbuf[lax.rem(s, WEIGHT_RING_SLOTS)]


def _mm_kernel(*refs, has_scale, has_rowss, has_res, has_gain, b_is_nk, ring_layer, ring):
    refs = list(refs)
    a_ref, b_ref = refs[0], refs[1]
    pos = 2
    if ring:
        b = _ring_tile(b_ref, refs[-2], refs[-1], ring_layer, refs[-2].shape[2 if not b_is_nk else 1], b_is_nk)
    else:
        b = b_ref[...]
    if has_scale:
        b = b * refs[pos][...]
        pos += 1
    contract = (((1,), (1 if b_is_nk else 0,)), ((), ()))
    acc = lax.dot_general(a_ref[...], b.astype(BF16), contract, preferred_element_type=F32)
    if has_rowss:
        acc = acc * _row_rsqrt(refs[pos][...], a_ref.shape[1])
        pos += 1
    if has_res:
        acc = acc + refs[pos][...]
        pos += 1
    if has_gain:
        gain_ref, o_ref, og_ref, ss_ref = refs[pos:pos + 4]
        og_ref[...] = (acc * gain_ref[...]).astype(og_ref.dtype)
        sq = acc * acc
        part = sq[:, 0:V7X_LANES]
        for c in range(1, acc.shape[1] // V7X_LANES):
            part = part + sq[:, c * V7X_LANES:(c + 1) * V7X_LANES]

        @pl.when(pl.program_id(1) == 0)
        def _():
            ss_ref[...] = jnp.zeros_like(ss_ref)
        ss_ref[...] += part
    else:
        o_ref = refs[pos]
    o_ref[...] = acc.astype(o_ref.dtype)


def _matmul(a, b, out_dtype, residual=None, n=None, layer=None, col_scale=None, b_is_nk=False,
            row_ss=None, next_gain=None, ring=False, rows=MM_ROWS, cols=MM_COLS):
    m, k = a.shape
    n_axis, k_axis = (-2, -1) if b_is_nk else (-1, -2)
    n = b.shape[n_axis] if n is None else n
    tm, tn = min(rows, m), min(cols, n)
    assert m % tm == 0 and n % tn == 0 and n <= b.shape[n_axis] and b.shape[k_axis] == k
    assert not (b_is_nk and col_scale is not None) and tn % V7X_LANES == 0
    osz = jnp.dtype(out_dtype).itemsize
    blk = 2 * (tm * k * 2 + k * tn * b.dtype.itemsize + tm * tn * osz) + tm * tn * 4 + k * tn * 6
    b_block, b_index = ((tn, k), lambda j: (j, 0)) if b_is_nk else ((k, tn), lambda j: (0, j))
    scratch = []
    if ring:
        b_spec = pl.BlockSpec(memory_space=pl.ANY)
        scratch = [pltpu.VMEM((WEIGHT_RING_SLOTS,) + b_block, b.dtype), pltpu.SemaphoreType.DMA((WEIGHT_RING_SLOTS,))]
        blk += (WEIGHT_RING_SLOTS - 2) * k * tn * b.dtype.itemsize
    elif layer is None:
        b_spec = pl.BlockSpec(b_block, lambda i, j: b_index(j))
    else:
        b_spec = pl.BlockSpec((None,) + b_block, lambda i, j: (layer,) + b_index(j))
    tile = pl.BlockSpec((tm, tn), lambda i, j: (i, j))
    row_stat = pl.BlockSpec((tm, V7X_LANES), lambda i, j: (i, 0))
    in_specs = [pl.BlockSpec((tm, k), lambda i, j: (i, 0)), b_spec]
    args = [a, b]
    if col_scale is not None:
        in_specs.append(pl.BlockSpec((1, tn), lambda i, j: (0, j)))
        args.append(col_scale)
    if row_ss is not None:
        in_specs.append(row_stat)
        args.append(row_ss)
    if residual is not None:
        in_specs.append(tile)
        args.append(residual)
        blk += 2 * tm * tn * 4
    out_shape = jax.ShapeDtypeStruct((m, n), out_dtype)
    out_specs = tile
    if next_gain is not None:
        in_specs.append(pl.BlockSpec((1, tn), lambda i, j: (0, j)))
        args.append(next_gain.reshape(1, n))
        out_shape = (out_shape, jax.ShapeDtypeStruct((m, n), BF16), jax.ShapeDtypeStruct((m, V7X_LANES), F32))
        out_specs = (tile, tile, row_stat)
        blk += 2 * tm * tn * 2 + 4 * tm * tn * 4
    return pl.pallas_call(
        functools.partial(_mm_kernel, has_scale=col_scale is not None, has_rowss=row_ss is not None,
                          has_res=residual is not None, has_gain=next_gain is not None, b_is_nk=b_is_nk,
                          ring_layer=layer, ring=ring),
        out_shape=out_shape,
        grid=(m // tm, n // tn),
        in_specs=in_specs,
        out_specs=out_specs,
        scratch_shapes=scratch,
        compiler_params=_params(("arbitrary", "arbitrary"), blk),
        name="matmul_res" if residual is not None else "matmul",
    )(*args)


def _mlstm_kernel(q_ref, k_ref, v_ref, o_ref, g_ref, gain_ref, out_ref, c_ref, n_ref, m_ref, *, dk, dv):
    L = q_ref.shape[0]
    heads = ML_HEADS

    @pl.when(pl.program_id(1) == 0)
    def _():
        c_ref[...] = jnp.zeros_like(c_ref)
        n_ref[...] = jnp.zeros_like(n_ref)
        m_ref[...] = jnp.zeros_like(m_ref)

    rows = lax.broadcasted_iota(jnp.int32, (L, L), 0)
    cols = lax.broadcasted_iota(jnp.int32, (L, L), 1)
    causal = cols <= rows
    tril = causal.astype(F32)

    g = g_ref[...]
    gcap = GATE_SOFTCAP * jnp.tanh(g / GATE_SOFTCAP)
    log_f = jnp.minimum(gcap, 0.0) - jnp.log(1.0 + jnp.exp(-jnp.abs(gcap)))
    bcum = jnp.dot(tril, log_f, precision=lax.Precision.HIGHEST, preferred_element_type=F32)
    lane = lax.broadcasted_iota(jnp.int32, g.shape, 1)
    gates = jnp.where(lane < heads, gcap, bcum)
    gates_t = gates.T

    scale = dk ** -0.5
    nt = (((1,), (1,)), ((), ()))
    tn = (((0,), (0,)), ((), ()))
    for hd in range(heads):
        q = q_ref[:, hd * dk:(hd + 1) * dk]
        k = k_ref[:, hd * dk:(hd + 1) * dk]
        v = v_ref[:, hd * dv:(hd + 1) * dv]
        i_col = gates[:, hd:hd + 1]
        b_col = gates[:, heads + hd:heads + hd + 1]
        i_row = gates_t[hd:hd + 1, :]
        b_row = gates_t[heads + hd:heads + hd + 1, :]
        m_prev = m_ref[hd, :, 0:1]
        c_prev = c_ref[hd]
        n_prev = n_ref[hd]

        dmat = jnp.where(causal, b_col - b_row + i_row, -jnp.inf)
        inter = b_col + m_prev
        m_row = jnp.maximum(jnp.max(dmat, axis=-1, keepdims=True), inter)
        a_inter = jnp.exp(inter - m_row)
        s = lax.dot_general(q, k, nt, preferred_element_type=F32) * scale * jnp.exp(dmat - m_row)
        num = (a_inter * jnp.dot(q, c_prev.astype(BF16), preferred_element_type=F32)
               + jnp.dot(s.astype(BF16), v, preferred_element_type=F32))
        den = (a_inter * jnp.sum(q.astype(F32) * n_prev, axis=-1, keepdims=True)
               + jnp.sum(s, axis=-1, keepdims=True))
        h = num / jnp.maximum(jnp.abs(den), jnp.exp(-m_row))

        b_last = b_col[L - 1:L, :]
        dec = b_last - b_col + i_col
        m_new = jnp.maximum(b_last + m_prev, jnp.max(dec, axis=0, keepdims=True))
        a_old = jnp.exp(b_last + m_prev - m_new)
        kw = k.astype(F32) * (jnp.exp(dec - m_new) * scale)
        c_ref[hd] = a_old * c_prev + lax.dot_general(kw.astype(BF16), v, tn, preferred_element_type=F32)
        n_ref[hd] = a_old * n_prev + jnp.sum(kw, axis=0, keepdims=True)
        m_ref[hd] = jnp.broadcast_to(m_new, m_ref.shape[1:])

        h = h * lax.rsqrt(jnp.mean(h * h, axis=-1, keepdims=True) + NORM_EPS)
        h = h * gain_ref[:, hd * dv:(hd + 1) * dv]
        h = h * _sigmoid(o_ref[:, hd * dv:(hd + 1) * dv].astype(F32))
        out_ref[:, hd * dv:(hd + 1) * dv] = h.astype(out_ref.dtype)


def _mlstm_scan(proj, gates, head_gain, bsz, seq):
    m, width = proj.shape
    d = head_gain.shape[0]
    dv = d // ML_HEADS
    dk = dv // 2
    hk = ML_HEADS * dk
    assert width == 2 * hk + 2 * d and hk * 2 == d
    L = min(ML_CHUNK, seq)
    assert seq % L == 0
    nc = seq // L
    row = lambda b, c: b * nc + c
    blk = 2 * (L * (2 * hk + 2 * d) * 2 + L * V7X_LANES * 4 + d * 4 + L * d * 2) \
        + ML_HEADS * dk * dv * 4 + 16 * L * L * 4 + 8 * L * dv * 4
    return pl.pallas_call(
        functools.partial(_mlstm_kernel, dk=dk, dv=dv),
        out_shape=jax.ShapeDtypeStruct((m, d), BF16),
        grid=(bsz, nc),
        in_specs=[pl.BlockSpec((L, hk), lambda b, c: (row(b, c), 0)),
                  pl.BlockSpec((L, hk), lambda b, c: (row(b, c), 1)),
                  pl.BlockSpec((L, d), lambda b, c: (row(b, c), 1)),
                  pl.BlockSpec((L, d), lambda b, c: (row(b, c), 2)),
                  pl.BlockSpec((L, V7X_LANES), lambda b, c: (row(b, c), 0)),
                  pl.BlockSpec((1, d), lambda b, c: (0, 0))],
        out_specs=pl.BlockSpec((L, d), lambda b, c: (row(b, c), 0)),
        scratch_shapes=[pltpu.VMEM((ML_HEADS, dk, dv), F32),
                        pltpu.VMEM((ML_HEADS, 1, dk), F32),
                        pltpu.VMEM((ML_HEADS, 1, V7X_LANES), F32)],
        compiler_params=_params(("arbitrary", "arbitrary"), blk),
        name="mlstm_scan",
    )(proj, proj, proj, proj, gates, head_gain.reshape(1, d))


def _moba_kernel(q_ref, qnext_ref, k_ref, v_ref, o_ref, kx_ref, vt_ref, km_ref, qx_ref, sown_ref, s_ref, acc_ref, *,
                 nb, bs, dh, hps, grp, qps):
    step = pl.program_id(2)
    dh_ext = vt_ref.shape[2]
    nbp = vt_ref.shape[1]
    nt = (((1,), (1,)), ((), ()))
    cw = grp * bs

    blk_idx = lax.broadcasted_iota(jnp.int32, (nb, bs), 0)
    row_idx = lax.broadcasted_iota(jnp.int32, (dh, bs), 0)
    streams = [(h, r) for h in range(hps) for r in range(qps)]

    def select(q, qb, h):
        km = km_ref[h]
        km_hi = km.astype(BF16)
        r1 = km - km_hi.astype(F32)
        km_mid = r1.astype(BF16)
        km_lo = (r1 - km_mid.astype(F32)).astype(BF16)
        k_own = k_ref[pl.ds(pl.multiple_of(jnp.minimum(qb, nb - 1) * bs, bs), bs), h * dh:(h + 1) * dh]
        g3 = lax.dot_general(jnp.concatenate([km_hi, km_mid, km_lo, k_own], axis=0), q, nt,
                             preferred_element_type=F32)
        gate = g3[0:nb] + g3[nb:2 * nb] + g3[2 * nb:3 * nb]
        removed = blk_idx >= qb
        sel = jnp.zeros((nb, bs), F32)
        for rank in range(MB_TOPK):
            gm = jnp.where(removed, -jnp.inf, gate)
            mx = jnp.max(gm, axis=0, keepdims=True)
            cand = jnp.logical_and(jnp.logical_not(removed), gm == mx)
            idx = jnp.min(jnp.where(cand, blk_idx, nb), axis=0, keepdims=True)
            hit = blk_idx == idx
            sel = jnp.where(jnp.logical_and(hit, qb > rank), 1.0, sel)
            removed = jnp.logical_or(removed, hit)
        sel_pad = jnp.concatenate([sel, jnp.zeros((dh - nb, bs), F32)], axis=0)
        neg = jnp.where(jnp.logical_or(sel_pad > 0.0, row_idx > nb), 0.0, -MB_MASK)
        return jnp.concatenate([q, neg.T.astype(BF16)], axis=1), g3[3 * nb:, :]

    @pl.when(step == 0)
    def _():
        lane = lax.broadcasted_iota(jnp.int32, (bs, dh), 1)

        def prep(n, carry):
            r0 = pl.multiple_of(n * bs, bs)
            onehot = (lane == n).astype(BF16)
            for h in range(hps):
                vb = v_ref[pl.ds(r0, bs), h * dh:(h + 1) * dh].astype(F32)
                vt_ref[h, n, 0:dh, :] = vb.T.astype(BF16)
                vt_ref[h, n, dh:dh_ext, :] = jnp.ones((dh_ext - dh, bs), BF16)
                kb = k_ref[pl.ds(r0, bs), h * dh:(h + 1) * dh]
                kx_ref[h, pl.ds(r0, bs), 0:dh] = kb
                kx_ref[h, pl.ds(r0, bs), dh:2 * dh] = onehot
                km_ref[h, pl.ds(n, 1), :] = jnp.mean(kb.astype(F32), axis=0, keepdims=True)
            return carry
        lax.fori_loop(0, nb, prep, 0)
        masked = (lane == nb).astype(BF16)
        for h in range(hps):
            for n in range(nb, nbp):
                vt_ref[h, n] = jnp.zeros((dh_ext, bs), BF16)
                kx_ref[h, n * bs:(n + 1) * bs, 0:dh] = jnp.zeros((bs, dh), BF16)
                kx_ref[h, n * bs:(n + 1) * bs, dh:2 * dh] = masked
        for t, (h, r) in enumerate(streams):
            qx_ref[t], sown_ref[t] = select(q_ref[r * bs:(r + 1) * bs, h * dh:(h + 1) * dh], r, h)

    kidx = lax.broadcasted_iota(jnp.int32, (bs, bs), 0)
    qidx = lax.broadcasted_iota(jnp.int32, (bs, bs), 1)

    qxs = [qx_ref[t] for t in range(len(streams))]
    m0s = []
    for t, (h, r) in enumerate(streams):
        qb = step * qps + r
        q = q_ref[r * bs:(r + 1) * bs, h * dh:(h + 1) * dh]

        s = jnp.where(kidx <= qidx, sown_ref[t], -jnp.inf)
        m0 = jnp.max(s, axis=0, keepdims=True)
        p = jnp.exp2(s - m0)
        acc_ref[t] = jnp.dot(vt_ref[h, qb], p.astype(BF16), preferred_element_type=F32)
        m0s.append(m0)

        qx_ref[t], sown_ref[t] = select(qnext_ref[r * bs:(r + 1) * bs, h * dh:(h + 1) * dh],
                                        (step + 1) * qps + r, h)

    def score(c, slot):
        r0 = pl.multiple_of(c * cw, cw)
        for t, (h, r) in enumerate(streams):
            s_ref[slot, t] = lax.dot_general(kx_ref[h, pl.ds(r0, cw), :], qxs[t], nt, preferred_element_type=F32)

    def consume(c, slot, ms):
        new_ms = []
        for t, (h, r) in enumerate(streams):
            maxes, pvs = [], []
            for g in range(grp):
                sg = s_ref[slot, t, g * bs:(g + 1) * bs, :]
                mg = jnp.max(sg, axis=0, keepdims=True)
                p = jnp.exp2(sg - mg).astype(BF16)
                pvs.append(jnp.dot(vt_ref[h, c * grp + g], p, preferred_element_type=F32))
                maxes.append(mg)
            mx = ms[t]
            for mg in maxes:
                mx = jnp.maximum(mx, mg)
            acc = acc_ref[t] * jnp.exp2(ms[t] - mx)
            for mg, pv in zip(maxes, pvs):
                acc = acc + pv * jnp.exp2(mg - mx)
            acc_ref[t] = acc
            new_ms.append(mx)
        return tuple(new_ms)

    def body(u, ms):
        c = 2 * u
        score(c + 1, 1)
        ms = consume(c, 0, ms)
        score(c + 2, 0)
        return consume(c + 1, 1, ms)

    n_chunks = lax.div(step * qps + (qps - 1) + (grp - 1), grp)
    score(0, 0)
    ms = lax.fori_loop(0, lax.div(n_chunks, 2), body, tuple(m0s))

    @pl.when(lax.rem(n_chunks, 2) == 1)
    def _():
        consume(n_chunks - 1, 0, ms)

    for t, (h, r) in enumerate(streams):
        acc = acc_ref[t]
        out = acc[0:dh, :] / acc[dh:dh + 1, :]
        o_ref[r * bs:(r + 1) * bs, h * dh:(h + 1) * dh] = out.T.astype(o_ref.dtype)


def _moba_attention(qkv, bsz, seq):
    m, width = qkv.shape
    d = width // 3
    dh = d // MB_HEADS
    bs = MB_BLOCK
    hps = MB_HEADS_PER_STEP
    nb = seq // bs
    grp = MB_GROUP
    nbp = (nb + grp - 1) // grp * grp + 2 * grp
    qps = min(MB_QBLOCKS_PER_STEP, nb)
    assert seq % bs == 0 and dh == V7X_LANES and MB_HEADS % hps == 0 and nb < dh and nb % qps == 0
    dh_ext = dh + V7X_BF16_SUBLANES
    wblk = hps * dh
    nhb = MB_HEADS // hps
    ns = nb // qps
    nstream = hps * qps
    blk = (2 * (2 * qps * bs * wblk * 2 + 2 * seq * wblk * 2) + hps * nbp * bs * (2 * dh + dh_ext) * 2
           + 2 * nstream * grp * bs * bs * 4 + nstream * (4 * grp + 8) * bs * bs * 4)
    return pl.pallas_call(
        functools.partial(_moba_kernel, nb=nb, bs=bs, dh=dh, hps=hps, grp=grp, qps=qps),
        out_shape=jax.ShapeDtypeStruct((m, d), BF16),
        grid=(bsz, nhb, ns),
        in_specs=[pl.BlockSpec((qps * bs, wblk), lambda b, h, i: (b * ns + i, h)),
                  pl.BlockSpec((qps * bs, wblk), lambda b, h, i: (b * ns + jnp.minimum(i + 1, ns - 1), h)),
                  pl.BlockSpec((seq, wblk), lambda b, h, i: (b, nhb + h)),
                  pl.BlockSpec((seq, wblk), lambda b, h, i: (b, 2 * nhb + h))],
        out_specs=pl.BlockSpec((qps * bs, wblk), lambda b, h, i: (b * ns + i, h)),
        scratch_shapes=[pltpu.VMEM((hps, nbp * bs, 2 * dh), BF16),
                        pltpu.VMEM((hps, nbp, dh_ext, bs), BF16),
                        pltpu.VMEM((hps, nb, dh), F32),
                        pltpu.VMEM((nstream, bs, 2 * dh), BF16),
                        pltpu.VMEM((nstream, bs, bs), F32),
                        pltpu.VMEM((2, nstream, grp * bs, bs), F32),
                        pltpu.VMEM((nstream, dh_ext, bs), F32)],
        compiler_params=_params(("arbitrary", "arbitrary", "arbitrary"), blk),
        name="moba_attention",
    )(qkv, qkv, qkv, qkv)


def _ffn_up_kernel(xprev_ref, x_ref, ssprev_ref, ss_ref, wg_ref, wu_ref, cwg_ref, cwu_ref, cbg_ref, cbu_ref, o_ref,
                   xperm_ref, stage_ref, hg_ref, hu_ref, unperm_ref, *, tiles_per_seq):
    tm, d = x_ref.shape
    rows = tm + CONV_HALO
    sub = V7X_SUBLANES
    seg = rows // sub
    tn = o_ref.shape[1]
    nslab = stage_ref.shape[0]

    @pl.when(pl.program_id(1) == 0)
    def _():
        seq_start = (pl.program_id(0) % tiles_per_seq) == 0
        r_prev = jnp.where(seq_start, 0.0, _row_rsqrt(ssprev_ref[...], d))
        r_tile = _row_rsqrt(ss_ref[...], d)
        for g in range(d // (nslab * V7X_LANES)):
            for sl in range(nslab):
                c0 = (g * nslab + sl) * V7X_LANES
                stage_ref[sl, 0:CONV_HALO, :] = xprev_ref[:, c0:c0 + V7X_LANES].astype(F32) * r_prev
                stage_ref[sl, CONV_HALO:, :] = x_ref[:, c0:c0 + V7X_LANES].astype(F32) * r_tile

            def gather(i, carry):
                v = 2 * i
                for sl in range(nslab):
                    c0 = (g * nslab + sl) * V7X_LANES
                    a = stage_ref[sl, pl.ds(v, sub, stride=seg), :]
                    b = stage_ref[sl, pl.ds(v + 1, sub, stride=seg), :]
                    r0 = pl.multiple_of(v * sub, 2 * sub)
                    xperm_ref[pl.ds(r0, 2 * sub), c0:c0 + V7X_LANES] = jnp.concatenate([a, b], axis=0).astype(BF16)
                return carry
            lax.fori_loop(0, seg // 2, gather, 0)

    xp = xperm_ref[...]
    hg_ref[...] = jnp.dot(xp, wg_ref[...].astype(BF16), preferred_element_type=F32)
    hu_ref[...] = jnp.dot(xp, wu_ref[...].astype(BF16), preferred_element_type=F32)

    def conv(h_ref, cw_ref, cb_ref):
        last = pltpu.roll(h_ref[rows - sub:rows, :], 1, 0)
        last2 = pltpu.roll(h_ref[rows - 2 * sub:rows - sub, :], 1, 0)
        back1 = jnp.concatenate([last, h_ref[0:rows - sub, :]], axis=0)
        back2 = jnp.concatenate([last2, last, h_ref[0:rows - 2 * sub, :]], axis=0)
        return (cb_ref[...] + cw_ref[2:3, :] * h_ref[...] + cw_ref[1:2, :] * back1 + cw_ref[0:1, :] * back2)

    gate = conv(hg_ref, cwg_ref, cbg_ref)
    up = conv(hu_ref, cwu_ref, cbu_ref)
    act = gate * _sigmoid(gate) * up
    for v in range(seg):
        for sl in range(tn // V7X_LANES):
            unperm_ref[sl, pl.ds(v, sub, stride=seg), :] = act[v * sub:(v + 1) * sub,
                                                               sl * V7X_LANES:(sl + 1) * V7X_LANES]
    o_ref[...] = jnp.concatenate([unperm_ref[sl, CONV_HALO:, :] for sl in range(tn // V7X_LANES)],
                                 axis=1).astype(o_ref.dtype)


def _ffn_up(h, row_ss, w_up, layer, conv_w, conv_b, seq):
    m, d = h.shape
    f = w_up.shape[-1] // 2
    tm = min(FFN_ROWS, seq)
    tn = min(FFN_COLS, f)
    rows = tm + CONV_HALO
    assert seq % tm == 0 and f % tn == 0 and tm % CONV_HALO == 0 and CONV_WIDTH == 3
    assert rows % (2 * V7X_SUBLANES) == 0 and d % (FFN_STAGE_SLABS * V7X_LANES) == 0 and tn % V7X_LANES == 0
    nj = f // tn
    halo_blocks = tm // CONV_HALO
    blk = (2 * (tm * d * 2 + CONV_HALO * d * 2 + 2 * d * tn * 4 + tm * tn * 2) + 2 * d * tn * 6
           + rows * d * 2 + FFN_STAGE_SLABS * rows * V7X_LANES * 4 + 3 * rows * tn * 4 + 8 * tm * tn * 4)
    return pl.pallas_call(
        functools.partial(_ffn_up_kernel, tiles_per_seq=seq // tm),
        out_shape=jax.ShapeDtypeStruct((m, f), BF16),
        grid=(m // tm, nj),
        in_specs=[pl.BlockSpec((CONV_HALO, d), lambda i, j: (jnp.maximum(i * halo_blocks - 1, 0), 0)),
                  pl.BlockSpec((tm, d), lambda i, j: (i, 0)),
                  pl.BlockSpec((CONV_HALO, V7X_LANES), lambda i, j: (jnp.maximum(i * halo_blocks - 1, 0), 0)),
                  pl.BlockSpec((tm, V7X_LANES), lambda i, j: (i, 0)),
                  pl.BlockSpec((None, d, tn), lambda i, j: (layer, 0, j)),
                  pl.BlockSpec((None, d, tn), lambda i, j: (layer, 0, nj + j)),
                  pl.BlockSpec((CONV_WIDTH, tn), lambda i, j: (0, j)),
                  pl.BlockSpec((CONV_WIDTH, tn), lambda i, j: (0, nj + j)),
                  pl.BlockSpec((1, tn), lambda i, j: (0, j)),
                  pl.BlockSpec((1, tn), lambda i, j: (0, nj + j))],
        out_specs=pl.BlockSpec((tm, tn), lambda i, j: (i, j)),
        scratch_shapes=[pltpu.VMEM((rows, d), BF16),
                        pltpu.VMEM((FFN_STAGE_SLABS, rows, V7X_LANES), F32),
                        pltpu.VMEM((rows, tn), F32),
                        pltpu.VMEM((rows, tn), F32),
                        pltpu.VMEM((tn // V7X_LANES, rows, V7X_LANES), F32)],
        compiler_params=_params(("arbitrary", "arbitrary"), blk),
        name="ffn_up_conv_gate",
    )(h, h, row_ss, row_ss, w_up, w_up, conv_w, conv_w, conv_b.reshape(1, 2 * f), conv_b.reshape(1, 2 * f))


def _conv_ffn(x, xg, row_ss, w_up, layer, conv_w, conv_b, w_down, seq, next_gain=None):
    act = _ffn_up(xg, row_ss, w_up, layer, conv_w, conv_b, seq)
    return _matmul(act, w_down, F32, residual=x, layer=layer, next_gain=next_gain, rows=DOWN_ROWS, cols=DOWN_COLS)


def kernel(x, norm_mix, norm_ffn, a_w_in, a_gate_bias, a_head_norm, a_w_out, b_w_qkv, b_w_out,
           ffn_w_up, ffn_conv_w, ffn_conv_b, ffn_w_down, final_norm):
    bsz, seq, d = x.shape
    m = bsz * seq
    x = x.reshape(m, d)

    w_down = ffn_w_down.astype(BF16)

    w_in_t = jnp.swapaxes(a_w_in, 1, 2)
    n_main = w_in_t.shape[1] - 2 * ML_HEADS
    w_gates = jnp.pad(w_in_t[0, n_main:, :].T, ((0, 0), (0, V7X_LANES - 2 * ML_HEADS)))
    gate_bias = jnp.pad(a_gate_bias[0], (0, V7X_LANES - 2 * ML_HEADS)).reshape(1, V7X_LANES)
    h, gates = _rmsnorm_gates(x, norm_mix[0], w_gates, gate_bias)
    proj = _matmul(h, w_in_t, BF16, n=n_main, layer=0, b_is_nk=True, ring=True)
    mixed = _mlstm_scan(proj, gates, a_head_norm[0], bsz, seq)
    x, xg, row_ss = _matmul(mixed, a_w_out, F32, residual=x, layer=0, next_gain=norm_ffn[0])
    x, xg, row_ss = _conv_ffn(x, xg, row_ss, ffn_w_up, 0, ffn_conv_w[0], ffn_conv_b[0], w_down, seq,
                              next_gain=norm_mix[1])

    q_scale = jnp.where(jnp.arange(3 * d) < d, (d // MB_HEADS) ** -0.5 * LOG2_E, 1.0).astype(F32)
    qkv = _matmul(xg, b_w_qkv, BF16, layer=0, col_scale=q_scale.reshape(1, 3 * d), row_ss=row_ss, ring=True)
    attn = _moba_attention(qkv, bsz, seq)
    x, xg, row_ss = _matmul(attn, b_w_out, F32, residual=x, layer=0, next_gain=norm_ffn[1])
    x = _conv_ffn(x, xg, row_ss, ffn_w_up, 1, ffn_conv_w[1], ffn_conv_b[1], w_down, seq)

    return _rmsnorm(x, final_norm, F32).reshape(bsz, seq, d)
```

```python
import functools
import math

import jax
import jax.numpy as jnp
from jax import lax
from jax.experimental import pallas as pl
from jax.experimental.pallas import tpu as pltpu

F32 = jnp.float32
BF16 = jnp.bfloat16

NORM_EPS = 1e-6
ML_HEADS = 8
GATE_SOFTCAP = 15.0
MB_HEADS = 32
MB_BLOCK = 256
MB_TOPK = 3
CONV_WIDTH = 3

V7X_LANES = 128
V7X_SUBLANES = 8
V7X_BF16_SUBLANES = 16
V7X_VMEM_BYTES = 64 * 1024 * 1024
VMEM_RESERVE_BYTES = 6 * 1024 * 1024

ML_CHUNK = 256
NORM_ROWS = 512
MM_ROWS = 1024
MM_COLS = 512
DOWN_ROWS = 512
DOWN_COLS = 512
WEIGHT_RING_SLOTS = 3
FFN_ROWS = 1024
FFN_STAGE_SLABS = 4
FFN_COLS = 256
CONV_HALO = V7X_BF16_SUBLANES
MB_GROUP = 2
MB_HEADS_PER_STEP = 2
MB_QBLOCKS_PER_STEP = 2

LOG2_E = math.log2(math.e)
MB_MASK = 2.0 ** 100


def _vmem_limit(block_bytes):
    want = int(block_bytes) + VMEM_RESERVE_BYTES
    return max(min(want, V7X_VMEM_BYTES - VMEM_RESERVE_BYTES), 16 * 1024 * 1024)


def _params(sem, block_bytes):
    return pltpu.CompilerParams(dimension_semantics=sem, vmem_limit_bytes=_vmem_limit(block_bytes))


def _sigmoid(x):
    return 1.0 / (1.0 + jnp.exp(-x))


def _rmsnorm_kernel(x_ref, g_ref, o_ref):
    x = x_ref[...]
    ms = jnp.mean(x * x, axis=-1, keepdims=True)
    o_ref[...] = (x * lax.rsqrt(ms + NORM_EPS) * g_ref[...]).astype(o_ref.dtype)


def _rmsnorm(x, g, out_dtype):
    m, d = x.shape
    tm = min(NORM_ROWS, m)
    assert m % tm == 0
    blk = 2 * tm * d * (4 + jnp.dtype(out_dtype).itemsize) + 3 * tm * d * 4
    return pl.pallas_call(
        _rmsnorm_kernel,
        out_shape=jax.ShapeDtypeStruct((m, d), out_dtype),
        grid=(m // tm,),
        in_specs=[pl.BlockSpec((tm, d), lambda i: (i, 0)), pl.BlockSpec((1, d), lambda i: (0, 0))],
        out_specs=pl.BlockSpec((tm, d), lambda i: (i, 0)),
        compiler_params=_params(("arbitrary",), blk),
        name="rmsnorm",
    )(x, g.reshape(1, d))


def _rmsnorm_gates_kernel(x_ref, g_ref, wg_ref, b_ref, o_ref, gate_ref):
    x = x_ref[...]
    ms = jnp.mean(x * x, axis=-1, keepdims=True)
    y = x * lax.rsqrt(ms + NORM_EPS) * g_ref[...]
    o_ref[...] = y.astype(o_ref.dtype)
    y_hi = y.astype(BF16)
    y_lo = (y - y_hi.astype(F32)).astype(BF16)
    w = wg_ref[...]
    w_hi = w.astype(BF16)
    w_lo = (w - w_hi.astype(F32)).astype(BF16)
    gate_ref[...] = (jnp.dot(y_hi, w_hi, preferred_element_type=F32) + jnp.dot(y_hi, w_lo, preferred_element_type=F32)
                     + jnp.dot(y_lo, w_hi, preferred_element_type=F32) + b_ref[...])


def _rmsnorm_gates(x, g, w_gates, bias):
    m, d = x.shape
    tm = min(NORM_ROWS, m)
    assert m % tm == 0 and w_gates.shape == (d, V7X_LANES)
    blk = 2 * tm * d * 6 + 2 * d * V7X_LANES * 4 + 4 * tm * d * 4
    return pl.pallas_call(
        _rmsnorm_gates_kernel,
        out_shape=(jax.ShapeDtypeStruct((m, d), BF16), jax.ShapeDtypeStruct((m, V7X_LANES), F32)),
        grid=(m // tm,),
        in_specs=[pl.BlockSpec((tm, d), lambda i: (i, 0)), pl.BlockSpec((1, d), lambda i: (0, 0)),
                  pl.BlockSpec((d, V7X_LANES), lambda i: (0, 0)), pl.BlockSpec((1, V7X_LANES), lambda i: (0, 0))],
        out_specs=(pl.BlockSpec((tm, d), lambda i: (i, 0)), pl.BlockSpec((tm, V7X_LANES), lambda i: (i, 0))),
        compiler_params=_params(("arbitrary",), blk),
        name="rmsnorm_gates",
    )(x, g.reshape(1, d), w_gates, bias)


def _row_rsqrt(ss, width):
    return lax.rsqrt(jnp.sum(ss, axis=-1, keepdims=True) * (1.0 / width) + NORM_EPS)


def _ring_tile(b_hbm, wbuf, sem, layer, tn, b_is_nk):
    nj = pl.num_programs(1)
    total = pl.num_programs(0) * nj
    s = pl.program_id(0) * nj + pl.program_id(1)
    ahead = WEIGHT_RING_SLOTS - 1

    def copy(t):
        col = pl.multiple_of(lax.rem(t, nj) * tn, tn)
        src = b_hbm if layer is None else b_hbm.at[layer]
        src = src.at[pl.ds(col, tn), :] if b_is_nk else src.at[:, pl.ds(col, tn)]
        slot = lax.rem(t, WEIGHT_RING_SLOTS)
        return pltpu.make_async_copy(src, wbuf.at[slot], sem.at[slot])

    @pl.when(s == 0)
    def _():
        for t in range(ahead):
            @pl.when(t < total)
            def _():
                copy(t).start()

    @pl.when(s + ahead < total)
    def _():
        copy(s + ahead).start()

    copy(s).wait()
    return wbuf[lax.rem(s, WEIGHT_RING_SLOTS)]


def _mm_kernel(*refs, has_scale, has_rowss, has_res, has_gain, b_is_nk, ring_layer, ring):
    refs = list(refs)
    a_ref, b_ref = refs[0], refs[1]
    pos = 2
    if ring:
        b = _ring_tile(b_ref, refs[-2], refs[-1], ring_layer, refs[-2].shape[2 if not b_is_nk else 1], b_is_nk)
    else:
        b = b_ref[...]
    if has_scale:
        b = b * refs[pos][...]
        pos += 1
    contract = (((1,), (1 if b_is_nk else 0,)), ((), ()))
    acc = lax.dot_general(a_ref[...], b.astype(BF16), contract, preferred_element_type=F32)
    if has_rowss:
        acc = acc * _row_rsqrt(refs[pos][...], a_ref.shape[1])
        pos += 1
    if has_res:
        acc = acc + refs[pos][...]
        pos += 1
    if has_gain:
        gain_ref, o_ref, og_ref, ss_ref = refs[pos:pos + 4]
        og_ref[...] = (acc * gain_ref[...]).astype(og_ref.dtype)
        sq = acc * acc
        part = sq[:, 0:V7X_LANES]
        for c in range(1, acc.shape[1] // V7X_LANES):
            part = part + sq[:, c * V7X_LANES:(c + 1) * V7X_LANES]

        @pl.when(pl.program_id(1) == 0)
        def _():
            ss_ref[...] = jnp.zeros_like(ss_ref)
        ss_ref[...] += part
    else:
        o_ref = refs[pos]
    o_ref[...] = acc.astype(o_ref.dtype)


def _matmul(a, b, out_dtype, residual=None, n=None, layer=None, col_scale=None, b_is_nk=False,
            row_ss=None, next_gain=None, ring=False, rows=MM_ROWS, cols=MM_COLS):
    m, k = a.shape
    n_axis, k_axis = (-2, -1) if b_is_nk else (-1, -2)
    n = b.shape[n_axis] if n is None else n
    tm, tn = min(rows, m), min(cols, n)
    assert m % tm == 0 and n % tn == 0 and n <= b.shape[n_axis] and b.shape[k_axis] == k
    assert not (b_is_nk and col_scale is not None) and tn % V7X_LANES == 0
    osz = jnp.dtype(out_dtype).itemsize
    blk = 2 * (tm * k * 2 + k * tn * b.dtype.itemsize + tm * tn * osz) + tm * tn * 4 + k * tn * 6
    b_block, b_index = ((tn, k), lambda j: (j, 0)) if b_is_nk else ((k, tn), lambda j: (0, j))
    scratch = []
    if ring:
        b_spec = pl.BlockSpec(memory_space=pl.ANY)
        scratch = [pltpu.VMEM((WEIGHT_RING_SLOTS,) + b_block, b.dtype), pltpu.SemaphoreType.DMA((WEIGHT_RING_SLOTS,))]
        blk += (WEIGHT_RING_SLOTS - 2) * k * tn * b.dtype.itemsize
    elif layer is None:
        b_spec = pl.BlockSpec(b_block, lambda i, j: b_index(j))
    else:
        b_spec = pl.BlockSpec((None,) + b_block, lambda i, j: (layer,) + b_index(j))
    tile = pl.BlockSpec((tm, tn), lambda i, j: (i, j))
    row_stat = pl.BlockSpec((tm, V7X_LANES), lambda i, j: (i, 0))
    in_specs = [pl.BlockSpec((tm, k), lambda i, j: (i, 0)), b_spec]
    args = [a, b]
    if col_scale is not None:
        in_specs.append(pl.BlockSpec((1, tn), lambda i, j: (0, j)))
        args.append(col_scale)
    if row_ss is not None:
        in_specs.append(row_stat)
        args.append(row_ss)
    if residual is not None:
        in_specs.append(tile)
        args.append(residual)
        blk += 2 * tm * tn * 4
    out_shape = jax.ShapeDtypeStruct((m, n), out_dtype)
    out_specs = tile
    if next_gain is not None:
        in_specs.append(pl.BlockSpec((1, tn), lambda i, j: (0, j)))
        args.append(next_gain.reshape(1, n))
        out_shape = (out_shape, jax.ShapeDtypeStruct((m, n), BF16), jax.ShapeDtypeStruct((m, V7X_LANES), F32))
        out_specs = (tile, tile, row_stat)
        blk += 2 * tm * tn * 2 + 4 * tm * tn * 4
    return pl.pallas_call(
        functools.partial(_mm_kernel, has_scale=col_scale is not None, has_rowss=row_ss is not None,
                          has_res=residual is not None, has_gain=next_gain is not None, b_is_nk=b_is_nk,
                          ring_layer=layer, ring=ring),
        out_shape=out_shape,
        grid=(m // tm, n // tn),
        in_specs=in_specs,
        out_specs=out_specs,
        scratch_shapes=scratch,
        compiler_params=_params(("arbitrary", "arbitrary"), blk),
        name="matmul_res" if residual is not None else "matmul",
    )(*args)


def _mlstm_kernel(q_ref, k_ref, v_ref, o_ref, g_ref, gain_ref, out_ref, c_ref, n_ref, m_ref, *, dk, dv):
    L = q_ref.shape[0]
    heads = ML_HEADS

    @pl.when(pl.program_id(1) == 0)
    def _():
        c_ref[...] = jnp.zeros_like(c_ref)
        n_ref[...] = jnp.zeros_like(n_ref)
        m_ref[...] = jnp.zeros_like(m_ref)

    rows = lax.broadcasted_iota(jnp.int32, (L, L), 0)
    cols = lax.broadcasted_iota(jnp.int32, (L, L), 1)
    causal = cols <= rows
    tril = causal.astype(BF16)

    g = g_ref[...]
    gcap = GATE_SOFTCAP * jnp.tanh(g / GATE_SOFTCAP)
    log_f = jnp.minimum(gcap, 0.0) - jnp.log(1.0 + jnp.exp(-jnp.abs(gcap)))
    f_hi = log_f.astype(BF16)
    f_r1 = log_f - f_hi.astype(F32)
    f_mid = f_r1.astype(BF16)
    f_lo = (f_r1 - f_mid.astype(F32)).astype(BF16)
    parts = jnp.dot(tril, jnp.concatenate([f_hi, f_mid, f_lo], axis=1), preferred_element_type=F32)
    bcum = parts[:, 0:V7X_LANES] + parts[:, V7X_LANES:2 * V7X_LANES] + parts[:, 2 * V7X_LANES:3 * V7X_LANES]
    lane = lax.broadcasted_iota(jnp.int32, g.shape, 1)
    gates = jnp.where(lane < heads, gcap, bcum)
    gates_t = gates.T

    scale = dk ** -0.5
    nt = (((1,), (1,)), ((), ()))
    tn = (((0,), (0,)), ((), ()))
    for hd in range(heads):
        q = q_ref[:, hd * dk:(hd + 1) * dk]
        k = k_ref[:, hd * dk:(hd + 1) * dk]
        v = v_ref[:, hd * dv:(hd + 1) * dv]
        i_col = gates[:, hd:hd + 1]
        b_col = gates[:, heads + hd:heads + hd + 1]
        i_row = gates_t[hd:hd + 1, :]
        b_row = gates_t[heads + hd:heads + hd + 1, :]
        m_prev = m_ref[hd, :, 0:1]
        c_prev = c_ref[hd]
        n_prev = n_ref[hd]

        dmat = jnp.where(causal, b_col - b_row + i_row, -jnp.inf)
        inter = b_col + m_prev
        m_row = jnp.maximum(jnp.max(dmat, axis=-1, keepdims=True), inter)
        a_inter = jnp.exp(inter - m_row)
        s = lax.dot_general(q, k, nt, preferred_element_type=F32) * scale * jnp.exp(dmat - m_row)
        num = (a_inter * jnp.dot(q, c_prev.astype(BF16), preferred_element_type=F32)
               + jnp.dot(s.astype(BF16), v, preferred_element_type=F32))
        den = (a_inter * jnp.sum(q.astype(F32) * n_prev, axis=-1, keepdims=True)
               + jnp.sum(s, axis=-1, keepdims=True))
        h = num / jnp.maximum(jnp.abs(den), jnp.exp(-m_row))

        b_last = b_col[L - 1:L, :]
        dec = b_last - b_col + i_col
        m_new = jnp.maximum(b_last + m_prev, jnp.max(dec, axis=0, keepdims=True))
        a_old = jnp.exp(b_last + m_prev - m_new)
        kw = k.astype(F32) * (jnp.exp(dec - m_new) * scale)
        c_ref[hd] = a_old * c_prev + lax.dot_general(kw.astype(BF16), v, tn, preferred_element_type=F32)
        n_ref[hd] = a_old * n_prev + jnp.sum(kw, axis=0, keepdims=True)
        m_ref[hd] = jnp.broadcast_to(m_new, m_ref.shape[1:])

        h = h * lax.rsqrt(jnp.mean(h * h, axis=-1, keepdims=True) + NORM_EPS)
        h = h * gain_ref[:, hd * dv:(hd + 1) * dv]
        h = h * _sigmoid(o_ref[:, hd * dv:(hd + 1) * dv].astype(F32))
        out_ref[:, hd * dv:(hd + 1) * dv] = h.astype(out_ref.dtype)


def _mlstm_scan(proj, gates, head_gain, bsz, seq):
    m, width = proj.shape
    d = head_gain.shape[0]
    dv = d // ML_HEADS
    dk = dv // 2
    hk = ML_HEADS * dk
    assert width == 2 * hk + 2 * d and hk * 2 == d
    L = min(ML_CHUNK, seq)
    assert seq % L == 0
    nc = seq // L
    row = lambda b, c: b * nc + c
    blk = 2 * (L * (2 * hk + 2 * d) * 2 + L * V7X_LANES * 4 + d * 4 + L * d * 2) \
        + ML_HEADS * dk * dv * 4 + 16 * L * L * 4 + 8 * L * dv * 4
    return pl.pallas_call(
        functools.partial(_mlstm_kernel, dk=dk, dv=dv),
        out_shape=jax.ShapeDtypeStruct((m, d), BF16),
        grid=(bsz, nc),
        in_specs=[pl.BlockSpec((L, hk), lambda b, c: (row(b, c), 0)),
                  pl.BlockSpec((L, hk), lambda b, c: (row(b, c), 1)),
                  pl.BlockSpec((L, d), lambda b, c: (row(b, c), 1)),
                  pl.BlockSpec((L, d), lambda b, c: (row(b, c), 2)),
                  pl.BlockSpec((L, V7X_LANES), lambda b, c: (row(b, c), 0)),
                  pl.BlockSpec((1, d), lambda b, c: (0, 0))],
        out_specs=pl.BlockSpec((L, d), lambda b, c: (row(b, c), 0)),
        scratch_shapes=[pltpu.VMEM((ML_HEADS, dk, dv), F32),
                        pltpu.VMEM((ML_HEADS, 1, dk), F32),
                        pltpu.VMEM((ML_HEADS, 1, V7X_LANES), F32)],
        compiler_params=_params(("arbitrary", "arbitrary"), blk),
        name="mlstm_scan",
    )(proj, proj, proj, proj, gates, head_gain.reshape(1, d))


def _moba_kernel(q_ref, qnext_ref, k_ref, v_ref, o_ref, kx_ref, vt_ref, km_ref, qx_ref, sown_ref, s_ref, acc_ref, *,
                 nb, bs, dh, hps, grp, qps):
    step = pl.program_id(2)
    dh_ext = vt_ref.shape[2]
    nbp = vt_ref.shape[1]
    nt = (((1,), (1,)), ((), ()))
    cw = grp * bs

    blk_idx = lax.broadcasted_iota(jnp.int32, (nb, bs), 0)
    row_idx = lax.broadcasted_iota(jnp.int32, (dh, bs), 0)
    streams = [(h, r) for h in range(hps) for r in range(qps)]

    def select(q, qb, h):
        km = km_ref[h]
        km_hi = km.astype(BF16)
        r1 = km - km_hi.astype(F32)
        km_mid = r1.astype(BF16)
        km_lo = (r1 - km_mid.astype(F32)).astype(BF16)
        k_own = k_ref[pl.ds(pl.multiple_of(jnp.minimum(qb, nb - 1) * bs, bs), bs), h * dh:(h + 1) * dh]
        g3 = lax.dot_general(jnp.concatenate([km_hi, km_mid, km_lo, k_own], axis=0), q, nt,
                             preferred_element_type=F32)
        gate = g3[0:nb] + g3[nb:2 * nb] + g3[2 * nb:3 * nb]
        removed = blk_idx >= qb
        sel = jnp.zeros((nb, bs), F32)
        for rank in range(MB_TOPK):
            gm = jnp.where(removed, -jnp.inf, gate)
            mx = jnp.max(gm, axis=0, keepdims=True)
            cand = jnp.logical_and(jnp.logical_not(removed), gm == mx)
            idx = jnp.min(jnp.where(cand, blk_idx, nb), axis=0, keepdims=True)
            hit = blk_idx == idx
            sel = jnp.where(jnp.logical_and(hit, qb > rank), 1.0, sel)
            removed = jnp.logical_or(removed, hit)
        sel_pad = jnp.concatenate([sel, jnp.zeros((dh - nb, bs), F32)], axis=0)
        neg = jnp.where(jnp.logical_or(sel_pad > 0.0, row_idx > nb), 0.0, -MB_MASK)
        return jnp.concatenate([q, neg.T.astype(BF16)], axis=1), g3[3 * nb:, :]

    @pl.when(step == 0)
    def _():
        lane = lax.broadcasted_iota(jnp.int32, (bs, dh), 1)

        def prep(n, carry):
            r0 = pl.multiple_of(n * bs, bs)
            onehot = (lane == n).astype(BF16)
            for h in range(hps):
                vb = v_ref[pl.ds(r0, bs), h * dh:(h + 1) * dh].astype(F32)
                vt_ref[h, n, 0:dh, :] = vb.T.astype(BF16)
                vt_ref[h, n, dh:dh_ext, :] = jnp.ones((dh_ext - dh, bs), BF16)
                kb = k_ref[pl.ds(r0, bs), h * dh:(h + 1) * dh]
                kx_ref[h, pl.ds(r0, bs), 0:dh] = kb
                kx_ref[h, pl.ds(r0, bs), dh:2 * dh] = onehot
                km_ref[h, pl.ds(n, 1), :] = jnp.mean(kb.astype(F32), axis=0, keepdims=True)
            return carry
        lax.fori_loop(0, nb, prep, 0)
        masked = (lane == nb).astype(BF16)
        for h in range(hps):
            for n in range(nb, nbp):
                vt_ref[h, n] = jnp.zeros((dh_ext, bs), BF16)
                kx_ref[h, n * bs:(n + 1) * bs, 0:dh] = jnp.zeros((bs, dh), BF16)
                kx_ref[h, n * bs:(n + 1) * bs, dh:2 * dh] = masked
        for t, (h, r) in enumerate(streams):
            qx_ref[t], sown_ref[t] = select(q_ref[r * bs:(r + 1) * bs, h * dh:(h + 1) * dh], r, h)

    kidx = lax.broadcasted_iota(jnp.int32, (bs, bs), 0)
    qidx = lax.broadcasted_iota(jnp.int32, (bs, bs), 1)

    qxs = [qx_ref[t] for t in range(len(streams))]
    m0s = []
    for t, (h, r) in enumerate(streams):
        qb = step * qps + r
        q = q_ref[r * bs:(r + 1) * bs, h * dh:(h + 1) * dh]

        s = jnp.where(kidx <= qidx, sown_ref[t], -jnp.inf)
        m0 = jnp.max(s, axis=0, keepdims=True)
        p = jnp.exp2(s - m0)
        acc_ref[t] = jnp.dot(vt_ref[h, qb], p.astype(BF16), preferred_element_type=F32)
        m0s.append(m0)

        qx_ref[t], sown_ref[t] = select(qnext_ref[r * bs:(r + 1) * bs, h * dh:(h + 1) * dh],
                                        (step + 1) * qps + r, h)

    def score(c, slot):
        r0 = pl.multiple_of(c * cw, cw)
        for t, (h, r) in enumerate(streams):
            s_ref[slot, t] = lax.dot_general(kx_ref[h, pl.ds(r0, cw), :], qxs[t], nt, preferred_element_type=F32)

    def consume(c, slot, ms):
        new_ms = []
        for t, (h, r) in enumerate(streams):
            maxes, pvs = [], []
            for g in range(grp):
                sg = s_ref[slot, t, g * bs:(g + 1) * bs, :]
                mg = jnp.max(sg, axis=0, keepdims=True)
                p = jnp.exp2(sg - mg).astype(BF16)
                pvs.append(jnp.dot(vt_ref[h, c * grp + g], p, preferred_element_type=F32))
                maxes.append(mg)
            mx = ms[t]
            for mg in maxes:
                mx = jnp.maximum(mx, mg)
            acc = acc_ref[t] * jnp.exp2(ms[t] - mx)
            for mg, pv in zip(maxes, pvs):
                acc = acc + pv * jnp.exp2(mg - mx)
            acc_ref[t] = acc
            new_ms.append(mx)
        return tuple(new_ms)

    def body(u, ms):
        c = 2 * u
        score(c + 1, 1)
        ms = consume(c, 0, ms)
        score(c + 2, 0)
        return consume(c + 1, 1, ms)

    n_chunks = lax.div(step * qps + (qps - 1) + (grp - 1), grp)
    score(0, 0)
    ms = lax.fori_loop(0, lax.div(n_chunks, 2), body, tuple(m0s))

    @pl.when(lax.rem(n_chunks, 2) == 1)
    def _():
        consume(n_chunks - 1, 0, ms)

    for t, (h, r) in enumerate(streams):
        acc = acc_ref[t]
        out = acc[0:dh, :] / acc[dh:dh + 1, :]
        o_ref[r * bs:(r + 1) * bs, h * dh:(h + 1) * dh] = out.T.astype(o_ref.dtype)


def _moba_attention(qkv, bsz, seq):
    m, width = qkv.shape
    d = width // 3
    dh = d // MB_HEADS
    bs = MB_BLOCK
    hps = MB_HEADS_PER_STEP
    nb = seq // bs
    grp = MB_GROUP
    nbp = (nb + grp - 1) // grp * grp + 2 * grp
    qps = min(MB_QBLOCKS_PER_STEP, nb)
    assert seq % bs == 0 and dh == V7X_LANES and MB_HEADS % hps == 0 and nb < dh and nb % qps == 0
    dh_ext = dh + V7X_BF16_SUBLANES
    wblk = hps * dh
    nhb = MB_HEADS // hps
    ns = nb // qps
    nstream = hps * qps
    blk = (2 * (2 * qps * bs * wblk * 2 + 2 * seq * wblk * 2) + hps * nbp * bs * (2 * dh + dh_ext) * 2
           + 2 * nstream * grp * bs * bs * 4 + nstream * (4 * grp + 8) * bs * bs * 4)
    return pl.pallas_call(
        functools.partial(_moba_kernel, nb=nb, bs=bs, dh=dh, hps=hps, grp=grp, qps=qps),
        out_shape=jax.ShapeDtypeStruct((m, d), BF16),
        grid=(bsz, nhb, ns),
        in_specs=[pl.BlockSpec((qps * bs, wblk), lambda b, h, i: (b * ns + i, h)),
                  pl.BlockSpec((qps * bs, wblk), lambda b, h, i: (b * ns + jnp.minimum(i + 1, ns - 1), h)),
                  pl.BlockSpec((seq, wblk), lambda b, h, i: (b, nhb + h)),
                  pl.BlockSpec((seq, wblk), lambda b, h, i: (b, 2 * nhb + h))],
        out_specs=pl.BlockSpec((qps * bs, wblk), lambda b, h, i: (b * ns + i, h)),
        scratch_shapes=[pltpu.VMEM((hps, nbp * bs, 2 * dh), BF16),
                        pltpu.VMEM((hps, nbp, dh_ext, bs), BF16),
                        pltpu.VMEM((hps, nb, dh), F32),
                        pltpu.VMEM((nstream, bs, 2 * dh), BF16),
                        pltpu.VMEM((nstream, bs, bs), F32),
                        pltpu.VMEM((2, nstream, grp * bs, bs), F32),
                        pltpu.VMEM((nstream, dh_ext, bs), F32)],
        compiler_params=_params(("arbitrary", "arbitrary", "arbitrary"), blk),
        name="moba_attention",
    )(qkv, qkv, qkv, qkv)


def _ffn_up_kernel(xprev_ref, x_ref, ssprev_ref, ss_ref, wg_ref, wu_ref, cwg_ref, cwu_ref, cbg_ref, cbu_ref, o_ref,
                   xperm_ref, stage_ref, hg_ref, hu_ref, unperm_ref, *, tiles_per_seq):
    tm, d = x_ref.shape
    rows = tm + CONV_HALO
    sub = V7X_SUBLANES
    seg = rows // sub
    tn = o_ref.shape[1]
    nslab = stage_ref.shape[0]

    @pl.when(pl.program_id(1) == 0)
    def _():
        seq_start = (pl.program_id(0) % tiles_per_seq) == 0
        r_prev = jnp.where(seq_start, 0.0, _row_rsqrt(ssprev_ref[...], d))
        r_tile = _row_rsqrt(ss_ref[...], d)
        for g in range(d // (nslab * V7X_LANES)):
            for sl in range(nslab):
                c0 = (g * nslab + sl) * V7X_LANES
                stage_ref[sl, 0:CONV_HALO, :] = xprev_ref[:, c0:c0 + V7X_LANES].astype(F32) * r_prev
                stage_ref[sl, CONV_HALO:, :] = x_ref[:, c0:c0 + V7X_LANES].astype(F32) * r_tile

            def gather(i, carry):
                v = 2 * i
                for sl in range(nslab):
                    c0 = (g * nslab + sl) * V7X_LANES
                    a = stage_ref[sl, pl.ds(v, sub, stride=seg), :]
                    b = stage_ref[sl, pl.ds(v + 1, sub, stride=seg), :]
                    r0 = pl.multiple_of(v * sub, 2 * sub)
                    xperm_ref[pl.ds(r0, 2 * sub), c0:c0 + V7X_LANES] = jnp.concatenate([a, b], axis=0).astype(BF16)
                return carry
            lax.fori_loop(0, seg // 2, gather, 0)

    xp = xperm_ref[...]
    hg_ref[...] = jnp.dot(xp, wg_ref[...].astype(BF16), preferred_element_type=F32)
    hu_ref[...] = jnp.dot(xp, wu_ref[...].astype(BF16), preferred_element_type=F32)

    def conv(h_ref, cw_ref, cb_ref):
        last = pltpu.roll(h_ref[rows - sub:rows, :], 1, 0)
        last2 = pltpu.roll(h_ref[rows - 2 * sub:rows - sub, :], 1, 0)
        back1 = jnp.concatenate([last, h_ref[0:rows - sub, :]], axis=0)
        back2 = jnp.concatenate([last2, last, h_ref[0:rows - 2 * sub, :]], axis=0)
        return (cb_ref[...] + cw_ref[2:3, :] * h_ref[...] + cw_ref[1:2, :] * back1 + cw_ref[0:1, :] * back2)

    gate = conv(hg_ref, cwg_ref, cbg_ref)
    up = conv(hu_ref, cwu_ref, cbu_ref)
    act = gate * _sigmoid(gate) * up
    for v in range(seg):
        for sl in range(tn // V7X_LANES):
            unperm_ref[sl, pl.ds(v, sub, stride=seg), :] = act[v * sub:(v + 1) * sub,
                                                               sl * V7X_LANES:(sl + 1) * V7X_LANES]
    o_ref[...] = jnp.concatenate([unperm_ref[sl, CONV_HALO:, :] for sl in range(tn // V7X_LANES)],
                                 axis=1).astype(o_ref.dtype)


def _ffn_up(h, row_ss, w_up, layer, conv_w, conv_b, seq):
    m, d = h.shape
    f = w_up.shape[-1] // 2
    tm = min(FFN_ROWS, seq)
    tn = min(FFN_COLS, f)
    rows = tm + CONV_HALO
    assert seq % tm == 0 and f % tn == 0 and tm % CONV_HALO == 0 and CONV_WIDTH == 3
    assert rows % (2 * V7X_SUBLANES) == 0 and d % (FFN_STAGE_SLABS * V7X_LANES) == 0 and tn % V7X_LANES == 0
    nj = f // tn
    halo_blocks = tm // CONV_HALO
    blk = (2 * (tm * d * 2 + CONV_HALO * d * 2 + 2 * d * tn * 4 + tm * tn * 2) + 2 * d * tn * 6
           + rows * d * 2 + FFN_STAGE_SLABS * rows * V7X_LANES * 4 + 3 * rows * tn * 4 + 8 * tm * tn * 4)
    return pl.pallas_call(
        functools.partial(_ffn_up_kernel, tiles_per_seq=seq // tm),
        out_shape=jax.ShapeDtypeStruct((m, f), BF16),
        grid=(m // tm, nj),
        in_specs=[pl.BlockSpec((CONV_HALO, d), lambda i, j: (jnp.maximum(i * halo_blocks - 1, 0), 0)),
                  pl.BlockSpec((tm, d), lambda i, j: (i, 0)),
                  pl.BlockSpec((CONV_HALO, V7X_LANES), lambda i, j: (jnp.maximum(i * halo_blocks - 1, 0), 0)),
                  pl.BlockSpec((tm, V7X_LANES), lambda i, j: (i, 0)),
                  pl.BlockSpec((None, d, tn), lambda i, j: (layer, 0, j)),
                  pl.BlockSpec((None, d, tn), lambda i, j: (layer, 0, nj + j)),
                  pl.BlockSpec((CONV_WIDTH, tn), lambda i, j: (0, j)),
                  pl.BlockSpec((CONV_WIDTH, tn), lambda i, j: (0, nj + j)),
                  pl.BlockSpec((1, tn), lambda i, j: (0, j)),
                  pl.BlockSpec((1, tn), lambda i, j: (0, nj + j))],
        out_specs=pl.BlockSpec((tm, tn), lambda i, j: (i, j)),
        scratch_shapes=[pltpu.VMEM((rows, d), BF16),
                        pltpu.VMEM((FFN_STAGE_SLABS, rows, V7X_LANES), F32),
                        pltpu.VMEM((rows, tn), F32),
                        pltpu.VMEM((rows, tn), F32),
                        pltpu.VMEM((tn // V7X_LANES, rows, V7X_LANES), F32)],
        compiler_params=_params(("arbitrary", "arbitrary"), blk),
        name="ffn_up_conv_gate",
    )(h, h, row_ss, row_ss, w_up, w_up, conv_w, conv_w, conv_b.reshape(1, 2 * f), conv_b.reshape(1, 2 * f))


def _conv_ffn(x, xg, row_ss, w_up, layer, conv_w, conv_b, w_down, seq, next_gain=None):
    act = _ffn_up(xg, row_ss, w_up, layer, conv_w, conv_b, seq)
    return _matmul(act, w_down, F32, residual=x, layer=layer, next_gain=next_gain, rows=DOWN_ROWS, cols=DOWN_COLS)


def kernel(x, norm_mix, norm_ffn, a_w_in, a_gate_bias, a_head_norm, a_w_out, b_w_qkv, b_w_out,
           ffn_w_up, ffn_conv_w, ffn_conv_b, ffn_w_down, final_norm):
    bsz, seq, d = x.shape
    m = bsz * seq
    x = x.reshape(m, d)

    w_down = ffn_w_down.astype(BF16)

    w_in_t = jnp.swapaxes(a_w_in, 1, 2)
    n_main = w_in_t.shape[1] - 2 * ML_HEADS
    w_gates = jnp.pad(w_in_t[0, n_main:, :].T, ((0, 0), (0, V7X_LANES - 2 * ML_HEADS)))
    gate_bias = jnp.pad(a_gate_bias[0], (0, V7X_LANES - 2 * ML_HEADS)).reshape(1, V7X_LANES)
    h, gates = _rmsnorm_gates(x, norm_mix[0], w_gates, gate_bias)
    proj = _matmul(h, w_in_t, BF16, n=n_main, layer=0, b_is_nk=True, ring=True)
    mixed = _mlstm_scan(proj, gates, a_head_norm[0], bsz, seq)
    x, xg, row_ss = _matmul(mixed, a_w_out, F32, residual=x, layer=0, next_gain=norm_ffn[0])
    x, xg, row_ss = _conv_ffn(x, xg, row_ss, ffn_w_up, 0, ffn_conv_w[0], ffn_conv_b[0], w_down, seq,
                              next_gain=norm_mix[1])

    q_scale = jnp.where(jnp.arange(3 * d) < d, (d // MB_HEADS) ** -0.5 * LOG2_E, 1.0).astype(F32)
    qkv = _matmul(xg, b_w_qkv, BF16, layer=0, col_scale=q_scale.reshape(1, 3 * d), row_ss=row_ss, ring=True)
    attn = _moba_attention(qkv, bsz, seq)
    x, xg, row_ss = _matmul(attn, b_w_out, F32, residual=x, layer=0, next_gain=norm_ffn[1])
    x = _conv_ffn(x, xg, row_ss, ffn_w_up, 1, ffn_conv_w[1], ffn_conv_b[1], w_down, seq)

    return _rmsnorm(x, final_norm, F32).reshape(bsz, seq, d)
```
